```python
import math
import jax
import jax.numpy as jnp
from jax import lax
import numpy as np

D_MODEL = 4096
BATCH = 2
SEQ = 4096
DEPTH = 2

GRID_W = 64
CTX_LEN = 256
N_Q_HEADS = 16
N_KV_HEADS = 4
HEAD_DIM = 128
Q_GROUP = N_Q_HEADS // N_KV_HEADS
ATT_WIDTH = N_Q_HEADS * HEAD_DIM
KV_WIDTH = N_KV_HEADS * HEAD_DIM
WINDOW = 128
BLOCK = 128
ROPE_BASE = 10000.0
ROPE_AXIS_DIM = HEAD_DIM // 2
ROPE_FREQS = ROPE_AXIS_DIM // 2
MASK_VALUE = -1e30
HY_WIDTH = 2048
HY_ORDER = 2
HY_DIRS = 2
SHORT_CONV = 3
FILTER_BANDS = 16
FILTER_EMB = 1 + 2 * FILTER_BANDS
FILTER_HIDDEN = 64
DECAY_TARGET = 1e-2
FAST_DECAY_PCT = 0.3
SLOW_DECAY_PCT = 1.5
MIN_DECAY = math.log(DECAY_TARGET) / SLOW_DECAY_PCT
MAX_DECAY = math.log(DECAY_TARGET) / FAST_DECAY_PCT
K_OFF = ATT_WIDTH
V_OFF = K_OFF + KV_WIDTH
HY_OFF = V_OFF + KV_WIDTH
GA_OFF = HY_OFF + (HY_ORDER + 1) * HY_WIDTH
GH_OFF = GA_OFF + D_MODEL
IN_WIDTH = GH_OFF + D_MODEL
D_FF = 11008
N_EXPERTS = 8
TOP_K = 2
D_FF_EXPERT = 3584
MOE_BLOCK = 256
N_DENSE = (DEPTH + 1) // 2
N_MOE = DEPTH // 2
RMS_EPS = 1e-6
F32 = jnp.float32

kernel_name = 'hybrid_dit_gqa_hyena_moe'


def rmsnorm(x, g):
    xf = x.astype(F32)
    y = xf * lax.rsqrt(jnp.mean(xf * xf, axis=-1, keepdims=True) + RMS_EPS)
    return (y * g.astype(F32)).astype(x.dtype)


def modulate(x, shift, scale):
    return x * (1 + scale) + shift


def split_heads(t, n_heads):
    return t.reshape(t.shape[:-1] + (n_heads, HEAD_DIM))


def axial_rope_tables(rows):
    row = jnp.repeat(jnp.arange(rows), GRID_W).astype(F32)
    col = jnp.tile(jnp.arange(GRID_W), rows).astype(F32)
    inv = ROPE_BASE ** (-jnp.arange(ROPE_FREQS, dtype=F32) / ROPE_FREQS)
    ang = jnp.stack([row[:, None] * inv, col[:, None] * inv], axis=1)
    return jnp.cos(ang), jnp.sin(ang)


def apply_axial_rope(x, cos, sin):
    b, n, h, _ = x.shape
    xr = x.astype(F32).reshape(b, n, h, 2, 2, ROPE_FREQS)
    x1, x2 = xr[..., 0, :], xr[..., 1, :]
    cs, sn = cos[None, :, None], sin[None, :, None]
    out = jnp.stack([x1 * cs - x2 * sn, x1 * sn + x2 * cs], axis=-2)
    return out.reshape(b, n, h, HEAD_DIM).astype(x.dtype)


def latent_window_attention(q, k, v, k_c, v_c, sink):
    b, s = q.shape[:2]
    nb = s // BLOCK
    scale = HEAD_DIM ** -0.5
    qb = q.reshape(b, nb, BLOCK, N_KV_HEADS, Q_GROUP, HEAD_DIM)
    pad = ((0, 0), (BLOCK, BLOCK), (0, 0), (0, 0))
    kp, vp = jnp.pad(k, pad), jnp.pad(v, pad)

    def band(t):
        return jnp.concatenate([t[:, j * BLOCK:j * BLOCK + s].reshape(b, nb, BLOCK, N_KV_HEADS, HEAD_DIM)
                                for j in range(3)], axis=2)

    kb, vb = band(kp), band(vp)
    s_loc = jnp.einsum('bnqhgd,bnkhd->bnhgqk', qb, kb).astype(F32) * scale
    q_pos = jnp.arange(nb)[:, None, None] * BLOCK + jnp.arange(BLOCK)[None, :, None]
    k_pos = jnp.arange(nb)[:, None, None] * BLOCK - BLOCK + jnp.arange(3 * BLOCK)[None, None, :]
    valid = (jnp.abs(q_pos - k_pos) <= WINDOW) & (k_pos >= 0) & (k_pos < s)
    s_loc = jnp.where(valid[None, :, None, None], s_loc, MASK_VALUE)
    s_ctx = jnp.einsum('bnqhgd,bchd->bnhgqc', qb, k_c).astype(F32) * scale
    s_sink = jnp.broadcast_to(sink.astype(F32).reshape(1, 1, N_KV_HEADS, Q_GROUP, 1, 1), s_ctx.shape[:-1] + (1,))
    p = jax.nn.softmax(jnp.concatenate([s_loc, s_ctx, s_sink], axis=-1), axis=-1)
    n_loc, n_ctx = 3 * BLOCK, k_c.shape[1]
    p_loc = p[..., :n_loc].astype(v.dtype)
    p_ctx = p[..., n_loc:n_loc + n_ctx].astype(v.dtype)
    o = jnp.einsum('bnhgqk,bnkhd->bnqhgd', p_loc, vb) + jnp.einsum('bnhgqc,bchd->bnqhgd', p_ctx, v_c)
    return o.reshape(b, s, ATT_WIDTH)


def context_attention(q, k, v, sink):
    b, n = q.shape[:2]
    scale = HEAD_DIM ** -0.5
    qg = q.reshape(b, n, N_KV_HEADS, Q_GROUP, HEAD_DIM)
    sc = jnp.einsum('bqhgd,bkhd->bhgqk', qg, k).astype(F32) * scale
    s_sink = jnp.broadcast_to(sink.astype(F32).reshape(1, N_KV_HEADS, Q_GROUP, 1, 1), sc.shape[:-1] + (1,))
    p = jax.nn.softmax(jnp.concatenate([sc, s_sink], axis=-1), axis=-1)
    o = jnp.einsum('bhgqk,bkhd->bqhgd', p[..., :n].astype(v.dtype), v)
    return o.reshape(b, n, ATT_WIDTH)


def short_conv(u, w, bias):
    n = u.shape[1]
    half = SHORT_CONV // 2
    up = jnp.pad(u, ((0, 0), (half, SHORT_CONV - 1 - half), (0, 0)))
    out = up[:, 0:n] * w[0]
    for j in range(1, SHORT_CONV):
        out = out + up[:, j:j + n] * w[j]
    return out + bias


def hyena_filter_spectrum(n, w1, b1, w2, b2, w3, b3, freq, w_out):
    t = jnp.linspace(0.0, 1.0, n, dtype=F32)[:, None]
    w = 2.0 * math.pi * jnp.arange(n, dtype=F32)[:, None] / n
    bands = jnp.linspace(1e-4, FILTER_BANDS - 1, FILTER_BANDS, dtype=F32)[None, :]
    z = jnp.concatenate([t, jnp.cos(bands * w), -jnp.sin(bands * w)], axis=-1)
    fr = freq.astype(F32)
    h = jnp.sin(fr * (z @ w1.astype(F32) + b1.astype(F32)))
    h = jnp.sin(fr * (h @ w2.astype(F32) + b2.astype(F32)))
    h = jnp.sin(fr * (h @ w3.astype(F32) + b3.astype(F32)))
    h = (h @ w_out.astype(F32)).reshape(n, HY_ORDER, HY_DIRS, HY_WIDTH)
    deltas = jnp.abs(jnp.linspace(MIN_DECAY, MAX_DECAY, HY_WIDTH, dtype=F32))
    h = h * jnp.exp(-t * deltas)[:, None, None, :]
    fwd, bwd = h[:, :, 0], h[:, :, 1]
    full = jnp.concatenate([fwd, jnp.zeros((1, HY_ORDER, HY_WIDTH), F32), bwd[1:][::-1]], axis=0)
    return jnp.fft.rfft(full, axis=0)


def hyena_mixer(u, conv_w, conv_b, spec, hy_bias):
    n = u.shape[1]
    uc = short_conv(u, conv_w, conv_b).astype(F32)
    v, x1, x2 = jnp.split(uc, 3, axis=-1)
    bias = hy_bias.astype(F32)
    z = v
    for o, gate in enumerate((x1, x2)):
        zf = jnp.fft.rfft(z, n=2 * n, axis=1)
        conv = jnp.fft.irfft(zf * spec[None, :, o], n=2 * n, axis=1)[:, :n]
        z = gate * (conv + z * bias[o])
    return z.astype(u.dtype)


def gated_merge(u, att, hy, w_ao, w_ho, w_o):
    ga = jax.nn.sigmoid(u[..., GA_OFF:GH_OFF])
    gh = jax.nn.sigmoid(u[..., GH_OFF:IN_WIDTH])
    return (ga * (att @ w_ao) + gh * (hy @ w_ho)) @ w_o


def swiglu(h, w1, w3, w2):
    return (jax.nn.silu(h @ w1) * (h @ w3)) @ w2


def moe_swiglu(h, w_router, w1, w3, w2):
    t, d = h.shape
    logits = (h @ w_router).astype(F32)
    top_logit, top_idx = lax.top_k(logits, TOP_K)
    gate = jax.nn.softmax(top_logit, axis=-1)
    n_assign = t * TOP_K
    flat_e = top_idx.reshape(-1)
    flat_tok = jnp.repeat(jnp.arange(t, dtype=jnp.int32), TOP_K)
    flat_gate = gate.reshape(-1)
    order = jnp.argsort(flat_e)
    e_sorted = flat_e[order]
    counts = jnp.bincount(flat_e, length=N_EXPERTS)
    padded = (counts + MOE_BLOCK - 1) // MOE_BLOCK * MOE_BLOCK
    start = jnp.cumsum(counts) - counts
    pad_end = jnp.cumsum(padded)
    pad_start = pad_end - padded
    dest = pad_start[e_sorted] + jnp.arange(n_assign) - start[e_sorted]
    n_blocks = -(-n_assign // MOE_BLOCK) + N_EXPERTS
    n_slots = n_blocks * MOE_BLOCK
    slot_tok = jnp.full((n_slots,), t, jnp.int32).at[dest].set(flat_tok[order])
    slot_gate = jnp.zeros((n_slots,), F32).at[dest].set(flat_gate[order])
    block_expert = jnp.minimum(jnp.searchsorted(pad_end, jnp.arange(n_blocks) * MOE_BLOCK, side='right'),
                               N_EXPERTS - 1)
    h_pad = jnp.concatenate([h, jnp.zeros((1, d), h.dtype)], axis=0)
    xs = h_pad[slot_tok].reshape(n_blocks, MOE_BLOCK, d)

    def expert_block(args):
        xb, e = args
        return (jax.nn.silu(xb @ w1[e]) * (xb @ w3[e])) @ w2[e]

    ys = lax.map(expert_block, (xs, block_expert)).reshape(n_slots, d)
    ys = ys * slot_gate[:, None].astype(ys.dtype)
    out = jnp.zeros((t + 1, d), h.dtype).at[slot_tok].add(ys)
    return out[:t]


def channel_mixer(h, layer, ffn_w1, ffn_w3, ffn_w2, moe_router, moe_w1, moe_w3, moe_w2):
    i = layer // 2
    if layer % 2 == 0:
        return swiglu(h, ffn_w1[i], ffn_w3[i], ffn_w2[i])
    shp = h.shape
    out = moe_swiglu(h.reshape(-1, shp[-1]), moe_router[i], moe_w1[i], moe_w3[i], moe_w2[i])
    return out.reshape(shp)


def setup_inputs(seed: int = 0) -> dict:
    key = jax.random.key(seed)
    keys = jax.random.split(key, 32)

    def nrm(i, shape, std):
        return jax.random.normal(keys[i], shape, F32) * std

    L, D = DEPTH, D_MODEL
    return {
        'x': nrm(0, (BATCH, SEQ, D), 1.0),
        'c': nrm(1, (BATCH, D), 1.0),
        'ctx': nrm(2, (BATCH, CTX_LEN, D), 1.0),
        'c_ctx': nrm(3, (D,), 1.0),
        'w_ada': nrm(4, (L, D, 6 * D), 0.5 * D ** -0.5),
        'b_ada': nrm(5, (L, 6 * D), 0.02),
        'norm_g': 1.0 + nrm(6, (L, 4, D), 0.02),
        'w_in': nrm(7, (L, D, IN_WIDTH), D ** -0.5),
        'attn_sink': nrm(8, (L, N_Q_HEADS), 0.5),
        'conv_w': nrm(9, (L, SHORT_CONV, (HY_ORDER + 1) * HY_WIDTH), SHORT_CONV ** -0.5),
        'conv_b': nrm(10, (L, (HY_ORDER + 1) * HY_WIDTH), 0.02),
        'filt_w1': nrm(11, (L, FILTER_EMB, FILTER_HIDDEN), FILTER_EMB ** -0.5),
        'filt_b1': nrm(12, (L, FILTER_HIDDEN), 0.02),
        'filt_w2': nrm(13, (L, FILTER_HIDDEN, FILTER_HIDDEN), FILTER_HIDDEN ** -0.5),
        'filt_b2': nrm(14, (L, FILTER_HIDDEN), 0.02),
        'filt_w3': nrm(15, (L, FILTER_HIDDEN, FILTER_HIDDEN), FILTER_HIDDEN ** -0.5),
        'filt_b3': nrm(16, (L, FILTER_HIDDEN), 0.02),
        'filt_freq': 1.0 + nrm(17, (L, FILTER_HIDDEN), 0.02),
        'filt_w_out': nrm(18, (L, FILTER_HIDDEN, HY_ORDER * HY_DIRS * HY_WIDTH), 0.01),
        'hyena_bias': nrm(19, (L, HY_ORDER, HY_WIDTH), 0.5),
        'w_attn_out': nrm(20, (L, ATT_WIDTH, D), ATT_WIDTH ** -0.5),
        'w_hyena_out': nrm(21, (L, HY_WIDTH, D), HY_WIDTH ** -0.5),
        'w_out': nrm(22, (L, D, D), D ** -0.5),
        'ffn_w1': nrm(23, (N_DENSE, D, D_FF), D ** -0.5),
        'ffn_w3': nrm(24, (N_DENSE, D, D_FF), D ** -0.5),
        'ffn_w2': nrm(25, (N_DENSE, D_FF, D), D_FF ** -0.5),
        'moe_router': nrm(26, (N_MOE, D, N_EXPERTS), D ** -0.5),
        'moe_w1': nrm(27, (N_MOE, N_EXPERTS, D, D_FF_EXPERT), D ** -0.5),
        'moe_w3': nrm(28, (N_MOE, N_EXPERTS, D, D_FF_EXPERT), D ** -0.5),
        'moe_w2': nrm(29, (N_MOE, N_EXPERTS, D_FF_EXPERT, D), D_FF_EXPERT ** -0.5),
    }


def reference(x, c, ctx, c_ctx, w_ada, b_ada, norm_g, w_in, attn_sink, conv_w, conv_b,
              filt_w1, filt_b1, filt_w2, filt_b2, filt_w3, filt_b3, filt_freq, filt_w_out, hyena_bias,
              w_attn_out, w_hyena_out, w_out, ffn_w1, ffn_w3, ffn_w2,
              moe_router, moe_w1, moe_w3, moe_w2):
    b, s, d = x.shape
    n_ctx = ctx.shape[1]
    rows = s // GRID_W
    rope_cos, rope_sin = axial_rope_tables(rows)
    xc = ctx
    for l in range(DEPTH):
        last = l == DEPTH - 1
        mod = jax.nn.silu(c) @ w_ada[l] + b_ada[l]
        mod_c = jax.nn.silu(c_ctx) @ w_ada[l] + b_ada[l]
        sh1, sc1, gt1, sh2, sc2, gt2 = jnp.split(mod[:, None, :], 6, axis=-1)
        csh1, csc1, cgt1, csh2, csc2, cgt2 = jnp.split(mod_c, 6)
        g_pre1, g_post1, g_pre2, g_post2 = norm_g[l]
        filt = (filt_w1[l], filt_b1[l], filt_w2[l], filt_b2[l], filt_w3[l], filt_b3[l], filt_freq[l], filt_w_out[l])

        h = modulate(rmsnorm(x, g_pre1), sh1, sc1)
        hc = modulate(rmsnorm(xc, g_pre1), csh1, csc1)
        u = h @ w_in[l]
        if last:
            kv_c = hc @ w_in[l][:, K_OFF:HY_OFF]
        else:
            u_c = hc @ w_in[l]
            kv_c = u_c[..., K_OFF:HY_OFF]
        k_c = split_heads(kv_c[..., :KV_WIDTH], N_KV_HEADS)
        v_c = split_heads(kv_c[..., KV_WIDTH:], N_KV_HEADS)

        q = apply_axial_rope(split_heads(u[..., :K_OFF], N_Q_HEADS), rope_cos, rope_sin)
        k = apply_axial_rope(split_heads(u[..., K_OFF:V_OFF], N_KV_HEADS), rope_cos, rope_sin)
        v = split_heads(u[..., V_OFF:HY_OFF], N_KV_HEADS)
        att = latent_window_attention(q, k, v, k_c, v_c, attn_sink[l])
        hy = hyena_mixer(u[..., HY_OFF:GA_OFF], conv_w[l], conv_b[l], hyena_filter_spectrum(s, *filt), hyena_bias[l])
        y = gated_merge(u, att, hy, w_attn_out[l], w_hyena_out[l], w_out[l])
        x = x + gt1 * rmsnorm(y, g_post1)
        if not last:
            q_c = split_heads(u_c[..., :K_OFF], N_Q_HEADS)
            att_c = context_attention(q_c, k_c, v_c, attn_sink[l])
            hy_c = hyena_mixer(u_c[..., HY_OFF:GA_OFF], conv_w[l], conv_b[l],
                               hyena_filter_spectrum(n_ctx, *filt), hyena_bias[l])
            y_c = gated_merge(u_c, att_c, hy_c, w_attn_out[l], w_hyena_out[l], w_out[l])
            xc = xc + cgt1 * rmsnorm(y_c, g_post1)

        h2 = modulate(rmsnorm(x, g_pre2), sh2, sc2)
        f = channel_mixer(h2, l, ffn_w1, ffn_w3, ffn_w2, moe_router, moe_w1, moe_w3, moe_w2)
        x = x + gt2 * rmsnorm(f, g_post2)
        if not last:
            h2c = modulate(rmsnorm(xc, g_pre2), csh2, csc2)
            fc = channel_mixer(h2c, l, ffn_w1, ffn_w3, ffn_w2, moe_router, moe_w1, moe_w3, moe_w2)
            xc = xc + cgt2 * rmsnorm(fc, g_post2)
    return x
```

```python
import functools
import math

import numpy as np
import jax
import jax.numpy as jnp
from jax import lax
from jax.experimental import pallas as pl
from jax.experimental.pallas import tpu as pltpu

F32 = jnp.float32
BF16 = jnp.bfloat16

GRID_W = 64
N_Q_HEADS = 16
N_KV_HEADS = 4
HEAD_DIM = 128
Q_GROUP = N_Q_HEADS // N_KV_HEADS
ATT_WIDTH = N_Q_HEADS * HEAD_DIM
KV_WIDTH = N_KV_HEADS * HEAD_DIM
BLOCK = 128
ROPE_BASE = 10000.0
ROPE_FREQS = HEAD_DIM // 4
MASK_VALUE = -1e30
HY_WIDTH = 2048
HY_ORDER = 2
SHORT_CONV = 3
FILTER_BANDS = 16
FILTER_HIDDEN = 64
DECAY_TARGET = 1e-2
MIN_DECAY = math.log(DECAY_TARGET) / 1.5
MAX_DECAY = math.log(DECAY_TARGET) / 0.3
K_OFF = ATT_WIDTH
V_OFF = K_OFF + KV_WIDTH
HY_OFF = V_OFF + KV_WIDTH
N_EXPERTS = 8
TOP_K = 2
MOE_BLOCK = 256
RMS_EPS = 1e-6

LANES = 128
VMEM_LIMIT_BYTES = 56 * 1024 * 1024

FFT_NO = 128
FFT_NI = 64
HY_TC = 128


def _cparams(n_axes):
    return pltpu.CompilerParams(dimension_semantics=("arbitrary",) * n_axes,
                                vmem_limit_bytes=VMEM_LIMIT_BYTES)


def _ada_kernel(c_ref, w_ref, b_ref, o_ref):
    a = c_ref[...]
    a = a * jax.nn.sigmoid(a)
    o_ref[0] = jnp.dot(a, w_ref[0], preferred_element_type=F32,
                       precision=lax.Precision.HIGHEST) + b_ref[0]


def _ada(cond, w_ada, b_ada, tn=512):
    n_layers, d, n = w_ada.shape
    rows = cond.shape[0]
    return pl.pallas_call(
        _ada_kernel,
        grid=(n_layers, n // tn),
        in_specs=[pl.BlockSpec((rows, d), lambda l, j: (0, 0)),
                  pl.BlockSpec((1, d, tn), lambda l, j: (l, 0, j)),
                  pl.BlockSpec((1, 1, tn), lambda l, j: (l, 0, j))],
        out_specs=pl.BlockSpec((1, rows, tn), lambda l, j: (l, 0, j)),
        out_shape=jax.ShapeDtypeStruct((n_layers, rows, n), F32),
        compiler_params=_cparams(2),
        name="ada_mod",
    )(cond, w_ada, b_ada.reshape(n_layers, 1, n))


def _rms(x):
    return x * lax.rsqrt(jnp.mean(x * x, axis=-1, keepdims=True) + RMS_EPS)


def _norm_mod_kernel(x_ref, g_ref, sh_ref, sc_ref, o_ref):
    y = _rms(x_ref[...]) * g_ref[...]
    o_ref[...] = (y * (1.0 + sc_ref[0]) + sh_ref[0]).astype(o_ref.dtype)


def _group_of_tile(tr, seq):
    return lambda i: (jnp.minimum((i * tr) // seq, 2), 0, 0)


def _norm_mod(x_all, g, sh, sc, seq, tr=256):
    t, d = x_all.shape
    grp = _group_of_tile(tr, seq)
    return pl.pallas_call(
        _norm_mod_kernel,
        grid=(t // tr,),
        in_specs=[pl.BlockSpec((tr, d), lambda i: (i, 0)),
                  pl.BlockSpec((1, d), lambda i: (0, 0)),
                  pl.BlockSpec((1, 1, d), grp),
                  pl.BlockSpec((1, 1, d), grp)],
        out_specs=pl.BlockSpec((tr, d), lambda i: (i, 0)),
        out_shape=jax.ShapeDtypeStruct((t, d), BF16),
        compiler_params=_cparams(1),
        name="norm_mod",
    )(x_all, g.reshape(1, d), sh, sc)


def _post_kernel(x_ref, y_ref, gt_ref, gpost_ref, *rest, with_next, with_router):
    xn = x_ref[...] + gt_ref[0] * (_rms(y_ref[...].astype(F32)) * gpost_ref[...])
    if not with_next:
        (xo_ref,) = rest
        xo_ref[...] = xn
        return
    if with_router:
        gpre_ref, sh_ref, sc_ref, wr_ref, xo_ref, ho_ref, lg_ref = rest
    else:
        gpre_ref, sh_ref, sc_ref, xo_ref, ho_ref = rest
    xo_ref[...] = xn
    h = (_rms(xn) * gpre_ref[...]) * (1.0 + sc_ref[0]) + sh_ref[0]
    ho_ref[...] = h.astype(ho_ref.dtype)
    if with_router:
        lg_ref[...] = jnp.dot(h, wr_ref[...], preferred_element_type=F32,
                              precision=lax.Precision.HIGHEST)


def _post(x_all, y, gt, g_post, seq, rows, nxt=None, router=None, tr=256):
    t, d = x_all.shape
    grp = _group_of_tile(tr, seq)
    row = pl.BlockSpec((tr, d), lambda i: (i, 0))
    vec = pl.BlockSpec((1, d), lambda i: (0, 0))
    mod = pl.BlockSpec((1, 1, d), grp)
    in_specs = [row, row, mod, vec]
    args = [x_all, y, gt, g_post.reshape(1, d)]
    out_specs = [row]
    out_shape = [jax.ShapeDtypeStruct((rows, d), F32)]
    if nxt is not None:
        g_pre, sh, sc = nxt
        in_specs += [vec, mod, mod]
        args += [g_pre.reshape(1, d), sh, sc]
        out_specs.append(row)
        out_shape.append(jax.ShapeDtypeStruct((rows, d), BF16))
        if router is not None:
            in_specs.append(pl.BlockSpec((d, LANES), lambda i: (0, 0)))
            args.append(router)
            out_specs.append(pl.BlockSpec((tr, LANES), lambda i: (i, 0)))
            out_shape.append(jax.ShapeDtypeStruct((rows, LANES), F32))
    return pl.pallas_call(
        functools.partial(_post_kernel, with_next=nxt is not None, with_router=router is not None),
        grid=(rows // tr,),
        in_specs=in_specs,
        out_specs=out_specs,
        out_shape=out_shape,
        compiler_params=_cparams(1),
        name="post_norm",
    )(*args)


def _weight_changed(be_ref, m):
    return jnp.logical_or(m == 0, be_ref[m] != be_ref[jnp.maximum(m - 1, 0)])


def _gmm_kernel(be_ref, nv_ref, a_ref, w_ref, o_ref, wb_ref):
    m = pl.program_id(1)

    @pl.when(m < nv_ref[0])
    def _():
        @pl.when(_weight_changed(be_ref, m))
        def _():
            wb_ref[...] = w_ref[0].astype(BF16)

        o_ref[...] = jnp.dot(a_ref[...], wb_ref[...],
                             preferred_element_type=F32).astype(o_ref.dtype)


def _swiglu_kernel(be_ref, nv_ref, a_ref, w1_ref, w3_ref, o_ref, w1b_ref, w3b_ref):
    m = pl.program_id(1)

    @pl.when(m < nv_ref[0])
    def _():
        @pl.when(_weight_changed(be_ref, m))
        def _():
            w1b_ref[...] = w1_ref[0].astype(BF16)
            w3b_ref[...] = w3_ref[0].astype(BF16)

        a = a_ref[...]
        g = jnp.dot(a, w1b_ref[...], preferred_element_type=F32)
        u = jnp.dot(a, w3b_ref[...], preferred_element_type=F32)
        o_ref[...] = (g * jax.nn.sigmoid(g) * u).astype(o_ref.dtype)


def _gmm(a, ws, be, nvalid, *, k, n, tm, tn, out_dtype, w_col_off=0, rows=None, name="gmm"):
    rows = a.shape[0] if rows is None else rows
    assert rows % tm == 0 and n % tn == 0 and w_col_off % tn == 0
    off = w_col_off // tn

    def a_map(j, m, be_ref, nv_ref):
        return (jnp.minimum(m, nv_ref[0] - 1), 0)

    def w_map(j, m, be_ref, nv_ref):
        return (be_ref[jnp.minimum(m, nv_ref[0] - 1)], 0, j + off)

    def o_map(j, m, be_ref, nv_ref):
        return (m, j)

    kernel = _gmm_kernel if len(ws) == 1 else _swiglu_kernel
    return pl.pallas_call(
        kernel,
        grid_spec=pltpu.PrefetchScalarGridSpec(
            num_scalar_prefetch=2,
            grid=(n // tn, rows // tm),
            in_specs=[pl.BlockSpec((tm, k), a_map)] + [pl.BlockSpec((1, k, tn), w_map)] * len(ws),
            out_specs=pl.BlockSpec((tm, tn), o_map),
            scratch_shapes=[pltpu.VMEM((k, tn), BF16)] * len(ws)),
        out_shape=jax.ShapeDtypeStruct((rows, n), out_dtype),
        compiler_params=_cparams(2),
        name=name,
    )(be, nvalid, a, *ws)


def _dense_ids(rows, tm, idx):
    nb = rows // tm
    return jnp.full((nb,), idx, jnp.int32), jnp.full((1,), nb, jnp.int32)


def _merge_kernel(att_ref, hy_ref, ga_ref, gh_ref, wa_ref, wh_ref, o_ref, wab_ref, whb_ref):
    @pl.when(pl.program_id(1) == 0)
    def _():
        wab_ref[...] = wa_ref[0].astype(BF16)
        whb_ref[...] = wh_ref[0].astype(BF16)

    pa = jnp.dot(att_ref[...], wab_ref[...], preferred_element_type=F32)
    ph = jnp.dot(hy_ref[...], whb_ref[...], preferred_element_type=F32)
    ga = jax.nn.sigmoid(ga_ref[...].astype(F32))
    gh = jax.nn.sigmoid(gh_ref[...].astype(F32))
    o_ref[...] = (ga * pa + gh * ph).astype(o_ref.dtype)


def _merge(att, hy, u, w_ao, w_ho, layer, rows, ga_off, gh_off, tm, tn=512):
    ka, d = w_ao.shape[1:]
    kh = w_ho.shape[1]
    return pl.pallas_call(
        _merge_kernel,
        grid=(d // tn, rows // tm),
        in_specs=[pl.BlockSpec((tm, ka), lambda j, m: (m, 0)),
                  pl.BlockSpec((tm, kh), lambda j, m: (m, 0)),
                  pl.BlockSpec((tm, tn), lambda j, m: (m, ga_off // tn + j)),
                  pl.BlockSpec((tm, tn), lambda j, m: (m, gh_off // tn + j)),
                  pl.BlockSpec((1, ka, tn), lambda j, m: (layer, 0, j)),
                  pl.BlockSpec((1, kh, tn), lambda j, m: (layer, 0, j))],
        out_specs=pl.BlockSpec((tm, tn), lambda j, m: (m, j)),
        out_shape=jax.ShapeDtypeStruct((rows, d), BF16),
        scratch_shapes=[pltpu.VMEM((ka, tn), BF16), pltpu.VMEM((kh, tn), BF16)],
        compiler_params=_cparams(2),
        name="gated_merge",
    )(att, hy, u, u, w_ao, w_ho)


def _rope_tables(seq):
    rows = seq // GRID_W
    row = jnp.repeat(jnp.arange(rows), GRID_W).astype(F32)
    col = jnp.tile(jnp.arange(GRID_W), rows).astype(F32)
    inv = ROPE_BASE ** (-jnp.arange(ROPE_FREQS, dtype=F32) / ROPE_FREQS)
    ang = jnp.stack([row[:, None] * inv, col[:, None] * inv], axis=1)
    cos, sin = jnp.cos(ang), jnp.sin(ang)
    zero = jnp.zeros_like(sin)
    cos_t = jnp.stack([cos, cos], axis=2).reshape(seq, HEAD_DIM)
    s_lo = jnp.stack([-sin, zero], axis=2).reshape(seq, HEAD_DIM)
    s_hi = jnp.stack([zero, sin], axis=2).reshape(seq, HEAD_DIM)
    return cos_t, s_lo, s_hi


def _rope_kernel(x_ref, c_ref, lo_ref, hi_ref, o_ref):
    heads = x_ref.shape[1] // HEAD_DIM
    c, lo, hi = c_ref[...], lo_ref[...], hi_ref[...]
    for h in range(heads):
        sl = slice(h * HEAD_DIM, (h + 1) * HEAD_DIM)
        x = x_ref[:, sl].astype(F32)
        up = pltpu.roll(x, HEAD_DIM - ROPE_FREQS, 1)
        dn = pltpu.roll(x, ROPE_FREQS, 1)
        o_ref[:, sl] = (x * c + up * lo + dn * hi).astype(o_ref.dtype)


def _rope(u, tables, n_latent, seq, width, tr=256, tw=512):
    nseq = seq // tr
    tab = pl.BlockSpec((tr, HEAD_DIM), lambda i, j: (i % nseq, 0))
    return pl.pallas_call(
        _rope_kernel,
        grid=(n_latent // tr, width // tw),
        in_specs=[pl.BlockSpec((tr, tw), lambda i, j: (i, j)), tab, tab, tab],
        out_specs=pl.BlockSpec((tr, tw), lambda i, j: (i, j)),
        out_shape=jax.ShapeDtypeStruct((n_latent, width), BF16),
        compiler_params=_cparams(2),
        name="rope",
    )(u, *tables)


def _scores(q, k):
    return lax.dot_general(q, k, (((1,), (1,)), ((), ())), preferred_element_type=F32)


def _win_attn_kernel(sink_ref, q_ref, kp_ref, kc_ref, kn_ref, vp_ref, vc_ref, vn_ref,
                     kx_ref, vx_ref, o_ref):
    n = pl.program_id(1)
    h = pl.program_id(2)
    nb = pl.num_programs(1)
    scale = HEAD_DIM ** -0.5
    qi = lax.broadcasted_iota(jnp.int32, (BLOCK, BLOCK), 0)
    kj = lax.broadcasted_iota(jnp.int32, (BLOCK, BLOCK), 1)
    ok_prev = jnp.logical_and(kj >= qi, n > 0)
    ok_next = jnp.logical_and(kj <= qi, n < nb - 1)
    kp, kc, kn, kx = kp_ref[...], kc_ref[...], kn_ref[...], kx_ref[...]
    vp, vc, vn, vx = vp_ref[...], vc_ref[...], vn_ref[...], vx_ref[...]
    for g in range(Q_GROUP):
        sl = slice(g * HEAD_DIM, (g + 1) * HEAD_DIM)
        q = q_ref[:, sl]
        sp = jnp.where(ok_prev, _scores(q, kp) * scale, MASK_VALUE)
        sc = _scores(q, kc) * scale
        sn = jnp.where(ok_next, _scores(q, kn) * scale, MASK_VALUE)
        sx = _scores(q, kx) * scale
        sink = sink_ref[h, g]
        mx = jnp.maximum(jnp.maximum(jnp.max(sp, axis=-1, keepdims=True),
                                     jnp.max(sc, axis=-1, keepdims=True)),
                         jnp.maximum(jnp.max(sn, axis=-1, keepdims=True),
                                     jnp.max(sx, axis=-1, keepdims=True)))
        mx = jnp.maximum(mx, sink)
        pp, pc, pn, px = jnp.exp(sp - mx), jnp.exp(sc - mx), jnp.exp(sn - mx), jnp.exp(sx - mx)
        den = (jnp.sum(pp, axis=-1, keepdims=True) + jnp.sum(pc, axis=-1, keepdims=True)
               + jnp.sum(pn, axis=-1, keepdims=True) + jnp.sum(px, axis=-1, keepdims=True)
               + jnp.exp(sink - mx))
        o = (jnp.dot(pp.astype(BF16), vp, preferred_element_type=F32)
             + jnp.dot(pc.astype(BF16), vc, preferred_element_type=F32)
             + jnp.dot(pn.astype(BF16), vn, preferred_element_type=F32)
             + jnp.dot(px.astype(BF16), vx, preferred_element_type=F32))
        o_ref[:, sl] = (o / den).astype(o_ref.dtype)


def _win_attn(qk, u, sink, batch, seq, n_ctx):
    nb = seq // BLOCK
    hd = HEAD_DIM
    qw = Q_GROUP * hd
    kcol = K_OFF // hd
    vcol = V_OFF // hd
    ctx_blk = (batch * seq) // n_ctx

    def blk(shift, col):
        def index(b, n, h):
            return (b * nb + jnp.clip(n + shift, 0, nb - 1), col + h)
        return pl.BlockSpec((BLOCK, hd), index)

    return pl.pallas_call(
        _win_attn_kernel,
        grid=(batch, nb, N_KV_HEADS),
        in_specs=[pl.BlockSpec(memory_space=pltpu.SMEM),
                  pl.BlockSpec((BLOCK, qw), lambda b, n, h: (b * nb + n, h)),
                  blk(-1, kcol), blk(0, kcol), blk(1, kcol),
                  blk(-1, vcol), blk(0, vcol), blk(1, vcol),
                  pl.BlockSpec((n_ctx, hd), lambda b, n, h: (ctx_blk + b, kcol + h)),
                  pl.BlockSpec((n_ctx, hd), lambda b, n, h: (ctx_blk + b, vcol + h))],
        out_specs=pl.BlockSpec((BLOCK, qw), lambda b, n, h: (b * nb + n, h)),
        out_shape=jax.ShapeDtypeStruct((batch * seq, ATT_WIDTH), BF16),
        compiler_params=_cparams(3),
        name="window_attention",
    )(sink.reshape(N_KV_HEADS, Q_GROUP), qk, qk, qk, qk, u, u, u, u, u)


def _ctx_attn_kernel(sink_ref, q_ref, k_ref, v_ref, o_ref):
    h = pl.program_id(1)
    scale = HEAD_DIM ** -0.5
    k, v = k_ref[...], v_ref[...]
    for g in range(Q_GROUP):
        sl = slice(g * HEAD_DIM, (g + 1) * HEAD_DIM)
        s = _scores(q_ref[:, sl], k) * scale
        sink = sink_ref[h, g]
        mx = jnp.maximum(jnp.max(s, axis=-1, keepdims=True), sink)
        p = jnp.exp(s - mx)
        den = jnp.sum(p, axis=-1, keepdims=True) + jnp.exp(sink - mx)
        o = jnp.dot(p.astype(BF16), v, preferred_element_type=F32)
        o_ref[:, sl] = (o / den).astype(o_ref.dtype)


def _ctx_attn(u, sink, batch, seq, n_ctx):
    hd = HEAD_DIM
    qw = Q_GROUP * hd
    ctx_blk = (batch * seq) // n_ctx
    return pl.pallas_call(
        _ctx_attn_kernel,
        grid=(batch, N_KV_HEADS),
        in_specs=[pl.BlockSpec(memory_space=pltpu.SMEM),
                  pl.BlockSpec((n_ctx, qw), lambda b, h: (ctx_blk + b, h)),
                  pl.BlockSpec((n_ctx, hd), lambda b, h: (ctx_blk + b, K_OFF // hd + h)),
                  pl.BlockSpec((n_ctx, hd), lambda b, h: (ctx_blk + b, V_OFF // hd + h))],
        out_specs=pl.BlockSpec((n_ctx, qw), lambda b, h: (b, h)),
        out_shape=jax.ShapeDtypeStruct((batch * n_ctx, ATT_WIDTH), BF16),
        compiler_params=_cparams(2),
        name="context_attention",
    )(sink.reshape(N_KV_HEADS, Q_GROUP), u, u, u)


def _short_conv_kernel(u_ref, w_ref, b_ref, o_ref):
    x = u_ref[...].astype(F32)
    n = x.shape[0]
    r = lax.broadcasted_iota(jnp.int32, x.shape, 0)
    prev = jnp.where(r == 0, 0.0, pltpu.roll(x, 1, 0))
    nxt = jnp.where(r == n - 1, 0.0, pltpu.roll(x, n - 1, 0))
    w = w_ref[0]
    o_ref[...] = (prev * w[0:1] + x * w[1:2] + nxt * w[2:3] + b_ref[0]).astype(o_ref.dtype)


def _short_conv(u, conv_w, conv_b, layer, n_seq, seg, row_blk_off, tw=256):
    width = conv_w.shape[-1]
    cb = conv_b.reshape(conv_b.shape[0], 1, width)
    return pl.pallas_call(
        _short_conv_kernel,
        grid=(n_seq, width // tw),
        in_specs=[pl.BlockSpec((seg, tw), lambda s, j: (row_blk_off + s, HY_OFF // tw + j)),
                  pl.BlockSpec((1, SHORT_CONV, tw), lambda s, j: (layer, 0, j)),
                  pl.BlockSpec((1, 1, tw), lambda s, j: (layer, 0, j))],
        out_specs=pl.BlockSpec((seg, tw), lambda s, j: (s, j)),
        out_shape=jax.ShapeDtypeStruct((n_seq * seg, width), BF16),
        compiler_params=_cparams(2),
        name="short_conv",
    )(u, conv_w, cb)


def _filter_positions(n, n_fft):
    t = jnp.linspace(0.0, 1.0, n, dtype=F32)[:, None]
    w = 2.0 * math.pi * jnp.arange(n, dtype=F32)[:, None] / n
    bands = jnp.linspace(1e-4, FILTER_BANDS - 1, FILTER_BANDS, dtype=F32)[None, :]
    z = jnp.concatenate([t, jnp.cos(bands * w), -jnp.sin(bands * w)], axis=-1)
    zt = jnp.concatenate([z, t], axis=-1)
    mid = jnp.zeros((n_fft - 2 * n + 1, zt.shape[1]), F32)
    full = jnp.concatenate([zt, mid, zt[1:][::-1]], axis=0)
    feat = jnp.pad(full[:, :-1], ((0, 0), (0, FILTER_HIDDEN - (zt.shape[1] - 1))))
    return feat, full[:, -1:]


def _filter_kernel(z_ref, t_ref, w1_ref, b1_ref, w2_ref, b2_ref, w3_ref, b3_ref, fr_ref,
                   wf_ref, wb_ref, dl_ref, o_ref, *, n, n_fft):
    hi = lax.Precision.HIGHEST
    fr = fr_ref[...]
    h = jnp.sin(fr * (jnp.dot(z_ref[...], w1_ref[...], preferred_element_type=F32, precision=hi)
                      + b1_ref[...]))
    h = jnp.sin(fr * (jnp.dot(h, w2_ref[...], preferred_element_type=F32, precision=hi)
                      + b2_ref[...]))
    h = jnp.sin(fr * (jnp.dot(h, w3_ref[...], preferred_element_type=F32, precision=hi)
                      + b3_ref[...]))
    tr = z_ref.shape[0]
    row = pl.program_id(1) * tr + lax.broadcasted_iota(jnp.int32, (tr, 1), 0)
    decay = jnp.exp(-t_ref[...] * dl_ref[...])
    for o in range(HY_ORDER):
        fwd = jnp.dot(h, wf_ref[o, 0], preferred_element_type=F32, precision=hi)
        bwd = jnp.dot(h, wb_ref[o, 0], preferred_element_type=F32, precision=hi)
        sel = jnp.where(row < n, fwd, jnp.where(row > n_fft - n, bwd, 0.0))
        o_ref[o] = sel * decay


def _hyena_filter(n, n_fft, w1, b1, w2, b2, w3, b3, freq, w_out, tr=512, tw=512):
    feat, tpos = _filter_positions(n, n_fft)
    hid = FILTER_HIDDEN
    w1p = jnp.pad(w1, ((0, hid - w1.shape[0]), (0, 0)))
    wo = w_out.reshape(hid, HY_ORDER, 2, HY_WIDTH).transpose(1, 2, 0, 3)
    deltas = jnp.abs(jnp.linspace(MIN_DECAY, MAX_DECAY, HY_WIDTH, dtype=F32)).reshape(1, HY_WIDTH)
    small = lambda shape: pl.BlockSpec(shape, lambda j, r: (0,) * len(shape))
    return pl.pallas_call(
        functools.partial(_filter_kernel, n=n, n_fft=n_fft),
        grid=(HY_WIDTH // tw, n_fft // tr),
        in_specs=[pl.BlockSpec((tr, hid), lambda j, r: (r, 0)),
                  pl.BlockSpec((tr, 1), lambda j, r: (r, 0)),
                  small((hid, hid)), small((1, hid)), small((hid, hid)), small((1, hid)),
                  small((hid, hid)), small((1, hid)), small((1, hid)),
                  pl.BlockSpec((HY_ORDER, 1, hid, tw), lambda j, r: (0, 0, 0, j)),
                  pl.BlockSpec((HY_ORDER, 1, hid, tw), lambda j, r: (0, 1, 0, j)),
                  pl.BlockSpec((1, tw), lambda j, r: (0, j))],
        out_specs=pl.BlockSpec((HY_ORDER, tr, tw), lambda j, r: (0, r, j)),
        out_shape=jax.ShapeDtypeStruct((HY_ORDER, n_fft, HY_WIDTH), F32),
        compiler_params=_cparams(2),
        name="hyena_filter",
    )(feat, tpos, w1p, b1.reshape(1, hid), w2, b2.reshape(1, hid), w3, b3.reshape(1, hid),
      freq.reshape(1, hid), wo, wo, deltas)


@functools.lru_cache(maxsize=None)
def _dft_constants(n_fft, no, ni):
    jo = np.arange(no)
    k1 = np.arange(no)
    half = no // 2
    f1r, f1d, f1i = [], [], []
    for i in range(ni):
        ang = 2.0 * np.pi * np.outer(k1, jo * ni + i) / n_fft
        c, s = np.cos(ang), np.sin(ang)
        f1r.append(np.concatenate([c, -s], axis=0))
        ch, sh = c[:, :half], s[:, :half]
        f1d.append(np.block([[ch, sh], [-sh, ch]]))
        angi = 2.0 * np.pi * np.outer(np.arange(half) * ni + i, k1) / n_fft
        ci, si = np.cos(angi), np.sin(angi)
        f1i.append(np.block([[ci, -si], [si, ci]]))
    ang3 = 2.0 * np.pi * np.outer(np.arange(ni), np.arange(ni)) / ni
    c3, s3 = np.cos(ang3), np.sin(ang3)
    m3 = np.block([[c3, s3], [-s3, c3]])
    m3c = np.block([[c3, -s3], [s3, c3]])
    as_bf16 = lambda a: np.asarray(a, np.float32).astype(BF16)
    return (as_bf16(np.stack(f1r)), as_bf16(np.stack(f1d)), as_bf16(np.stack(f1i)),
            as_bf16(m3), as_bf16(m3c))


def _const_spec(shape, n_axes):
    zeros = (0,) * len(shape)
    if n_axes == 2:
        index = lambda a, b: zeros
    else:
        index = lambda a: zeros
    return pl.BlockSpec(shape, index, pipeline_mode=pl.Buffered(1))


def _dft_stage3_rhs(a_ref, k1, no, ni):
    re = a_ref[pl.ds(k1, ni, stride=2 * no), :]
    im = a_ref[pl.ds(no + k1, ni, stride=2 * no), :]
    return jnp.concatenate([re, im], axis=0).astype(BF16)


def _row_block(i, size):
    return pl.ds(pl.multiple_of(i * size, size), size)


def _spectrum_kernel(h_ref, f1r_ref, m3_ref, o_ref, a_ref, *, no, ni):
    inv_n = 1.0 / (no * ni)

    def stage1(ji, carry):
        rhs = h_ref[0, pl.ds(ji, no, stride=ni), :].astype(BF16)
        a_ref[_row_block(ji, 2 * no), :] = jnp.dot(f1r_ref[ji], rhs, preferred_element_type=F32)
        return carry

    lax.fori_loop(0, ni, stage1, 0)

    def stage3(k1, carry):
        x = jnp.dot(m3_ref[...], _dft_stage3_rhs(a_ref, k1, no, ni), preferred_element_type=F32) * inv_n
        r0 = pl.multiple_of(k1 * ni, ni)
        o_ref[0, 0, pl.ds(r0, ni), :] = x[:ni].astype(o_ref.dtype)
        o_ref[0, 1, pl.ds(r0, ni), :] = x[ni:].astype(o_ref.dtype)
        return carry

    lax.fori_loop(0, no, stage3, 0)


def _filter_spectrum(hfull, no=FFT_NO, ni=FFT_NI, tc=HY_TC):
    n_ord, n_fft, width = hfull.shape
    f1r, _, _, m3, _ = _dft_constants(n_fft, no, ni)
    return pl.pallas_call(
        functools.partial(_spectrum_kernel, no=no, ni=ni),
        grid=(n_ord, width // tc),
        in_specs=[pl.BlockSpec((1, n_fft, tc), lambda o, c: (o, 0, c)),
                  _const_spec(f1r.shape, 2), _const_spec(m3.shape, 2)],
        out_specs=pl.BlockSpec((1, 2, n_fft, tc), lambda o, c: (o, 0, 0, c)),
        out_shape=jax.ShapeDtypeStruct((n_ord, 2, n_fft, width), BF16),
        scratch_shapes=[pltpu.VMEM((ni * 2 * no, tc), F32)],
        compiler_params=_cparams(2),
        name="hyena_spectrum",
    )(hfull, f1r, m3)


def _long_conv_kernel(v_ref, g_ref, spec_ref, bias_ref, f1d_ref, f1i_ref, m3_ref, m3c_ref,
                      o_ref, z_ref, a_ref, b_ref, *, n, no, ni):
    order = pl.program_id(1)
    tc = z_ref.shape[-1]
    half = no // 2
    n_pad = half * ni

    @pl.when(order == 0)
    def _():
        for b in range(2):
            z_ref[b, 0:n, :] = v_ref[b * n:(b + 1) * n, :].astype(F32)
            if n < n_pad:
                z_ref[b, n:n_pad, :] = jnp.zeros((n_pad - n, tc), F32)

    def stage1(ji, carry):
        zr = z_ref[0, pl.ds(ji, half, stride=ni), :]
        zi = z_ref[1, pl.ds(ji, half, stride=ni), :]
        rhs = jnp.concatenate([zr, zi], axis=0).astype(BF16)
        a_ref[_row_block(ji, 2 * no), :] = jnp.dot(f1d_ref[ji], rhs, preferred_element_type=F32)
        return carry

    lax.fori_loop(0, ni, stage1, 0)

    def stage3(k1, carry):
        x = jnp.dot(m3_ref[...], _dft_stage3_rhs(a_ref, k1, no, ni), preferred_element_type=F32)
        r0 = pl.multiple_of(k1 * ni, ni)
        hr = spec_ref[0, 0, pl.ds(r0, ni), :].astype(F32)
        hi = spec_ref[0, 1, pl.ds(r0, ni), :].astype(F32)
        xr, xi = x[:ni], x[ni:]
        y = jnp.concatenate([xr * hr - xi * hi, xr * hi + xi * hr], axis=0).astype(BF16)
        b_ref[_row_block(k1, 2 * ni), :] = jnp.dot(m3c_ref[...], y, preferred_element_type=F32)
        return carry

    lax.fori_loop(0, no, stage3, 0)

    def stage1_inv(t2, carry):
        br = b_ref[pl.ds(t2, no, stride=2 * ni), :]
        bi = b_ref[pl.ds(ni + t2, no, stride=2 * ni), :]
        rhs = jnp.concatenate([br, bi], axis=0).astype(BF16)
        a_ref[pl.ds(pl.multiple_of(t2 * 2 * no, 2 * no), no), :] = jnp.dot(
            f1i_ref[t2], rhs, preferred_element_type=F32)
        return carry

    lax.fori_loop(0, ni, stage1_inv, 0)

    bias = bias_ref[0, 0]

    def finish(t1, carry):
        r0 = pl.multiple_of(t1 * ni, ni)
        conv = (a_ref[pl.ds(t1, ni, stride=2 * no), :], a_ref[pl.ds(half + t1, ni, stride=2 * no), :])
        for b in range(2):
            z = z_ref[b, pl.ds(r0, ni), :]
            gate = g_ref[pl.ds(b * n + r0, ni), :].astype(F32)
            zn = gate * (conv[b] + z * bias)
            z_ref[b, pl.ds(r0, ni), :] = zn
            o_ref[pl.ds(b * n + r0, ni), :] = zn.astype(o_ref.dtype)
        return carry

    lax.fori_loop(0, n // ni, finish, 0)


def _long_conv(uc, spec, hy_bias, layer, n, no=FFT_NO, ni=FFT_NI, tc=HY_TC):
    n_fft = no * ni
    assert n % ni == 0 and n <= n_fft // 2
    _, f1d, f1i, m3, m3c = _dft_constants(n_fft, no, ni)
    nct = HY_WIDTH // tc
    bias = hy_bias.reshape(hy_bias.shape[0], HY_ORDER, 1, HY_WIDTH)
    return pl.pallas_call(
        functools.partial(_long_conv_kernel, n=n, no=no, ni=ni),
        grid=(nct, HY_ORDER),
        in_specs=[pl.BlockSpec((2 * n, tc), lambda c, o: (0, c)),
                  pl.BlockSpec((2 * n, tc), lambda c, o: (0, (1 + o) * nct + c)),
                  pl.BlockSpec((1, 2, n_fft, tc), lambda c, o: (o, 0, 0, c)),
                  pl.BlockSpec((1, 1, 1, tc), lambda c, o: (layer, o, 0, c)),
                  _const_spec(f1d.shape, 2), _const_spec(f1i.shape, 2),
                  _const_spec(m3.shape, 2), _const_spec(m3c.shape, 2)],
        out_specs=pl.BlockSpec((2 * n, tc), lambda c, o: (0, c)),
        out_shape=jax.ShapeDtypeStruct((2 * n, HY_WIDTH), BF16),
        scratch_shapes=[pltpu.VMEM((2, n_fft // 2, tc), F32),
                        pltpu.VMEM((ni * 2 * no, tc), F32),
                        pltpu.VMEM((no * 2 * ni, tc), F32)],
        compiler_params=_cparams(2),
        name="hyena_long_conv",
    )(uc, uc, spec, bias, f1d, f1i, m3, m3c)


def _hyena(u, conv_w, conv_b, filt, hy_bias, layer, n_seq, seg, row_blk_off, n_fft):
    uc = _short_conv(u, conv_w, conv_b, layer, n_seq, seg, row_blk_off)
    spec = _filter_spectrum(_hyena_filter(seg, n_fft, *filt))
    return _long_conv(uc, spec, hy_bias, layer, seg)


def _moe(h2, logits, w1, w3, w2):
    t, d = h2.shape
    dff = w1.shape[-1]
    top_logit, top_idx = lax.top_k(logits, TOP_K)
    gate = jax.nn.softmax(top_logit, axis=-1)
    n_assign = t * TOP_K
    flat_e = top_idx.reshape(-1)
    flat_tok = jnp.repeat(jnp.arange(t, dtype=jnp.int32), TOP_K)
    order = jnp.argsort(flat_e)
    e_sorted = flat_e[order]
    counts = jnp.bincount(flat_e, length=N_EXPERTS)
    padded = (counts + MOE_BLOCK - 1) // MOE_BLOCK * MOE_BLOCK
    start = jnp.cumsum(counts) - counts
    pad_end = jnp.cumsum(padded)
    pad_start = pad_end - padded
    dest = (pad_start[e_sorted] + jnp.arange(n_assign) - start[e_sorted]).astype(jnp.int32)
    n_blocks = -(-n_assign // MOE_BLOCK) + N_EXPERTS
    n_slots = n_blocks * MOE_BLOCK
    slot_tok = jnp.full((n_slots,), t, jnp.int32).at[dest].set(flat_tok[order])
    block_expert = jnp.minimum(
        jnp.searchsorted(pad_end, jnp.arange(n_blocks) * MOE_BLOCK, side='right'),
        N_EXPERTS - 1).astype(jnp.int32)
    n_used = (pad_end[-1] // MOE_BLOCK).astype(jnp.int32).reshape(1)
    h_pad = jnp.concatenate([h2, jnp.zeros((1, d), h2.dtype)], axis=0)
    xs = h_pad[slot_tok]
    act = _gmm(xs, (w1, w3), block_expert, n_used, k=d, n=dff, tm=MOE_BLOCK, tn=512,
               out_dtype=BF16, name="moe_up")
    ys = _gmm(act, (w2,), block_expert, n_used, k=dff, n=d, tm=MOE_BLOCK, tn=512,
              out_dtype=F32, name="moe_down")
    pos = jnp.zeros((n_assign,), jnp.int32).at[order].set(dest).reshape(t, TOP_K)
    return gate[:, 0:1] * ys[pos[:, 0]] + gate[:, 1:2] * ys[pos[:, 1]]


def kernel(x, c, ctx, c_ctx, w_ada, b_ada, norm_g, w_in, attn_sink, conv_w, conv_b,
           filt_w1, filt_b1, filt_w2, filt_b2, filt_w3, filt_b3, filt_freq, filt_w_out, hyena_bias,
           w_attn_out, w_hyena_out, w_out, ffn_w1, ffn_w3, ffn_w2,
           moe_router, moe_w1, moe_w3, moe_w2):
    batch, seq, d = x.shape
    n_ctx = ctx.shape[1]
    depth = w_in.shape[0]
    in_width = w_in.shape[-1]
    n_lat = batch * seq
    n_all = n_lat + batch * n_ctx
    n_fft = 2 * seq
    ga_off = HY_OFF + (HY_ORDER + 1) * HY_WIDTH
    gh_off = ga_off + d
    assert batch == 2 and FFT_NO * FFT_NI == n_fft

    x_all = jnp.concatenate([x.reshape(n_lat, d), ctx.reshape(batch * n_ctx, d)], axis=0)
    cond = jnp.concatenate([c, c_ctx[None], jnp.zeros((8 - batch - 1, d), F32)], axis=0)
    mod = _ada(cond, w_ada, b_ada)
    mod = mod.reshape(depth, 8, 6, 1, d)
    rope_tabs = _rope_tables(seq)
    tm_all = n_all // 8
    tm_lat = n_lat // 8

    def mods(l, j):
        return mod[l, :, j]

    h = _norm_mod(x_all, norm_g[0, 0], mods(0, 0), mods(0, 1), seq)
    for l in range(depth):
        last = l == depth - 1
        rows = n_lat if last else n_all
        tm = tm_lat if last else tm_all
        filt = (filt_w1[l], filt_b1[l], filt_w2[l], filt_b2[l], filt_w3[l], filt_b3[l],
                filt_freq[l], filt_w_out[l])

        u = _gmm(h, (w_in,), *_dense_ids(n_all, tm_all, l), k=d, n=in_width, tm=tm_all, tn=512,
                 out_dtype=BF16, name="in_proj")
        qk = _rope(u, rope_tabs, n_lat, seq, V_OFF)
        att = _win_attn(qk, u, attn_sink[l], batch, seq, n_ctx)
        hy = _hyena(u, conv_w, conv_b, filt, hyena_bias, l, batch, seq, 0, n_fft)
        if not last:
            att = jnp.concatenate([att, _ctx_attn(u, attn_sink[l], batch, seq, n_ctx)], axis=0)
            hy_c = _hyena(u, conv_w, conv_b, filt, hyena_bias, l, batch, n_ctx, n_lat // n_ctx, n_fft)
            hy = jnp.concatenate([hy, hy_c], axis=0)
        mrg = _merge(att, hy, u, w_attn_out, w_hyena_out, l, rows, ga_off, gh_off, tm)
        y = _gmm(mrg, (w_out,), *_dense_ids(rows, tm, l), k=d, n=d, tm=tm, tn=512,
                 out_dtype=F32, name="out_proj")
        router = None
        if l % 2 == 1:
            router = jnp.pad(moe_router[l // 2], ((0, 0), (0, LANES - N_EXPERTS)))
        res = _post(x_all, y, mods(l, 2), norm_g[l, 1], seq, rows,
                    nxt=(norm_g[l, 2], mods(l, 3), mods(l, 4)), router=router)
        x_all, h2 = res[0], res[1]

        if l % 2 == 0:
            i = l // 2
            dff = ffn_w1.shape[-1]
            act = _gmm(h2, (ffn_w1, ffn_w3), *_dense_ids(rows, tm, i), k=d, n=dff, tm=tm, tn=256,
                       out_dtype=BF16, name="ffn_up")
            f = _gmm(act, (ffn_w2,), *_dense_ids(rows, tm // 4, i), k=dff, n=d, tm=tm // 4, tn=256,
                     out_dtype=F32, name="ffn_down")
        else:
            i = l // 2
            f = _moe(h2, res[2][:, :N_EXPERTS], moe_w1[i], moe_w3[i], moe_w2[i])
        if last:
            (x_all,) = _post(x_all, f, mods(l, 5), norm_g[l, 3], seq, rows)
        else:
            x_all, h = _post(x_all, f, mods(l, 5), norm_g[l, 3], seq, rows,
                             nxt=(norm_g[l + 1, 0], mods(l + 1, 0), mods(l + 1, 1)))
    return x_all[:n_lat].reshape(batch, seq, d)
```

```python
import functools
import math

import numpy as np
import jax
import jax.numpy as jnp
from jax import lax
from jax.experimental import pallas as pl
from jax.experimental.pallas import tpu as pltpu

F32 = jnp.float32
BF16 = jnp.bfloat16

GRID_W = 64
N_Q_HEADS = 16
N_KV_HEADS = 4
HEAD_DIM = 128
Q_GROUP = N_Q_HEADS // N_KV_HEADS
ATT_WIDTH = N_Q_HEADS * HEAD_DIM
KV_WIDTH = N_KV_HEADS * HEAD_DIM
BLOCK = 128
ROPE_BASE = 10000.0
ROPE_FREQS = HEAD_DIM // 4
MASK_VALUE = -1e30
HY_WIDTH = 2048
HY_ORDER = 2
SHORT_CONV = 3
FILTER_BANDS = 16
FILTER_HIDDEN = 64
DECAY_TARGET = 1e-2
MIN_DECAY = math.log(DECAY_TARGET) / 1.5
MAX_DECAY = math.log(DECAY_TARGET) / 0.3
K_OFF = ATT_WIDTH
V_OFF = K_OFF + KV_WIDTH
HY_OFF = V_OFF + KV_WIDTH
N_EXPERTS = 8
TOP_K = 2
MOE_BLOCK = 256
RMS_EPS = 1e-6

LANES = 128
VMEM_LIMIT_BYTES = 56 * 1024 * 1024

FFT_NO = 128
FFT_NI = 64
HY_TC = 128


def _cparams(n_axes):
    return pltpu.CompilerParams(dimension_semantics=("arbitrary",) * n_axes,
                                vmem_limit_bytes=VMEM_LIMIT_BYTES)


def _ada_kernel(c_ref, w_ref, b_ref, o_ref):
    a = c_ref[...]
    a = a * jax.nn.sigmoid(a)
    o_ref[0] = jnp.dot(a, w_ref[0], preferred_element_type=F32,
                       precision=lax.Precision.HIGHEST) + b_ref[0]


def _ada(cond, w_ada, b_ada, tn=512):
    n_layers, d, n = w_ada.shape
    rows = cond.shape[0]
    return pl.pallas_call(
        _ada_kernel,
        grid=(n_layers, n // tn),
        in_specs=[pl.BlockSpec((rows, d), lambda l, j: (0, 0)),
                  pl.BlockSpec((1, d, tn), lambda l, j: (l, 0, j)),
                  pl.BlockSpec((1, 1, tn), lambda l, j: (l, 0, j))],
        out_specs=pl.BlockSpec((1, rows, tn), lambda l, j: (l, 0, j)),
        out_shape=jax.ShapeDtypeStruct((n_layers, rows, n), F32),
        compiler_params=_cparams(2),
        name="ada_mod",
    )(cond, w_ada, b_ada.reshape(n_layers, 1, n))


def _rms(x):
    return x * lax.rsqrt(jnp.mean(x * x, axis=-1, keepdims=True) + RMS_EPS)


def _norm_mod_kernel(x_ref, g_ref, sh_ref, sc_ref, o_ref):
    y = _rms(x_ref[...]) * g_ref[...]
    o_ref[...] = (y * (1.0 + sc_ref[0]) + sh_ref[0]).astype(o_ref.dtype)


def _group_of_tile(tr, seq):
    return lambda i: (jnp.minimum((i * tr) // seq, 2), 0, 0)


def _norm_mod(x_all, g, sh, sc, seq, tr=256):
    t, d = x_all.shape
    grp = _group_of_tile(tr, seq)
    return pl.pallas_call(
        _norm_mod_kernel,
        grid=(t // tr,),
        in_specs=[pl.BlockSpec((tr, d), lambda i: (i, 0)),
                  pl.BlockSpec((1, d), lambda i: (0, 0)),
                  pl.BlockSpec((1, 1, d), grp),
                  pl.BlockSpec((1, 1, d), grp)],
        out_specs=pl.BlockSpec((tr, d), lambda i: (i, 0)),
        out_shape=jax.ShapeDtypeStruct((t, d), BF16),
        compiler_params=_cparams(1),
        name="norm_mod",
    )(x_all, g.reshape(1, d), sh, sc)


def _post_kernel(x_ref, y_ref, gt_ref, gpost_ref, *rest, with_next, with_router):
    xn = x_ref[...] + gt_ref[0] * (_rms(y_ref[...].astype(F32)) * gpost_ref[...])
    if not with_next:
        (xo_ref,) = rest
        xo_ref[...] = xn
        return
    if with_router:
        gpre_ref, sh_ref, sc_ref, wr_ref, xo_ref, ho_ref, lg_ref = rest
    else:
        gpre_ref, sh_ref, sc_ref, xo_ref, ho_ref = rest
    xo_ref[...] = xn
    h = (_rms(xn) * gpre_ref[...]) * (1.0 + sc_ref[0]) + sh_ref[0]
    ho_ref[...] = h.astype(ho_ref.dtype)
    if with_router:
        lg_ref[...] = jnp.dot(h, wr_ref[...], preferred_element_type=F32,
                              precision=lax.Precision.HIGHEST)


def _post(x_all, y, gt, g_post, seq, rows, nxt=None, router=None, tr=256):
    t, d = x_all.shape
    grp = _group_of_tile(tr, seq)
    row = pl.BlockSpec((tr, d), lambda i: (i, 0))
    vec = pl.BlockSpec((1, d), lambda i: (0, 0))
    mod = pl.BlockSpec((1, 1, d), grp)
    in_specs = [row, row, mod, vec]
    args = [x_all, y, gt, g_post.reshape(1, d)]
    out_specs = [row]
    out_shape = [jax.ShapeDtypeStruct((rows, d), F32)]
    if nxt is not None:
        g_pre, sh, sc = nxt
        in_specs += [vec, mod, mod]
        args += [g_pre.reshape(1, d), sh, sc]
        out_specs.append(row)
        out_shape.append(jax.ShapeDtypeStruct((rows, d), BF16))
        if router is not None:
            in_specs.append(pl.BlockSpec((d, LANES), lambda i: (0, 0)))
            args.append(router)
            out_specs.append(pl.BlockSpec((tr, LANES), lambda i: (i, 0)))
            out_shape.append(jax.ShapeDtypeStruct((rows, LANES), F32))
    return pl.pallas_call(
        functools.partial(_post_kernel, with_next=nxt is not None, with_router=router is not None),
        grid=(rows // tr,),
        in_specs=in_specs,
        out_specs=out_specs,
        out_shape=out_shape,
        compiler_params=_cparams(1),
        name="post_norm",
    )(*args)


def _weight_changed(be_ref, m):
    return jnp.logical_or(m == 0, be_ref[m] != be_ref[jnp.maximum(m - 1, 0)])


def _gmm_kernel(be_ref, nv_ref, a_ref, w_ref, o_ref, wb_ref):
    m = pl.program_id(1)

    @pl.when(m < nv_ref[0])
    def _():
        @pl.when(_weight_changed(be_ref, m))
        def _():
            wb_ref[...] = w_ref[0].astype(BF16)

        o_ref[...] = jnp.dot(a_ref[...], wb_ref[...],
                             preferred_element_type=F32).astype(o_ref.dtype)


def _swiglu_kernel(be_ref, nv_ref, a_ref, w1_ref, w3_ref, o_ref, w1b_ref, w3b_ref):
    m = pl.program_id(1)

    @pl.when(m < nv_ref[0])
    def _():
        @pl.when(_weight_changed(be_ref, m))
        def _():
            w1b_ref[...] = w1_ref[0].astype(BF16)
            w3b_ref[...] = w3_ref[0].astype(BF16)

        a = a_ref[...]
        g = jnp.dot(a, w1b_ref[...], preferred_element_type=F32)
        u = jnp.dot(a, w3b_ref[...], preferred_element_type=F32)
        o_ref[...] = (g * jax.nn.sigmoid(g) * u).astype(o_ref.dtype)


def _gmm(a, ws, be, nvalid, *, k, n, tm, tn, out_dtype, w_col_off=0, rows=None, w_single_buffer=False,
         name="gmm"):
    rows = a.shape[0] if rows is None else rows
    assert rows % tm == 0 and n % tn == 0 and w_col_off % tn == 0
    off = w_col_off // tn
    w_mode = dict(pipeline_mode=pl.Buffered(1)) if w_single_buffer else {}

    def a_map(j, m, be_ref, nv_ref):
        return (jnp.minimum(m, nv_ref[0] - 1), 0)

    def w_map(j, m, be_ref, nv_ref):
        return (be_ref[jnp.minimum(m, nv_ref[0] - 1)], 0, j + off)

    def o_map(j, m, be_ref, nv_ref):
        return (m, j)

    kernel = _gmm_kernel if len(ws) == 1 else _swiglu_kernel
    return pl.pallas_call(
        kernel,
        grid_spec=pltpu.PrefetchScalarGridSpec(
            num_scalar_prefetch=2,
            grid=(n // tn, rows // tm),
            in_specs=[pl.BlockSpec((tm, k), a_map)] + [pl.BlockSpec((1, k, tn), w_map, **w_mode)] * len(ws),
            out_specs=pl.BlockSpec((tm, tn), o_map),
            scratch_shapes=[pltpu.VMEM((k, tn), BF16)] * len(ws)),
        out_shape=jax.ShapeDtypeStruct((rows, n), out_dtype),
        compiler_params=_cparams(2),
        name=name,
    )(be, nvalid, a, *ws)


def _dense_ids(rows, tm, idx):
    nb = rows // tm
    return jnp.full((nb,), idx, jnp.int32), jnp.full((1,), nb, jnp.int32)


def _merge_kernel(att_ref, hy_ref, ga_ref, gh_ref, wa_ref, wh_ref, o_ref, wab_ref, whb_ref):
    @pl.when(pl.program_id(1) == 0)
    def _():
        wab_ref[...] = wa_ref[0].astype(BF16)
        whb_ref[...] = wh_ref[0].astype(BF16)

    pa = jnp.dot(att_ref[...], wab_ref[...], preferred_element_type=F32)
    ph = jnp.dot(hy_ref[...], whb_ref[...], preferred_element_type=F32)
    ga = jax.nn.sigmoid(ga_ref[...].astype(F32))
    gh = jax.nn.sigmoid(gh_ref[...].astype(F32))
    o_ref[...] = (ga * pa + gh * ph).astype(o_ref.dtype)


def _merge(att, hy, u, w_ao, w_ho, layer, rows, ga_off, gh_off, tm, tn=512):
    ka, d = w_ao.shape[1:]
    kh = w_ho.shape[1]
    return pl.pallas_call(
        _merge_kernel,
        grid=(d // tn, rows // tm),
        in_specs=[pl.BlockSpec((tm, ka), lambda j, m: (m, 0)),
                  pl.BlockSpec((tm, kh), lambda j, m: (m, 0)),
                  pl.BlockSpec((tm, tn), lambda j, m: (m, ga_off // tn + j)),
                  pl.BlockSpec((tm, tn), lambda j, m: (m, gh_off // tn + j)),
                  pl.BlockSpec((1, ka, tn), lambda j, m: (layer, 0, j)),
                  pl.BlockSpec((1, kh, tn), lambda j, m: (layer, 0, j))],
        out_specs=pl.BlockSpec((tm, tn), lambda j, m: (m, j)),
        out_shape=jax.ShapeDtypeStruct((rows, d), BF16),
        scratch_shapes=[pltpu.VMEM((ka, tn), BF16), pltpu.VMEM((kh, tn), BF16)],
        compiler_params=_cparams(2),
        name="gated_merge",
    )(att, hy, u, u, w_ao, w_ho)


def _rope_tables(seq):
    rows = seq // GRID_W
    row = jnp.repeat(jnp.arange(rows), GRID_W).astype(F32)
    col = jnp.tile(jnp.arange(GRID_W), rows).astype(F32)
    inv = ROPE_BASE ** (-jnp.arange(ROPE_FREQS, dtype=F32) / ROPE_FREQS)
    ang = jnp.stack([row[:, None] * inv, col[:, None] * inv], axis=1)
    cos, sin = jnp.cos(ang), jnp.sin(ang)
    zero = jnp.zeros_like(sin)
    cos_t = jnp.stack([cos, cos], axis=2).reshape(seq, HEAD_DIM)
    s_lo = jnp.stack([-sin, zero], axis=2).reshape(seq, HEAD_DIM)
    s_hi = jnp.stack([zero, sin], axis=2).reshape(seq, HEAD_DIM)
    return cos_t, s_lo, s_hi


def _rope_kernel(x_ref, c_ref, lo_ref, hi_ref, o_ref):
    heads = x_ref.shape[1] // HEAD_DIM
    c, lo, hi = c_ref[...], lo_ref[...], hi_ref[...]
    for h in range(heads):
        sl = slice(h * HEAD_DIM, (h + 1) * HEAD_DIM)
        x = x_ref[:, sl].astype(F32)
        up = pltpu.roll(x, HEAD_DIM - ROPE_FREQS, 1)
        dn = pltpu.roll(x, ROPE_FREQS, 1)
        o_ref[:, sl] = (x * c + up * lo + dn * hi).astype(o_ref.dtype)


def _rope(u, tables, n_latent, seq, width, tr=256, tw=512):
    nseq = seq // tr
    tab = pl.BlockSpec((tr, HEAD_DIM), lambda i, j: (i % nseq, 0))
    return pl.pallas_call(
        _rope_kernel,
        grid=(n_latent // tr, width // tw),
        in_specs=[pl.BlockSpec((tr, tw), lambda i, j: (i, j)), tab, tab, tab],
        out_specs=pl.BlockSpec((tr, tw), lambda i, j: (i, j)),
        out_shape=jax.ShapeDtypeStruct((n_latent, width), BF16),
        compiler_params=_cparams(2),
        name="rope",
    )(u, *tables)


def _scores(q, k):
    return lax.dot_general(q, k, (((1,), (1,)), ((), ())), preferred_element_type=F32)


def _win_attn_kernel(sink_ref, q_ref, kp_ref, kc_ref, kn_ref, vp_ref, vc_ref, vn_ref,
                     kx_ref, vx_ref, o_ref):
    n = pl.program_id(1)
    h = pl.program_id(2)
    nb = pl.num_programs(1)
    scale = HEAD_DIM ** -0.5
    qi = lax.broadcasted_iota(jnp.int32, (BLOCK, BLOCK), 0)
    kj = lax.broadcasted_iota(jnp.int32, (BLOCK, BLOCK), 1)
    ok_prev = jnp.logical_and(kj >= qi, n > 0)
    ok_next = jnp.logical_and(kj <= qi, n < nb - 1)
    kp, kc, kn, kx = kp_ref[...], kc_ref[...], kn_ref[...], kx_ref[...]
    vp, vc, vn, vx = vp_ref[...], vc_ref[...], vn_ref[...], vx_ref[...]
    for g in range(Q_GROUP):
        sl = slice(g * HEAD_DIM, (g + 1) * HEAD_DIM)
        q = q_ref[:, sl]
        sp = jnp.where(ok_prev, _scores(q, kp) * scale, MASK_VALUE)
        sc = _scores(q, kc) * scale
        sn = jnp.where(ok_next, _scores(q, kn) * scale, MASK_VALUE)
        sx = _scores(q, kx) * scale
        sink = sink_ref[h, g]
        mx = jnp.maximum(jnp.maximum(jnp.max(sp, axis=-1, keepdims=True),
                                     jnp.max(sc, axis=-1, keepdims=True)),
                         jnp.maximum(jnp.max(sn, axis=-1, keepdims=True),
                                     jnp.max(sx, axis=-1, keepdims=True)))
        mx = jnp.maximum(mx, sink)
        pp, pc, pn, px = jnp.exp(sp - mx), jnp.exp(sc - mx), jnp.exp(sn - mx), jnp.exp(sx - mx)
        den = (jnp.sum(pp, axis=-1, keepdims=True) + jnp.sum(pc, axis=-1, keepdims=True)
               + jnp.sum(pn, axis=-1, keepdims=True) + jnp.sum(px, axis=-1, keepdims=True)
               + jnp.exp(sink - mx))
        o = (jnp.dot(pp.astype(BF16), vp, preferred_element_type=F32)
             + jnp.dot(pc.astype(BF16), vc, preferred_element_type=F32)
             + jnp.dot(pn.astype(BF16), vn, preferred_element_type=F32)
             + jnp.dot(px.astype(BF16), vx, preferred_element_type=F32))
        o_ref[:, sl] = (o / den).astype(o_ref.dtype)


def _win_attn(qk, u, sink, batch, seq, n_ctx):
    nb = seq // BLOCK
    hd = HEAD_DIM
    qw = Q_GROUP * hd
    kcol = K_OFF // hd
    vcol = V_OFF // hd
    ctx_blk = (batch * seq) // n_ctx

    def blk(shift, col):
        def index(b, n, h):
            return (b * nb + jnp.clip(n + shift, 0, nb - 1), col + h)
        return pl.BlockSpec((BLOCK, hd), index)

    return pl.pallas_call(
        _win_attn_kernel,
        grid=(batch, nb, N_KV_HEADS),
        in_specs=[pl.BlockSpec(memory_space=pltpu.SMEM),
                  pl.BlockSpec((BLOCK, qw), lambda b, n, h: (b * nb + n, h)),
                  blk(-1, kcol), blk(0, kcol), blk(1, kcol),
                  blk(-1, vcol), blk(0, vcol), blk(1, vcol),
                  pl.BlockSpec((n_ctx, hd), lambda b, n, h: (ctx_blk + b, kcol + h)),
                  pl.BlockSpec((n_ctx, hd), lambda b, n, h: (ctx_blk + b, vcol + h))],
        out_specs=pl.BlockSpec((BLOCK, qw), lambda b, n, h: (b * nb + n, h)),
        out_shape=jax.ShapeDtypeStruct((batch * seq, ATT_WIDTH), BF16),
        compiler_params=_cparams(3),
        name="window_attention",
    )(sink.reshape(N_KV_HEADS, Q_GROUP), qk, qk, qk, qk, u, u, u, u, u)


def _ctx_attn_kernel(sink_ref, q_ref, k_ref, v_ref, o_ref):
    h = pl.program_id(1)
    scale = HEAD_DIM ** -0.5
    k, v = k_ref[...], v_ref[...]
    for g in range(Q_GROUP):
        sl = slice(g * HEAD_DIM, (g + 1) * HEAD_DIM)
        s = _scores(q_ref[:, sl], k) * scale
        sink = sink_ref[h, g]
        mx = jnp.maximum(jnp.max(s, axis=-1, keepdims=True), sink)
        p = jnp.exp(s - mx)
        den = jnp.sum(p, axis=-1, keepdims=True) + jnp.exp(sink - mx)
        o = jnp.dot(p.astype(BF16), v, preferred_element_type=F32)
        o_ref[:, sl] = (o / den).astype(o_ref.dtype)


def _ctx_attn(u, sink, batch, seq, n_ctx):
    hd = HEAD_DIM
    qw = Q_GROUP * hd
    ctx_blk = (batch * seq) // n_ctx
    return pl.pallas_call(
        _ctx_attn_kernel,
        grid=(batch, N_KV_HEADS),
        in_specs=[pl.BlockSpec(memory_space=pltpu.SMEM),
                  pl.BlockSpec((n_ctx, qw), lambda b, h: (ctx_blk + b, h)),
                  pl.BlockSpec((n_ctx, hd), lambda b, h: (ctx_blk + b, K_OFF // hd + h)),
                  pl.BlockSpec((n_ctx, hd), lambda b, h: (ctx_blk + b, V_OFF // hd + h))],
        out_specs=pl.BlockSpec((n_ctx, qw), lambda b, h: (b, h)),
        out_shape=jax.ShapeDtypeStruct((batch * n_ctx, ATT_WIDTH), BF16),
        compiler_params=_cparams(2),
        name="context_attention",
    )(sink.reshape(N_KV_HEADS, Q_GROUP), u, u, u)


def _short_conv_kernel(u_ref, w_ref, b_ref, o_ref):
    x = u_ref[...].astype(F32)
    n = x.shape[0]
    r = lax.broadcasted_iota(jnp.int32, x.shape, 0)
    prev = jnp.where(r == 0, 0.0, pltpu.roll(x, 1, 0))
    nxt = jnp.where(r == n - 1, 0.0, pltpu.roll(x, n - 1, 0))
    w = w_ref[0]
    o_ref[...] = (prev * w[0:1] + x * w[1:2] + nxt * w[2:3] + b_ref[0]).astype(o_ref.dtype)


def _short_conv(u, conv_w, conv_b, layer, n_seq, seg, row_blk_off, tw=256):
    width = conv_w.shape[-1]
    cb = conv_b.reshape(conv_b.shape[0], 1, width)
    return pl.pallas_call(
        _short_conv_kernel,
        grid=(n_seq, width // tw),
        in_specs=[pl.BlockSpec((seg, tw), lambda s, j: (row_blk_off + s, HY_OFF // tw + j)),
                  pl.BlockSpec((1, SHORT_CONV, tw), lambda s, j: (layer, 0, j)),
                  pl.BlockSpec((1, 1, tw), lambda s, j: (layer, 0, j))],
        out_specs=pl.BlockSpec((seg, tw), lambda s, j: (s, j)),
        out_shape=jax.ShapeDtypeStruct((n_seq * seg, width), BF16),
        compiler_params=_cparams(2),
        name="short_conv",
    )(u, conv_w, cb)


def _filter_positions(n, n_fft):
    t = jnp.linspace(0.0, 1.0, n, dtype=F32)[:, None]
    w = 2.0 * math.pi * jnp.arange(n, dtype=F32)[:, None] / n
    bands = jnp.linspace(1e-4, FILTER_BANDS - 1, FILTER_BANDS, dtype=F32)[None, :]
    z = jnp.concatenate([t, jnp.cos(bands * w), -jnp.sin(bands * w)], axis=-1)
    zt = jnp.concatenate([z, t], axis=-1)
    mid = jnp.zeros((n_fft - 2 * n + 1, zt.shape[1]), F32)
    full = jnp.concatenate([zt, mid, zt[1:][::-1]], axis=0)
    feat = jnp.pad(full[:, :-1], ((0, 0), (0, FILTER_HIDDEN - (zt.shape[1] - 1))))
    return feat, full[:, -1:]


def _filter_kernel(z_ref, t_ref, w1_ref, b1_ref, w2_ref, b2_ref, w3_ref, b3_ref, fr_ref,
                   wo_ref, dl_ref, o_ref, hid_ref, *, n, n_fft):
    hi = lax.Precision.HIGHEST
    tr = z_ref.shape[0]

    @pl.when(pl.program_id(1) == 0)
    def _():
        fr = fr_ref[...]
        h = jnp.sin(fr * (jnp.dot(z_ref[...], w1_ref[...], preferred_element_type=F32, precision=hi)
                          + b1_ref[...]))
        h = jnp.sin(fr * (jnp.dot(h, w2_ref[...], preferred_element_type=F32, precision=hi)
                          + b2_ref[...]))
        hid_ref[...] = jnp.sin(fr * (jnp.dot(h, w3_ref[...], preferred_element_type=F32, precision=hi)
                                     + b3_ref[...]))

    h = hid_ref[...]
    row = pl.program_id(0) * tr + lax.broadcasted_iota(jnp.int32, (tr, 1), 0)
    live = jnp.logical_or(row < n, row > n_fft - n)
    decay = jnp.where(live, jnp.exp(-t_ref[...] * dl_ref[...]), 0.0)
    for o in range(HY_ORDER):
        o_ref[o] = jnp.dot(h, wo_ref[o, 0], preferred_element_type=F32, precision=hi) * decay


def _hyena_filter(n, n_fft, w1, b1, w2, b2, w3, b3, freq, w_out, tw=512):
    tr = min(512, n)
    assert n % tr == 0 and n_fft % tr == 0
    feat, tpos = _filter_positions(n, n_fft)
    hid = FILTER_HIDDEN
    w1p = jnp.pad(w1, ((0, hid - w1.shape[0]), (0, 0)))
    wo = w_out.reshape(hid, HY_ORDER, 2, HY_WIDTH).transpose(1, 2, 0, 3)
    deltas = jnp.abs(jnp.linspace(MIN_DECAY, MAX_DECAY, HY_WIDTH, dtype=F32)).reshape(1, HY_WIDTH)
    small = lambda shape: pl.BlockSpec(shape, lambda r, j: (0,) * len(shape))
    return pl.pallas_call(
        functools.partial(_filter_kernel, n=n, n_fft=n_fft),
        grid=(n_fft // tr, HY_WIDTH // tw),
        in_specs=[pl.BlockSpec((tr, hid), lambda r, j: (r, 0)),
                  pl.BlockSpec((tr, 1), lambda r, j: (r, 0)),
                  small((hid, hid)), small((1, hid)), small((hid, hid)), small((1, hid)),
                  small((hid, hid)), small((1, hid)), small((1, hid)),
                  pl.BlockSpec((HY_ORDER, 1, hid, tw), lambda r, j: (0, jnp.where(r * tr >= n, 1, 0), 0, j)),
                  pl.BlockSpec((1, tw), lambda r, j: (0, j))],
        out_specs=pl.BlockSpec((HY_ORDER, tr, tw), lambda r, j: (0, r, j)),
        out_shape=jax.ShapeDtypeStruct((HY_ORDER, n_fft, HY_WIDTH), F32),
        scratch_shapes=[pltpu.VMEM((tr, hid), F32)],
        compiler_params=_cparams(2),
        name="hyena_filter",
    )(feat, tpos, w1p, b1.reshape(1, hid), w2, b2.reshape(1, hid), w3, b3.reshape(1, hid),
      freq.reshape(1, hid), wo, deltas)


@functools.lru_cache(maxsize=None)
def _dft_constants(n_fft, no, ni):
    jo = np.arange(no)
    k1 = np.arange(no)
    half = no // 2
    f1r, f1d, f1i = [], [], []
    for i in range(ni):
        ang = 2.0 * np.pi * np.outer(k1, jo * ni + i) / n_fft
        c, s = np.cos(ang), np.sin(ang)
        f1r.append(np.concatenate([c, -s], axis=0))
        ch, sh = c[:, :half], s[:, :half]
        f1d.append(np.block([[ch, sh], [-sh, ch]]))
        angi = 2.0 * np.pi * np.outer(np.arange(half) * ni + i, k1) / n_fft
        ci, si = np.cos(angi), np.sin(angi)
        f1i.append(np.block([[ci, -si], [si, ci]]))
    ang3 = 2.0 * np.pi * np.outer(np.arange(ni), np.arange(ni)) / ni
    c3, s3 = np.cos(ang3), np.sin(ang3)
    m3 = np.block([[c3, s3], [-s3, c3]])
    m3c = np.block([[c3, -s3], [s3, c3]])
    as_bf16 = lambda a: np.asarray(a, np.float32).astype(BF16)
    return (as_bf16(np.stack(f1r)), as_bf16(np.stack(f1d)), as_bf16(np.stack(f1i)),
            as_bf16(m3), as_bf16(m3c))


def _const_spec(shape, n_axes):
    zeros = (0,) * len(shape)
    if n_axes == 2:
        index = lambda a, b: zeros
    else:
        index = lambda a: zeros
    return pl.BlockSpec(shape, index, pipeline_mode=pl.Buffered(1))


SUBLANES = 8
DFT_UNROLL = 4


def _pitch(size):
    return size + SUBLANES


def _block_at(i, size):
    return pl.ds(pl.multiple_of(i * _pitch(size), SUBLANES), size)


def _dft_stage3_rhs(a_ref, k1, no, ni):
    re = a_ref[pl.ds(k1, ni, stride=_pitch(2 * no)), :]
    im = a_ref[pl.ds(no + k1, ni, stride=_pitch(2 * no)), :]
    return jnp.concatenate([re, im], axis=0).astype(BF16)


def _spectrum_kernel(h_ref, f1r_ref, m3_ref, o_ref, a_ref, *, no, ni):
    inv_n = 1.0 / (no * ni)

    def stage1(ji, carry):
        rhs = h_ref[0, pl.ds(ji, no, stride=ni), :].astype(BF16)
        a_ref[_block_at(ji, 2 * no), :] = jnp.dot(f1r_ref[ji], rhs, preferred_element_type=F32)
        return carry

    lax.fori_loop(0, ni, stage1, 0, unroll=DFT_UNROLL)

    def stage3(k1, carry):
        x = jnp.dot(m3_ref[...], _dft_stage3_rhs(a_ref, k1, no, ni), preferred_element_type=F32) * inv_n
        r0 = pl.multiple_of(k1 * ni, ni)
        o_ref[0, 0, pl.ds(r0, ni), :] = x[:ni].astype(o_ref.dtype)
        o_ref[0, 1, pl.ds(r0, ni), :] = x[ni:].astype(o_ref.dtype)
        return carry

    lax.fori_loop(0, no, stage3, 0, unroll=DFT_UNROLL)


def _filter_spectrum(hfull, no=FFT_NO, ni=FFT_NI, tc=HY_TC):
    n_ord, n_fft, width = hfull.shape
    f1r, _, _, m3, _ = _dft_constants(n_fft, no, ni)
    return pl.pallas_call(
        functools.partial(_spectrum_kernel, no=no, ni=ni),
        grid=(n_ord, width // tc),
        in_specs=[pl.BlockSpec((1, n_fft, tc), lambda o, c: (o, 0, c)),
                  _const_spec(f1r.shape, 2), _const_spec(m3.shape, 2)],
        out_specs=pl.BlockSpec((1, 2, n_fft, tc), lambda o, c: (o, 0, 0, c)),
        out_shape=jax.ShapeDtypeStruct((n_ord, 2, n_fft, width), BF16),
        scratch_shapes=[pltpu.VMEM((ni * _pitch(2 * no), tc), F32)],
        compiler_params=_cparams(2),
        name="hyena_spectrum",
    )(hfull, f1r, m3)


def _long_conv_kernel(v_ref, g_ref, spec_ref, bias_ref, f1d_ref, f1i_ref, m3_ref, m3c_ref,
                      o_ref, z_ref, a_ref, b_ref, *, n, no, ni):
    order = pl.program_id(1)
    half = no // 2
    assert n == half * ni

    @pl.when(order == 0)
    def _():
        def load(jo, carry):
            r0 = pl.multiple_of(jo * ni, ni)
            for b in range(2):
                z_ref[b, _block_at(jo, ni), :] = v_ref[pl.ds(b * n + r0, ni), :].astype(F32)
            return carry

        lax.fori_loop(0, half, load, 0, unroll=DFT_UNROLL)

    def stage1(ji, carry):
        zr = z_ref[0, pl.ds(ji, half, stride=_pitch(ni)), :]
        zi = z_ref[1, pl.ds(ji, half, stride=_pitch(ni)), :]
        rhs = jnp.concatenate([zr, zi], axis=0).astype(BF16)
        a_ref[_block_at(ji, 2 * no), :] = jnp.dot(f1d_ref[ji], rhs, preferred_element_type=F32)
        return carry

    lax.fori_loop(0, ni, stage1, 0, unroll=DFT_UNROLL)

    def stage3(k1, carry):
        x = jnp.dot(m3_ref[...], _dft_stage3_rhs(a_ref, k1, no, ni), preferred_element_type=F32)
        r0 = pl.multiple_of(k1 * ni, ni)
        hr = spec_ref[0, 0, pl.ds(r0, ni), :].astype(F32)
        hi = spec_ref[0, 1, pl.ds(r0, ni), :].astype(F32)
        xr, xi = x[:ni], x[ni:]
        y = jnp.concatenate([xr * hr - xi * hi, xr * hi + xi * hr], axis=0).astype(BF16)
        b_ref[_block_at(k1, 2 * ni), :] = jnp.dot(m3c_ref[...], y, preferred_element_type=F32)
        return carry

    lax.fori_loop(0, no, stage3, 0, unroll=DFT_UNROLL)

    def stage1_inv(t2, carry):
        br = b_ref[pl.ds(t2, no, stride=_pitch(2 * ni)), :]
        bi = b_ref[pl.ds(ni + t2, no, stride=_pitch(2 * ni)), :]
        rhs = jnp.concatenate([br, bi], axis=0).astype(BF16)
        a_ref[pl.ds(pl.multiple_of(t2 * _pitch(2 * no), SUBLANES), no), :] = jnp.dot(
            f1i_ref[t2], rhs, preferred_element_type=F32)
        return carry

    lax.fori_loop(0, ni, stage1_inv, 0, unroll=DFT_UNROLL)

    bias = bias_ref[0, 0]

    def finish(t1, carry):
        r0 = pl.multiple_of(t1 * ni, ni)
        conv = (a_ref[pl.ds(t1, ni, stride=_pitch(2 * no)), :],
                a_ref[pl.ds(half + t1, ni, stride=_pitch(2 * no)), :])
        for b in range(2):
            z = z_ref[b, _block_at(t1, ni), :]
            gate = g_ref[pl.ds(b * n + r0, ni), :].astype(F32)
            zn = gate * (conv[b] + z * bias)
            z_ref[b, _block_at(t1, ni), :] = zn
            o_ref[pl.ds(b * n + r0, ni), :] = zn.astype(o_ref.dtype)
        return carry

    lax.fori_loop(0, half, finish, 0, unroll=DFT_UNROLL)


def _long_conv(uc, spec, hy_bias, layer, n, no=FFT_NO, ni=FFT_NI, tc=HY_TC):
    n_fft = no * ni
    assert 2 * n == n_fft
    _, f1d, f1i, m3, m3c = _dft_constants(n_fft, no, ni)
    nct = HY_WIDTH // tc
    bias = hy_bias.reshape(hy_bias.shape[0], HY_ORDER, 1, HY_WIDTH)
    return pl.pallas_call(
        functools.partial(_long_conv_kernel, n=n, no=no, ni=ni),
        grid=(nct, HY_ORDER),
        in_specs=[pl.BlockSpec((2 * n, tc), lambda c, o: (0, c)),
                  pl.BlockSpec((2 * n, tc), lambda c, o: (0, (1 + o) * nct + c)),
                  pl.BlockSpec((1, 2, n_fft, tc), lambda c, o: (o, 0, 0, c)),
                  pl.BlockSpec((1, 1, 1, tc), lambda c, o: (layer, o, 0, c)),
                  _const_spec(f1d.shape, 2), _const_spec(f1i.shape, 2),
                  _const_spec(m3.shape, 2), _const_spec(m3c.shape, 2)],
        out_specs=pl.BlockSpec((2 * n, tc), lambda c, o: (0, c)),
        out_shape=jax.ShapeDtypeStruct((2 * n, HY_WIDTH), BF16),
        scratch_shapes=[pltpu.VMEM((2, (no // 2) * _pitch(ni), tc), F32),
                        pltpu.VMEM((ni * _pitch(2 * no), tc), F32),
                        pltpu.VMEM((no * _pitch(2 * ni), tc), F32)],
        compiler_params=_cparams(2),
        name="hyena_long_conv",
    )(uc, uc, spec, bias, f1d, f1i, m3, m3c)


@functools.lru_cache(maxsize=None)
def _small_dft_constants(n):
    n_fft = 2 * n
    k = np.arange(n_fft)
    ang = 2.0 * np.pi * np.outer(k, np.arange(n_fft)) / n_fft
    c, s = np.cos(ang), np.sin(ang)
    fr = np.concatenate([c, -s], axis=0)
    ch, sh = c[:, :n], s[:, :n]
    fd = np.block([[ch, sh], [-sh, ch]])
    ci, si = c[:n, :], s[:n, :]
    fi = np.block([[ci, -si], [si, ci]])
    as_bf16 = lambda a: np.asarray(a, np.float32).astype(BF16)
    return as_bf16(fr), as_bf16(fd), as_bf16(fi)


def _small_conv_kernel(v_ref, g1_ref, g2_ref, h_ref, bias_ref, fr_ref, fd_ref, fi_ref, o_ref, *, n):
    n_fft = 2 * n
    z = v_ref[...].astype(F32)
    for o, g_ref in enumerate((g1_ref, g2_ref)):
        hs = jnp.dot(fr_ref[...], h_ref[o].astype(BF16), preferred_element_type=F32) * (1.0 / n_fft)
        x = jnp.dot(fd_ref[...], z.astype(BF16), preferred_element_type=F32)
        xr, xi, hr, hi = x[:n_fft], x[n_fft:], hs[:n_fft], hs[n_fft:]
        y = jnp.concatenate([xr * hr - xi * hi, xr * hi + xi * hr], axis=0).astype(BF16)
        conv = jnp.dot(fi_ref[...], y, preferred_element_type=F32)
        z = g_ref[...].astype(F32) * (conv + z * bias_ref[0, o])
    o_ref[...] = z.astype(o_ref.dtype)


def _small_conv(uc, hfull, hy_bias, layer, n, tc=256):
    fr, fd, fi = _small_dft_constants(n)
    nct = HY_WIDTH // tc
    bias = hy_bias.reshape(hy_bias.shape[0], HY_ORDER, 1, HY_WIDTH)
    col = lambda k: pl.BlockSpec((2 * n, tc), lambda c: (0, k * nct + c))
    return pl.pallas_call(
        functools.partial(_small_conv_kernel, n=n),
        grid=(nct,),
        in_specs=[col(0), col(1), col(2),
                  pl.BlockSpec((HY_ORDER, 2 * n, tc), lambda c: (0, 0, c)),
                  pl.BlockSpec((1, HY_ORDER, 1, tc), lambda c: (layer, 0, 0, c)),
                  _const_spec(fr.shape, 1), _const_spec(fd.shape, 1), _const_spec(fi.shape, 1)],
        out_specs=pl.BlockSpec((2 * n, tc), lambda c: (0, c)),
        out_shape=jax.ShapeDtypeStruct((2 * n, HY_WIDTH), BF16),
        compiler_params=_cparams(1),
        name="hyena_small_conv",
    )(uc, uc, uc, hfull, bias, fr, fd, fi)


def _hyena(u, conv_w, conv_b, filt, hy_bias, layer, n_seq, seg, row_blk_off):
    assert n_seq == 2
    uc = _short_conv(u, conv_w, conv_b, layer, n_seq, seg, row_blk_off)
    hfull = _hyena_filter(seg, 2 * seg, *filt)
    if 2 * seg == FFT_NO * FFT_NI:
        return _long_conv(uc, _filter_spectrum(hfull), hy_bias, layer, seg)
    return _small_conv(uc, hfull, hy_bias, layer, seg)


def _moe(h2, logits, w1, w3, w2):
    t, d = h2.shape
    dff = w1.shape[-1]
    top_logit, top_idx = lax.top_k(logits, TOP_K)
    gate = jax.nn.softmax(top_logit, axis=-1)
    n_assign = t * TOP_K
    flat_e = top_idx.reshape(-1)
    flat_tok = jnp.repeat(jnp.arange(t, dtype=jnp.int32), TOP_K)
    order = jnp.argsort(flat_e)
    e_sorted = flat_e[order]
    counts = jnp.bincount(flat_e, length=N_EXPERTS)
    padded = (counts + MOE_BLOCK - 1) // MOE_BLOCK * MOE_BLOCK
    start = jnp.cumsum(counts) - counts
    pad_end = jnp.cumsum(padded)
    pad_start = pad_end - padded
    dest = (pad_start[e_sorted] + jnp.arange(n_assign) - start[e_sorted]).astype(jnp.int32)
    n_blocks = -(-n_assign // MOE_BLOCK) + N_EXPERTS
    n_slots = n_blocks * MOE_BLOCK
    slot_tok = jnp.full((n_slots,), t, jnp.int32).at[dest].set(flat_tok[order])
    block_expert = jnp.minimum(
        jnp.searchsorted(pad_end, jnp.arange(n_blocks) * MOE_BLOCK, side='right'),
        N_EXPERTS - 1).astype(jnp.int32)
    n_used = (pad_end[-1] // MOE_BLOCK).astype(jnp.int32).reshape(1)
    h_pad = jnp.concatenate([h2, jnp.zeros((1, d), h2.dtype)], axis=0)
    xs = h_pad[slot_tok]
    act = _gmm(xs, (w1, w3), block_expert, n_used, k=d, n=dff, tm=MOE_BLOCK, tn=512,
               out_dtype=BF16, name="moe_up")
    ys = _gmm(act, (w2,), block_expert, n_used, k=dff, n=d, tm=MOE_BLOCK, tn=1024,
              out_dtype=F32, name="moe_down")
    pos = jnp.zeros((n_assign,), jnp.int32).at[order].set(dest).reshape(t, TOP_K)
    return gate[:, 0:1] * ys[pos[:, 0]] + gate[:, 1:2] * ys[pos[:, 1]]


def kernel(x, c, ctx, c_ctx, w_ada, b_ada, norm_g, w_in, attn_sink, conv_w, conv_b,
           filt_w1, filt_b1, filt_w2, filt_b2, filt_w3, filt_b3, filt_freq, filt_w_out, hyena_bias,
           w_attn_out, w_hyena_out, w_out, ffn_w1, ffn_w3, ffn_w2,
           moe_router, moe_w1, moe_w3, moe_w2):
    batch, seq, d = x.shape
    n_ctx = ctx.shape[1]
    depth = w_in.shape[0]
    in_width = w_in.shape[-1]
    n_lat = batch * seq
    n_all = n_lat + batch * n_ctx
    n_fft = 2 * seq
    ga_off = HY_OFF + (HY_ORDER + 1) * HY_WIDTH
    gh_off = ga_off + d
    assert batch == 2 and FFT_NO * FFT_NI == n_fft

    x_all = jnp.concatenate([x.reshape(n_lat, d), ctx.reshape(batch * n_ctx, d)], axis=0)
    cond = jnp.concatenate([c, c_ctx[None], jnp.zeros((8 - batch - 1, d), F32)], axis=0)
    mod = _ada(cond, w_ada, b_ada)
    mod = mod.reshape(depth, 8, 6, 1, d)
    rope_tabs = _rope_tables(seq)
    tm_all = n_all // 8
    tm_lat = n_lat // 8

    def mods(l, j):
        return mod[l, :, j]

    h = _norm_mod(x_all, norm_g[0, 0], mods(0, 0), mods(0, 1), seq)
    for l in range(depth):
        last = l == depth - 1
        rows = n_lat if last else n_all
        tm = tm_lat if last else tm_all
        filt = (filt_w1[l], filt_b1[l], filt_w2[l], filt_b2[l], filt_w3[l], filt_b3[l],
                filt_freq[l], filt_w_out[l])

        u = _gmm(h, (w_in,), *_dense_ids(n_all, tm_all, l), k=d, n=in_width, tm=tm_all, tn=512,
                 out_dtype=BF16, name="in_proj")
        qk = _rope(u, rope_tabs, n_lat, seq, V_OFF)
        att = _win_attn(qk, u, attn_sink[l], batch, seq, n_ctx)
        hy = _hyena(u, conv_w, conv_b, filt, hyena_bias, l, batch, seq, 0)
        if not last:
            att = jnp.concatenate([att, _ctx_attn(u, attn_sink[l], batch, seq, n_ctx)], axis=0)
            hy_c = _hyena(u, conv_w, conv_b, filt, hyena_bias, l, batch, n_ctx, n_lat // n_ctx)
            hy = jnp.concatenate([hy, hy_c], axis=0)
        mrg = _merge(att, hy, u, w_attn_out, w_hyena_out, l, rows, ga_off, gh_off, tm)
        y = _gmm(mrg, (w_out,), *_dense_ids(rows, tm, l), k=d, n=d, tm=tm, tn=512,
                 out_dtype=F32, name="out_proj")
        router = None
        if l % 2 == 1:
            router = jnp.pad(moe_router[l // 2], ((0, 0), (0, LANES - N_EXPERTS)))
        res = _post(x_all, y, mods(l, 2), norm_g[l, 1], seq, rows,
                    nxt=(norm_g[l, 2], mods(l, 3), mods(l, 4)), router=router)
        x_all, h2 = res[0], res[1]

        if l % 2 == 0:
            i = l // 2
            dff = ffn_w1.shape[-1]
            act = _gmm(h2, (ffn_w1, ffn_w3), *_dense_ids(rows, tm, i), k=d, n=dff, tm=tm, tn=256,
                       out_dtype=BF16, name="ffn_up")
            f = _gmm(act, (ffn_w2,), *_dense_ids(rows, tm // 4, i), k=dff, n=d, tm=tm // 4, tn=512,
                     out_dtype=F32, w_single_buffer=True, name="ffn_down")
        else:
            i = l // 2
            f = _moe(h2, res[2][:, :N_EXPERTS], moe_w1[i], moe_w3[i], moe_w2[i])
        if last:
            (x_all,) = _post(x_all, f, mods(l, 5), norm_g[l, 3], seq, rows)
        else:
            x_all, h = _post(x_all, f, mods(l, 5), norm_g[l, 3], seq, rows,
                             nxt=(norm_g[l + 1, 0], mods(l + 1, 0), mods(l + 1, 1)))
    return x_all[:n_lat].reshape(batch, seq, d)
```

```python
import functools
import math

import numpy as np
import jax
import jax.numpy as jnp
from jax import lax
from jax.experimental import pallas as pl
from jax.experimental.pallas import tpu as pltpu

F32 = jnp.float32
BF16 = jnp.bfloat16

GRID_W = 64
N_Q_HEADS = 16
N_KV_HEADS = 4
HEAD_DIM = 128
Q_GROUP = N_Q_HEADS // N_KV_HEADS
ATT_WIDTH = N_Q_HEADS * HEAD_DIM
KV_WIDTH = N_KV_HEADS * HEAD_DIM
BLOCK = 128
ROPE_BASE = 10000.0
ROPE_FREQS = HEAD_DIM // 4
MASK_VALUE = -1e30
HY_WIDTH = 2048
HY_ORDER = 2
SHORT_CONV = 3
FILTER_BANDS = 16
FILTER_HIDDEN = 64
DECAY_TARGET = 1e-2
MIN_DECAY = math.log(DECAY_TARGET) / 1.5
MAX_DECAY = math.log(DECAY_TARGET) / 0.3
K_OFF = ATT_WIDTH
V_OFF = K_OFF + KV_WIDTH
HY_OFF = V_OFF + KV_WIDTH
N_EXPERTS = 8
TOP_K = 2
MOE_ROWS = 512
RMS_EPS = 1e-6

LANES = 128
SUBLANES = 8
VMEM_LIMIT_BYTES = 56 * 1024 * 1024

FFT_NO = 128
FFT_NI = 64
HY_TC = 128


def _cparams(n_axes):
    return pltpu.CompilerParams(dimension_semantics=("arbitrary",) * n_axes,
                                vmem_limit_bytes=VMEM_LIMIT_BYTES)


ADA_CHUNK = 64


def _ada_kernel(c_ref, w_ref, b_ref, o_ref, act_ref):
    n_rows, k, _ = c_ref.shape
    tn = w_ref.shape[-1]

    @pl.when(jnp.logical_and(pl.program_id(0) == 0, pl.program_id(1) == 0))
    def _():
        c = c_ref[...]
        act_ref[...] = c * jax.nn.sigmoid(c)

    def body(i, accs):
        r0 = pl.multiple_of(i * ADA_CHUNK, ADA_CHUNK)
        w = w_ref[0, pl.ds(r0, ADA_CHUNK), :]
        out = []
        for r in range(n_rows):
            a = act_ref[r, pl.ds(r0, ADA_CHUNK), :]
            acc = accs[r]
            for s in range(ADA_CHUNK // SUBLANES):
                rows = slice(s * SUBLANES, (s + 1) * SUBLANES)
                acc = acc + w[rows] * jnp.concatenate([a[rows]] * (tn // LANES), axis=1)
            out.append(acc)
        return tuple(out)

    zero = jnp.zeros((SUBLANES, tn), F32)
    accs = lax.fori_loop(0, k // ADA_CHUNK, body, (zero,) * n_rows)
    bias = b_ref[0]
    for r in range(n_rows):
        o_ref[0, r:r + 1, :] = jnp.sum(accs[r], axis=0, keepdims=True) + bias


def _ada(cond, w_ada, b_ada, tn=512):
    n_layers, d, n = w_ada.shape
    rows = cond.shape[0]
    cb = jnp.broadcast_to(cond[:, :, None], (rows, d, LANES))
    return pl.pallas_call(
        _ada_kernel,
        grid=(n_layers, n // tn),
        in_specs=[pl.BlockSpec((rows, d, LANES), lambda l, j: (0, 0, 0)),
                  pl.BlockSpec((1, d, tn), lambda l, j: (l, 0, j)),
                  pl.BlockSpec((1, 1, tn), lambda l, j: (l, 0, j))],
        out_specs=pl.BlockSpec((1, rows, tn), lambda l, j: (l, 0, j)),
        out_shape=jax.ShapeDtypeStruct((n_layers, rows, n), F32),
        scratch_shapes=[pltpu.VMEM((rows, d, LANES), F32)],
        compiler_params=_cparams(2),
        name="ada_mod",
    )(cb, w_ada, b_ada.reshape(n_layers, 1, n))


def _rms(x):
    return x * lax.rsqrt(jnp.mean(x * x, axis=-1, keepdims=True) + RMS_EPS)


def _norm_mod_kernel(x_ref, g_ref, sh_ref, sc_ref, o_ref):
    y = _rms(x_ref[...]) * g_ref[...]
    o_ref[...] = (y * (1.0 + sc_ref[0]) + sh_ref[0]).astype(o_ref.dtype)


def _group_of_tile(tr, seq):
    return lambda i: (jnp.minimum((i * tr) // seq, 2), 0, 0)


def _norm_mod(x_all, g, sh, sc, seq, tr=256):
    t, d = x_all.shape
    grp = _group_of_tile(tr, seq)
    return pl.pallas_call(
        _norm_mod_kernel,
        grid=(t // tr,),
        in_specs=[pl.BlockSpec((tr, d), lambda i: (i, 0)),
                  pl.BlockSpec((1, d), lambda i: (0, 0)),
                  pl.BlockSpec((1, 1, d), grp),
                  pl.BlockSpec((1, 1, d), grp)],
        out_specs=pl.BlockSpec((tr, d), lambda i: (i, 0)),
        out_shape=jax.ShapeDtypeStruct((t, d), BF16),
        compiler_params=_cparams(1),
        name="norm_mod",
    )(x_all, g.reshape(1, d), sh, sc)


def _post_kernel(x_ref, *rest, with_next, with_router, with_pair):
    if with_pair:
        y0_ref, y1_ref, w_ref, gt_ref, gpost_ref, *rest = rest
        w = w_ref[...]
        y = w[:, 0:1] * y0_ref[...].astype(F32) + w[:, 1:2] * y1_ref[...].astype(F32)
    else:
        y_ref, gt_ref, gpost_ref, *rest = rest
        y = y_ref[...].astype(F32)
    xn = x_ref[...] + gt_ref[0] * (_rms(y) * gpost_ref[...])
    if not with_next:
        (xo_ref,) = rest
        xo_ref[...] = xn
        return
    if with_router:
        gpre_ref, sh_ref, sc_ref, wr_ref, xo_ref, ho_ref, lg_ref = rest
    else:
        gpre_ref, sh_ref, sc_ref, xo_ref, ho_ref = rest
    xo_ref[...] = xn
    h = (_rms(xn) * gpre_ref[...]) * (1.0 + sc_ref[0]) + sh_ref[0]
    ho_ref[...] = h.astype(ho_ref.dtype)
    if with_router:
        lg_ref[...] = jnp.dot(h, wr_ref[...], preferred_element_type=F32,
                              precision=lax.Precision.HIGHEST)


def _post(x_all, y, gt, g_post, seq, rows, nxt=None, router=None, tr=256):
    t, d = x_all.shape
    grp = _group_of_tile(tr, seq)
    row = pl.BlockSpec((tr, d), lambda i: (i, 0))
    vec = pl.BlockSpec((1, d), lambda i: (0, 0))
    mod = pl.BlockSpec((1, 1, d), grp)
    with_pair = isinstance(y, tuple)
    if with_pair:
        y0, y1, w = y
        in_specs = [row, row, row, pl.BlockSpec((tr, w.shape[1]), lambda i: (i, 0)), mod, vec]
        args = [x_all, y0, y1, w, gt, g_post.reshape(1, d)]
    else:
        in_specs = [row, row, mod, vec]
        args = [x_all, y, gt, g_post.reshape(1, d)]
    out_specs = [row]
    out_shape = [jax.ShapeDtypeStruct((rows, d), F32)]
    if nxt is not None:
        g_pre, sh, sc = nxt
        in_specs += [vec, mod, mod]
        args += [g_pre.reshape(1, d), sh, sc]
        out_specs.append(row)
        out_shape.append(jax.ShapeDtypeStruct((rows, d), BF16))
        if router is not None:
            in_specs.append(pl.BlockSpec((d, LANES), lambda i: (0, 0)))
            args.append(router)
            out_specs.append(pl.BlockSpec((tr, LANES), lambda i: (i, 0)))
            out_shape.append(jax.ShapeDtypeStruct((rows, LANES), F32))
    return pl.pallas_call(
        functools.partial(_post_kernel, with_next=nxt is not None, with_router=router is not None,
                          with_pair=with_pair),
        grid=(rows // tr,),
        in_specs=in_specs,
        out_specs=out_specs,
        out_shape=out_shape,
        compiler_params=_cparams(1),
        name="post_norm",
    )(*args)


def _weight_changed(be_ref, m):
    return jnp.logical_or(m == 0, be_ref[m] != be_ref[jnp.maximum(m - 1, 0)])


def _gmm_kernel(be_ref, nv_ref, a_ref, w_ref, o_ref, wb_ref):
    m = pl.program_id(1)

    @pl.when(m < nv_ref[0])
    def _():
        @pl.when(_weight_changed(be_ref, m))
        def _():
            wb_ref[...] = w_ref[0].astype(BF16)

        o_ref[...] = jnp.dot(a_ref[...], wb_ref[...],
                             preferred_element_type=F32).astype(o_ref.dtype)


def _swiglu_kernel(be_ref, nv_ref, a_ref, w1_ref, w3_ref, o_ref, w1b_ref, w3b_ref):
    m = pl.program_id(1)

    @pl.when(m < nv_ref[0])
    def _():
        @pl.when(_weight_changed(be_ref, m))
        def _():
            w1b_ref[...] = w1_ref[0].astype(BF16)
            w3b_ref[...] = w3_ref[0].astype(BF16)

        a = a_ref[...]
        g = jnp.dot(a, w1b_ref[...], preferred_element_type=F32)
        u = jnp.dot(a, w3b_ref[...], preferred_element_type=F32)
        o_ref[...] = (g * jax.nn.sigmoid(g) * u).astype(o_ref.dtype)


def _gmm(a, ws, be, nvalid, *, k, n, tm, tn, out_dtype, w_col_off=0, rows=None, w_single_buffer=False,
         name="gmm"):
    rows = a.shape[0] if rows is None else rows
    assert rows % tm == 0 and n % tn == 0 and w_col_off % tn == 0
    off = w_col_off // tn
    w_mode = dict(pipeline_mode=pl.Buffered(1)) if w_single_buffer else {}

    def a_map(j, m, be_ref, nv_ref):
        return (jnp.minimum(m, nv_ref[0] - 1), 0)

    def w_map(j, m, be_ref, nv_ref):
        return (be_ref[jnp.minimum(m, nv_ref[0] - 1)], 0, j + off)

    def o_map(j, m, be_ref, nv_ref):
        return (m, j)

    kernel = _gmm_kernel if len(ws) == 1 else _swiglu_kernel
    return pl.pallas_call(
        kernel,
        grid_spec=pltpu.PrefetchScalarGridSpec(
            num_scalar_prefetch=2,
            grid=(n // tn, rows // tm),
            in_specs=[pl.BlockSpec((tm, k), a_map)] + [pl.BlockSpec((1, k, tn), w_map, **w_mode)] * len(ws),
            out_specs=pl.BlockSpec((tm, tn), o_map),
            scratch_shapes=[pltpu.VMEM((k, tn), BF16)] * len(ws)),
        out_shape=jax.ShapeDtypeStruct((rows, n), out_dtype),
        compiler_params=_cparams(2),
        name=name,
    )(be, nvalid, a, *ws)


def _dense_ids(rows, tm, idx):
    nb = rows // tm
    return jnp.full((nb,), idx, jnp.int32), jnp.full((1,), nb, jnp.int32)


def _merge_kernel(att_ref, hy_ref, ga_ref, gh_ref, wa_ref, wh_ref, o_ref, wab_ref, whb_ref):
    @pl.when(pl.program_id(1) == 0)
    def _():
        wab_ref[...] = wa_ref[0].astype(BF16)
        whb_ref[...] = wh_ref[0].astype(BF16)

    pa = jnp.dot(att_ref[...], wab_ref[...], preferred_element_type=F32)
    ph = jnp.dot(hy_ref[...], whb_ref[...], preferred_element_type=F32)
    ga = jax.nn.sigmoid(ga_ref[...].astype(F32))
    gh = jax.nn.sigmoid(gh_ref[...].astype(F32))
    o_ref[...] = (ga * pa + gh * ph).astype(o_ref.dtype)


def _merge(att, hy, u, w_ao, w_ho, layer, rows, ga_off, gh_off, tm, tn=512):
    ka, d = w_ao.shape[1:]
    kh = w_ho.shape[1]
    return pl.pallas_call(
        _merge_kernel,
        grid=(d // tn, rows // tm),
        in_specs=[pl.BlockSpec((tm, ka), lambda j, m: (m, 0)),
                  pl.BlockSpec((tm, kh), lambda j, m: (m, 0)),
                  pl.BlockSpec((tm, tn), lambda j, m: (m, ga_off // tn + j)),
                  pl.BlockSpec((tm, tn), lambda j, m: (m, gh_off // tn + j)),
                  pl.BlockSpec((1, ka, tn), lambda j, m: (layer, 0, j)),
                  pl.BlockSpec((1, kh, tn), lambda j, m: (layer, 0, j))],
        out_specs=pl.BlockSpec((tm, tn), lambda j, m: (m, j)),
        out_shape=jax.ShapeDtypeStruct((rows, d), BF16),
        scratch_shapes=[pltpu.VMEM((ka, tn), BF16), pltpu.VMEM((kh, tn), BF16)],
        compiler_params=_cparams(2),
        name="gated_merge",
    )(att, hy, u, u, w_ao, w_ho)


def _rope_tables(seq):
    rows = seq // GRID_W
    row = jnp.repeat(jnp.arange(rows), GRID_W).astype(F32)
    col = jnp.tile(jnp.arange(GRID_W), rows).astype(F32)
    inv = ROPE_BASE ** (-jnp.arange(ROPE_FREQS, dtype=F32) / ROPE_FREQS)
    ang = jnp.stack([row[:, None] * inv, col[:, None] * inv], axis=1)
    cos, sin = jnp.cos(ang), jnp.sin(ang)
    zero = jnp.zeros_like(sin)
    cos_t = jnp.stack([cos, cos], axis=2).reshape(seq, HEAD_DIM)
    s_lo = jnp.stack([-sin, zero], axis=2).reshape(seq, HEAD_DIM)
    s_hi = jnp.stack([zero, sin], axis=2).reshape(seq, HEAD_DIM)
    return cos_t, s_lo, s_hi


def _rope_tables_for_rows(seq, batch, n_extra):
    cos_t, s_lo, s_hi = (jnp.tile(t, (batch, 1)) for t in _rope_tables(seq))
    if n_extra:
        cos_t = jnp.concatenate([cos_t, jnp.ones((n_extra, HEAD_DIM), F32)], axis=0)
        pad = jnp.zeros((n_extra, HEAD_DIM), F32)
        s_lo, s_hi = jnp.concatenate([s_lo, pad], axis=0), jnp.concatenate([s_hi, pad], axis=0)
    return cos_t, s_lo, s_hi


def _in_proj_kernel(a_ref, w_ref, c_ref, lo_ref, hi_ref, o_ref, wb_ref, *, n_rope_tiles):
    j = pl.program_id(0)

    @pl.when(pl.program_id(1) == 0)
    def _():
        wb_ref[...] = w_ref[0].astype(BF16)

    acc = jnp.dot(a_ref[...], wb_ref[...], preferred_element_type=F32)

    @pl.when(j < n_rope_tiles)
    def _():
        c, lo, hi = c_ref[...], lo_ref[...], hi_ref[...]
        for hd in range(acc.shape[1] // HEAD_DIM):
            sl = slice(hd * HEAD_DIM, (hd + 1) * HEAD_DIM)
            x = acc[:, sl]
            up = pltpu.roll(x, HEAD_DIM - ROPE_FREQS, 1)
            dn = pltpu.roll(x, ROPE_FREQS, 1)
            o_ref[:, sl] = (x * c + up * lo + dn * hi).astype(o_ref.dtype)

    @pl.when(j >= n_rope_tiles)
    def _():
        o_ref[...] = acc.astype(o_ref.dtype)


def _in_proj(h, w_in, layer, tables, rows, tm, tn=512):
    k, n = w_in.shape[1:]
    assert rows % tm == 0 and n % tn == 0 and V_OFF % tn == 0
    n_rope_tiles = V_OFF // tn
    tab = pl.BlockSpec((tm, HEAD_DIM), lambda j, m: (jnp.where(j < n_rope_tiles, m, 0), 0))
    return pl.pallas_call(
        functools.partial(_in_proj_kernel, n_rope_tiles=n_rope_tiles),
        grid=(n // tn, rows // tm),
        in_specs=[pl.BlockSpec((tm, k), lambda j, m: (m, 0)),
                  pl.BlockSpec((1, k, tn), lambda j, m: (layer, 0, j)),
                  tab, tab, tab],
        out_specs=pl.BlockSpec((tm, tn), lambda j, m: (m, j)),
        out_shape=jax.ShapeDtypeStruct((rows, n), BF16),
        scratch_shapes=[pltpu.VMEM((k, tn), BF16)],
        compiler_params=_cparams(2),
        name="in_proj",
    )(h, w_in, *tables)


def _scores(q, k):
    return lax.dot_general(q, k, (((1,), (1,)), ((), ())), preferred_element_type=F32)


def _win_attn_kernel(sink_ref, q_ref, kp_ref, kc_ref, kn_ref, vp_ref, vc_ref, vn_ref,
                     kx_ref, vx_ref, o_ref):
    n = pl.program_id(1)
    nb = pl.num_programs(1)
    scale = HEAD_DIM ** -0.5
    n_keys = 3 * BLOCK + kx_ref.shape[0]
    qi = lax.broadcasted_iota(jnp.int32, (BLOCK, n_keys), 0)
    kj = lax.broadcasted_iota(jnp.int32, (BLOCK, n_keys), 1)
    bad_prev = jnp.logical_and(kj < BLOCK, jnp.logical_or(kj < qi, n == 0))
    bad_next = jnp.logical_and(jnp.logical_and(kj >= 2 * BLOCK, kj < 3 * BLOCK),
                               jnp.logical_or(kj - 2 * BLOCK > qi, n == nb - 1))
    valid = jnp.logical_not(jnp.logical_or(bad_prev, bad_next))
    for h in range(N_KV_HEADS):
        hs = slice(h * HEAD_DIM, (h + 1) * HEAD_DIM)
        keys = jnp.concatenate([kp_ref[:, hs], kc_ref[:, hs], kn_ref[:, hs], kx_ref[:, hs]], axis=0)
        vals = jnp.concatenate([vp_ref[:, hs], vc_ref[:, hs], vn_ref[:, hs], vx_ref[:, hs]], axis=0)
        for g in range(Q_GROUP):
            head = h * Q_GROUP + g
            sl = slice(head * HEAD_DIM, (head + 1) * HEAD_DIM)
            s = jnp.where(valid, _scores(q_ref[:, sl], keys) * scale, MASK_VALUE)
            sink = sink_ref[h, g]
            mx = jnp.maximum(jnp.max(s, axis=-1, keepdims=True), sink)
            p = jnp.exp(s - mx)
            den = jnp.sum(p, axis=-1, keepdims=True) + jnp.exp(sink - mx)
            o = jnp.dot(p.astype(BF16), vals, preferred_element_type=F32)
            o_ref[:, sl] = (o / den).astype(o_ref.dtype)


def _win_attn(u, kv_ctx, ctx_row_blk, ctx_k_col, sink, batch, seq, n_ctx):
    nb = seq // BLOCK
    assert K_OFF % KV_WIDTH == 0 and V_OFF % KV_WIDTH == 0 and ctx_k_col % KV_WIDTH == 0
    kcol = K_OFF // KV_WIDTH
    vcol = V_OFF // KV_WIDTH
    xk = ctx_k_col // KV_WIDTH

    def blk(shift, col):
        def index(b, n):
            return (b * nb + jnp.clip(n + shift, 0, nb - 1), col)
        return pl.BlockSpec((BLOCK, KV_WIDTH), index)

    return pl.pallas_call(
        _win_attn_kernel,
        grid=(batch, nb),
        in_specs=[pl.BlockSpec(memory_space=pltpu.SMEM),
                  pl.BlockSpec((BLOCK, ATT_WIDTH), lambda b, n: (b * nb + n, 0)),
                  blk(-1, kcol), blk(0, kcol), blk(1, kcol),
                  blk(-1, vcol), blk(0, vcol), blk(1, vcol),
                  pl.BlockSpec((n_ctx, KV_WIDTH), lambda b, n: (ctx_row_blk + b, xk)),
                  pl.BlockSpec((n_ctx, KV_WIDTH), lambda b, n: (ctx_row_blk + b, xk + 1))],
        out_specs=pl.BlockSpec((BLOCK, ATT_WIDTH), lambda b, n: (b * nb + n, 0)),
        out_shape=jax.ShapeDtypeStruct((batch * seq, ATT_WIDTH), BF16),
        compiler_params=_cparams(2),
        name="window_attention",
    )(sink.reshape(N_KV_HEADS, Q_GROUP), u, u, u, u, u, u, u, kv_ctx, kv_ctx)


def _ctx_attn_kernel(sink_ref, q_ref, k_ref, v_ref, o_ref):
    h = pl.program_id(1)
    scale = HEAD_DIM ** -0.5
    k, v = k_ref[...], v_ref[...]
    for g in range(Q_GROUP):
        sl = slice(g * HEAD_DIM, (g + 1) * HEAD_DIM)
        s = _scores(q_ref[:, sl], k) * scale
        sink = sink_ref[h, g]
        mx = jnp.maximum(jnp.max(s, axis=-1, keepdims=True), sink)
        p = jnp.exp(s - mx)
        den = jnp.sum(p, axis=-1, keepdims=True) + jnp.exp(sink - mx)
        o = jnp.dot(p.astype(BF16), v, preferred_element_type=F32)
        o_ref[:, sl] = (o / den).astype(o_ref.dtype)


def _ctx_attn(u, sink, batch, seq, n_ctx):
    hd = HEAD_DIM
    qw = Q_GROUP * hd
    ctx_blk = (batch * seq) // n_ctx
    return pl.pallas_call(
        _ctx_attn_kernel,
        grid=(batch, N_KV_HEADS),
        in_specs=[pl.BlockSpec(memory_space=pltpu.SMEM),
                  pl.BlockSpec((n_ctx, qw), lambda b, h: (ctx_blk + b, h)),
                  pl.BlockSpec((n_ctx, hd), lambda b, h: (ctx_blk + b, K_OFF // hd + h)),
                  pl.BlockSpec((n_ctx, hd), lambda b, h: (ctx_blk + b, V_OFF // hd + h))],
        out_specs=pl.BlockSpec((n_ctx, qw), lambda b, h: (b, h)),
        out_shape=jax.ShapeDtypeStruct((batch * n_ctx, ATT_WIDTH), BF16),
        compiler_params=_cparams(2),
        name="context_attention",
    )(sink.reshape(N_KV_HEADS, Q_GROUP), u, u, u)


def _short_conv_kernel(u_ref, w_ref, b_ref, o_ref):
    x = u_ref[...].astype(F32)
    n = x.shape[0]
    r = lax.broadcasted_iota(jnp.int32, x.shape, 0)
    prev = jnp.where(r == 0, 0.0, pltpu.roll(x, 1, 0))
    nxt = jnp.where(r == n - 1, 0.0, pltpu.roll(x, n - 1, 0))
    w = w_ref[0]
    o_ref[...] = (prev * w[0:1] + x * w[1:2] + nxt * w[2:3] + b_ref[0]).astype(o_ref.dtype)


def _short_conv(u, conv_w, conv_b, layer, n_seq, seg, row_blk_off, tw=256):
    width = conv_w.shape[-1]
    cb = conv_b.reshape(conv_b.shape[0], 1, width)
    return pl.pallas_call(
        _short_conv_kernel,
        grid=(n_seq, width // tw),
        in_specs=[pl.BlockSpec((seg, tw), lambda s, j: (row_blk_off + s, HY_OFF // tw + j)),
                  pl.BlockSpec((1, SHORT_CONV, tw), lambda s, j: (layer, 0, j)),
                  pl.BlockSpec((1, 1, tw), lambda s, j: (layer, 0, j))],
        out_specs=pl.BlockSpec((seg, tw), lambda s, j: (s, j)),
        out_shape=jax.ShapeDtypeStruct((n_seq * seg, width), BF16),
        compiler_params=_cparams(2),
        name="short_conv",
    )(u, conv_w, cb)


def _filter_positions(n, n_fft):
    t = jnp.linspace(0.0, 1.0, n, dtype=F32)[:, None]
    w = 2.0 * math.pi * jnp.arange(n, dtype=F32)[:, None] / n
    bands = jnp.linspace(1e-4, FILTER_BANDS - 1, FILTER_BANDS, dtype=F32)[None, :]
    z = jnp.concatenate([t, jnp.cos(bands * w), -jnp.sin(bands * w)], axis=-1)
    zt = jnp.concatenate([z, t], axis=-1)
    mid = jnp.zeros((n_fft - 2 * n + 1, zt.shape[1]), F32)
    full = jnp.concatenate([zt, mid, zt[1:][::-1]], axis=0)
    feat = jnp.pad(full[:, :-1], ((0, 0), (0, FILTER_HIDDEN - (zt.shape[1] - 1))))
    return feat, full[:, -1:]


def _filter_kernel(z_ref, t_ref, w1_ref, b1_ref, w2_ref, b2_ref, w3_ref, b3_ref, fr_ref,
                   wo_ref, dl_ref, o_ref, hid_ref, *, n, n_fft):
    hi = lax.Precision.HIGHEST
    tr = z_ref.shape[0]

    @pl.when(pl.program_id(1) == 0)
    def _():
        fr = fr_ref[...]
        h = jnp.sin(fr * (jnp.dot(z_ref[...], w1_ref[...], preferred_element_type=F32, precision=hi)
                          + b1_ref[...]))
        h = jnp.sin(fr * (jnp.dot(h, w2_ref[...], preferred_element_type=F32, precision=hi)
                          + b2_ref[...]))
        hid_ref[...] = jnp.sin(fr * (jnp.dot(h, w3_ref[...], preferred_element_type=F32, precision=hi)
                                     + b3_ref[...]))

    h = hid_ref[...]
    row = pl.program_id(0) * tr + lax.broadcasted_iota(jnp.int32, (tr, 1), 0)
    live = jnp.logical_or(row < n, row > n_fft - n)
    decay = jnp.where(live, jnp.exp(-t_ref[...] * dl_ref[...]), 0.0)
    for o in range(HY_ORDER):
        o_ref[o] = jnp.dot(h, wo_ref[o, 0], preferred_element_type=F32, precision=hi) * decay


def _hyena_filter(n, n_fft, w1, b1, w2, b2, w3, b3, freq, w_out, tw=512):
    tr = min(512, n)
    assert n % tr == 0 and n_fft % tr == 0
    feat, tpos = _filter_positions(n, n_fft)
    hid = FILTER_HIDDEN
    w1p = jnp.pad(w1, ((0, hid - w1.shape[0]), (0, 0)))
    wo = w_out.reshape(hid, HY_ORDER, 2, HY_WIDTH).transpose(1, 2, 0, 3)
    deltas = jnp.abs(jnp.linspace(MIN_DECAY, MAX_DECAY, HY_WIDTH, dtype=F32)).reshape(1, HY_WIDTH)
    small = lambda shape: pl.BlockSpec(shape, lambda r, j: (0,) * len(shape))
    return pl.pallas_call(
        functools.partial(_filter_kernel, n=n, n_fft=n_fft),
        grid=(n_fft // tr, HY_WIDTH // tw),
        in_specs=[pl.BlockSpec((tr, hid), lambda r, j: (r, 0)),
                  pl.BlockSpec((tr, 1), lambda r, j: (r, 0)),
                  small((hid, hid)), small((1, hid)), small((hid, hid)), small((1, hid)),
                  small((hid, hid)), small((1, hid)), small((1, hid)),
                  pl.BlockSpec((HY_ORDER, 1, hid, tw), lambda r, j: (0, jnp.where(r * tr >= n, 1, 0), 0, j)),
                  pl.BlockSpec((1, tw), lambda r, j: (0, j))],
        out_specs=pl.BlockSpec((HY_ORDER, tr, tw), lambda r, j: (0, r, j)),
        out_shape=jax.ShapeDtypeStruct((HY_ORDER, n_fft, HY_WIDTH), F32),
        scratch_shapes=[pltpu.VMEM((tr, hid), F32)],
        compiler_params=_cparams(2),
        name="hyena_filter",
    )(feat, tpos, w1p, b1.reshape(1, hid), w2, b2.reshape(1, hid), w3, b3.reshape(1, hid),
      freq.reshape(1, hid), wo, deltas)


@functools.lru_cache(maxsize=None)
def _dft_constants(n_fft, no, ni):
    jo = np.arange(no)
    k1 = np.arange(no)
    half = no // 2
    f1r, f1d, f1i = [], [], []
    for i in range(ni):
        ang = 2.0 * np.pi * np.outer(k1, jo * ni + i) / n_fft
        c, s = np.cos(ang), np.sin(ang)
        f1r.append(np.concatenate([c, -s], axis=0))
        ch, sh = c[:, :half], s[:, :half]
        f1d.append(np.block([[ch, sh], [-sh, ch]]))
        angi = 2.0 * np.pi * np.outer(np.arange(half) * ni + i, k1) / n_fft
        ci, si = np.cos(angi), np.sin(angi)
        f1i.append(np.block([[ci, -si], [si, ci]]))
    ang3 = 2.0 * np.pi * np.outer(np.arange(ni), np.arange(ni)) / ni
    c3, s3 = np.cos(ang3), np.sin(ang3)
    m3 = np.block([[c3, s3], [-s3, c3]])
    m3c = np.block([[c3, -s3], [s3, c3]])
    as_bf16 = lambda a: np.asarray(a, np.float32).astype(BF16)
    return (as_bf16(np.stack(f1r)), as_bf16(np.stack(f1d)), as_bf16(np.stack(f1i)),
            as_bf16(m3), as_bf16(m3c))


def _const_spec(shape, n_axes):
    zeros = (0,) * len(shape)
    if n_axes == 2:
        index = lambda a, b: zeros
    else:
        index = lambda a: zeros
    return pl.BlockSpec(shape, index, pipeline_mode=pl.Buffered(1))


DFT_UNROLL = 32


def _pitch(size):
    return size + SUBLANES


def _block_at(i, size):
    return pl.ds(pl.multiple_of(i * _pitch(size), SUBLANES), size)


def _dft_stage3_rhs(a_ref, k1, no, ni):
    re = a_ref[pl.ds(k1, ni, stride=_pitch(2 * no)), :]
    im = a_ref[pl.ds(no + k1, ni, stride=_pitch(2 * no)), :]
    return jnp.concatenate([re, im], axis=0).astype(BF16)


def _dft_stage3_rhs_pair(a_ref, pair, no, ni):
    return jnp.concatenate([_dft_stage3_rhs(a_ref, 2 * pair, no, ni),
                            _dft_stage3_rhs(a_ref, 2 * pair + 1, no, ni)], axis=1)


def _spectrum_kernel(h_ref, f1r_ref, m3_ref, o_ref, a_ref, *, no, ni):
    inv_n = 1.0 / (no * ni)
    tc = h_ref.shape[-1]

    def stage1(ji, carry):
        rhs = h_ref[0, pl.ds(ji, no, stride=ni), :].astype(BF16)
        a_ref[_block_at(ji, 2 * no), :] = jnp.dot(f1r_ref[ji], rhs, preferred_element_type=F32)
        return carry

    lax.fori_loop(0, ni, stage1, 0, unroll=DFT_UNROLL)

    def stage3(pair, carry):
        x = jnp.dot(m3_ref[...], _dft_stage3_rhs_pair(a_ref, pair, no, ni),
                    preferred_element_type=F32) * inv_n
        r0 = pl.multiple_of(pair * 2 * ni, 2 * ni)
        for i in range(2):
            lanes = slice(i * tc, (i + 1) * tc)
            o_ref[0, 0, pl.ds(r0 + i * ni, ni), :] = x[:ni, lanes].astype(o_ref.dtype)
            o_ref[0, 1, pl.ds(r0 + i * ni, ni), :] = x[ni:, lanes].astype(o_ref.dtype)
        return carry

    lax.fori_loop(0, no // 2, stage3, 0, unroll=DFT_UNROLL // 2)


def _filter_spectrum(hfull, no=FFT_NO, ni=FFT_NI, tc=HY_TC):
    n_ord, n_fft, width = hfull.shape
    f1r, _, _, m3, _ = _dft_constants(n_fft, no, ni)
    return pl.pallas_call(
        functools.partial(_spectrum_kernel, no=no, ni=ni),
        grid=(n_ord, width // tc),
        in_specs=[pl.BlockSpec((1, n_fft, tc), lambda o, c: (o, 0, c)),
                  _const_spec(f1r.shape, 2), _const_spec(m3.shape, 2)],
        out_specs=pl.BlockSpec((1, 2, n_fft, tc), lambda o, c: (o, 0, 0, c)),
        out_shape=jax.ShapeDtypeStruct((n_ord, 2, n_fft, width), BF16),
        scratch_shapes=[pltpu.VMEM((ni * _pitch(2 * no), tc), F32)],
        compiler_params=_cparams(2),
        name="hyena_spectrum",
    )(hfull, f1r, m3)


def _long_conv_kernel(v_ref, g_ref, spec_ref, bias_ref, f1d_ref, f1i_ref, m3_ref, m3c_ref,
                      o_ref, z_ref, a_ref, b_ref, *, n, no, ni):
    order = pl.program_id(1)
    half = no // 2
    assert n == half * ni

    @pl.when(order == 0)
    def _():
        def load(jo, carry):
            r0 = pl.multiple_of(jo * ni, ni)
            for b in range(2):
                z_ref[b, _block_at(jo, ni), :] = v_ref[pl.ds(b * n + r0, ni), :].astype(F32)
            return carry

        lax.fori_loop(0, half, load, 0, unroll=DFT_UNROLL)

    def stage1(ji, carry):
        zr = z_ref[0, pl.ds(ji, half, stride=_pitch(ni)), :]
        zi = z_ref[1, pl.ds(ji, half, stride=_pitch(ni)), :]
        rhs = jnp.concatenate([zr, zi], axis=0).astype(BF16)
        a_ref[_block_at(ji, 2 * no), :] = jnp.dot(f1d_ref[ji], rhs, preferred_element_type=F32)
        return carry

    lax.fori_loop(0, ni, stage1, 0, unroll=DFT_UNROLL)

    def stage3(pair, carry):
        x = jnp.dot(m3_ref[...], _dft_stage3_rhs_pair(a_ref, pair, no, ni), preferred_element_type=F32)
        r0 = pl.multiple_of(pair * 2 * ni, 2 * ni)
        side_by_side = lambda s: jnp.concatenate([s[:ni], s[ni:]], axis=1).astype(F32)
        hr = side_by_side(spec_ref[0, 0, pl.ds(r0, 2 * ni), :])
        hi = side_by_side(spec_ref[0, 1, pl.ds(r0, 2 * ni), :])
        xr, xi = x[:ni], x[ni:]
        y = jnp.concatenate([xr * hr - xi * hi, xr * hi + xi * hr], axis=0).astype(BF16)
        b = jnp.dot(m3c_ref[...], y, preferred_element_type=F32)
        tc = b.shape[1] // 2
        b_ref[_block_at(2 * pair, 2 * ni), :] = b[:, :tc]
        b_ref[_block_at(2 * pair + 1, 2 * ni), :] = b[:, tc:]
        return carry

    lax.fori_loop(0, no // 2, stage3, 0, unroll=DFT_UNROLL // 2)

    def stage1_inv(t2, carry):
        br = b_ref[pl.ds(t2, no, stride=_pitch(2 * ni)), :]
        bi = b_ref[pl.ds(ni + t2, no, stride=_pitch(2 * ni)), :]
        rhs = jnp.concatenate([br, bi], axis=0).astype(BF16)
        a_ref[pl.ds(pl.multiple_of(t2 * _pitch(2 * no), SUBLANES), no), :] = jnp.dot(
            f1i_ref[t2], rhs, preferred_element_type=F32)
        return carry

    lax.fori_loop(0, ni, stage1_inv, 0, unroll=DFT_UNROLL)

    bias = bias_ref[0, 0]

    def finish(t1, carry):
        r0 = pl.multiple_of(t1 * ni, ni)
        conv = (a_ref[pl.ds(t1, ni, stride=_pitch(2 * no)), :],
                a_ref[pl.ds(half + t1, ni, stride=_pitch(2 * no)), :])
        for b in range(2):
            z = z_ref[b, _block_at(t1, ni), :]
            gate = g_ref[pl.ds(b * n + r0, ni), :].astype(F32)
            zn = gate * (conv[b] + z * bias)
            z_ref[b, _block_at(t1, ni), :] = zn
            o_ref[pl.ds(b * n + r0, ni), :] = zn.astype(o_ref.dtype)
        return carry

    lax.fori_loop(0, half, finish, 0, unroll=DFT_UNROLL)


def _long_conv(uc, spec, hy_bias, layer, n, no=FFT_NO, ni=FFT_NI, tc=HY_TC):
    n_fft = no * ni
    assert 2 * n == n_fft
    _, f1d, f1i, m3, m3c = _dft_constants(n_fft, no, ni)
    nct = HY_WIDTH // tc
    bias = hy_bias.reshape(hy_bias.shape[0], HY_ORDER, 1, HY_WIDTH)
    return pl.pallas_call(
        functools.partial(_long_conv_kernel, n=n, no=no, ni=ni),
        grid=(nct, HY_ORDER),
        in_specs=[pl.BlockSpec((2 * n, tc), lambda c, o: (0, c)),
                  pl.BlockSpec((2 * n, tc), lambda c, o: (0, (1 + o) * nct + c)),
                  pl.BlockSpec((1, 2, n_fft, tc), lambda c, o: (o, 0, 0, c)),
                  pl.BlockSpec((1, 1, 1, tc), lambda c, o: (layer, o, 0, c)),
                  _const_spec(f1d.shape, 2), _const_spec(f1i.shape, 2),
                  _const_spec(m3.shape, 2), _const_spec(m3c.shape, 2)],
        out_specs=pl.BlockSpec((2 * n, tc), lambda c, o: (0, c)),
        out_shape=jax.ShapeDtypeStruct((2 * n, HY_WIDTH), BF16),
        scratch_shapes=[pltpu.VMEM((2, (no // 2) * _pitch(ni), tc), F32),
                        pltpu.VMEM((ni * _pitch(2 * no), tc), F32),
                        pltpu.VMEM((no * _pitch(2 * ni), tc), F32)],
        compiler_params=_cparams(2),
        name="hyena_long_conv",
    )(uc, uc, spec, bias, f1d, f1i, m3, m3c)


@functools.lru_cache(maxsize=None)
def _small_dft_constants(n):
    n_fft = 2 * n
    k = np.arange(n_fft)
    ang = 2.0 * np.pi * np.outer(k, np.arange(n_fft)) / n_fft
    c, s = np.cos(ang), np.sin(ang)
    fr = np.concatenate([c, -s], axis=0)
    ch, sh = c[:, :n], s[:, :n]
    fd = np.block([[ch, sh], [-sh, ch]])
    ci, si = c[:n, :], s[:n, :]
    fi = np.block([[ci, -si], [si, ci]])
    as_bf16 = lambda a: np.asarray(a, np.float32).astype(BF16)
    return as_bf16(fr), as_bf16(fd), as_bf16(fi)


def _small_conv_kernel(v_ref, g1_ref, g2_ref, h_ref, bias_ref, fr_ref, fd_ref, fi_ref, o_ref, *, n):
    n_fft = 2 * n
    z = v_ref[...].astype(F32)
    for o, g_ref in enumerate((g1_ref, g2_ref)):
        hs = jnp.dot(fr_ref[...], h_ref[o].astype(BF16), preferred_element_type=F32) * (1.0 / n_fft)
        x = jnp.dot(fd_ref[...], z.astype(BF16), preferred_element_type=F32)
        xr, xi, hr, hi = x[:n_fft], x[n_fft:], hs[:n_fft], hs[n_fft:]
        y = jnp.concatenate([xr * hr - xi * hi, xr * hi + xi * hr], axis=0).astype(BF16)
        conv = jnp.dot(fi_ref[...], y, preferred_element_type=F32)
        z = g_ref[...].astype(F32) * (conv + z * bias_ref[0, o])
    o_ref[...] = z.astype(o_ref.dtype)


def _small_conv(uc, hfull, hy_bias, layer, n, tc=256):
    fr, fd, fi = _small_dft_constants(n)
    nct = HY_WIDTH // tc
    bias = hy_bias.reshape(hy_bias.shape[0], HY_ORDER, 1, HY_WIDTH)
    col = lambda k: pl.BlockSpec((2 * n, tc), lambda c: (0, k * nct + c))
    return pl.pallas_call(
        functools.partial(_small_conv_kernel, n=n),
        grid=(nct,),
        in_specs=[col(0), col(1), col(2),
                  pl.BlockSpec((HY_ORDER, 2 * n, tc), lambda c: (0, 0, c)),
                  pl.BlockSpec((1, HY_ORDER, 1, tc), lambda c: (layer, 0, 0, c)),
                  _const_spec(fr.shape, 1), _const_spec(fd.shape, 1), _const_spec(fi.shape, 1)],
        out_specs=pl.BlockSpec((2 * n, tc), lambda c: (0, c)),
        out_shape=jax.ShapeDtypeStruct((2 * n, HY_WIDTH), BF16),
        compiler_params=_cparams(1),
        name="hyena_small_conv",
    )(uc, uc, uc, hfull, bias, fr, fd, fi)


def _hyena(u, conv_w, conv_b, filt, hy_bias, layer, n_seq, seg, row_blk_off):
    assert n_seq == 2
    uc = _short_conv(u, conv_w, conv_b, layer, n_seq, seg, row_blk_off)
    hfull = _hyena_filter(seg, 2 * seg, *filt)
    if 2 * seg == FFT_NO * FFT_NI:
        return _long_conv(uc, _filter_spectrum(hfull), hy_bias, layer, seg)
    return _small_conv(uc, hfull, hy_bias, layer, seg)


def _moe(h2, logits, w1, w3, w2):
    t, d = h2.shape
    dff = w1.shape[-1]
    top_logit, top_idx = lax.top_k(logits, TOP_K)
    gate = jax.nn.softmax(top_logit, axis=-1)
    n_assign = t * TOP_K
    flat_e = top_idx.reshape(-1)
    flat_tok = jnp.repeat(jnp.arange(t, dtype=jnp.int32), TOP_K)
    onehot = (flat_e[:, None] == jnp.arange(N_EXPERTS, dtype=flat_e.dtype)[None, :]).astype(jnp.int32)
    counts = jnp.sum(onehot, axis=0)
    rank = jnp.sum((jnp.cumsum(onehot, axis=0) - onehot) * onehot, axis=1)
    padded = (counts + MOE_ROWS - 1) // MOE_ROWS * MOE_ROWS
    pad_end = jnp.cumsum(padded)
    pad_start = pad_end - padded
    dest = (jnp.sum(onehot * pad_start[None, :], axis=1) + rank).astype(jnp.int32)
    n_blocks = n_assign // MOE_ROWS + N_EXPERTS
    n_slots = n_blocks * MOE_ROWS
    slot_tok = jnp.full((n_slots,), t, jnp.int32).at[dest].set(flat_tok)
    block_start = jnp.arange(n_blocks, dtype=pad_end.dtype) * MOE_ROWS
    block_expert = jnp.minimum(jnp.sum((pad_end[None, :] <= block_start[:, None]).astype(jnp.int32), axis=1),
                               N_EXPERTS - 1).astype(jnp.int32)
    n_used = (pad_end[-1] // MOE_ROWS).astype(jnp.int32).reshape(1)
    h_pad = jnp.concatenate([h2, jnp.zeros((1, d), h2.dtype)], axis=0)
    xs = h_pad[slot_tok]
    act = _gmm(xs, (w1, w3), block_expert, n_used, k=d, n=dff, tm=MOE_ROWS, tn=512,
               out_dtype=BF16, name="moe_up")
    ys = _gmm(act, (w2,), block_expert, n_used, k=dff, n=d, tm=MOE_ROWS, tn=1024,
              out_dtype=BF16, name="moe_down")
    pos = dest.reshape(t, TOP_K)
    return ys[pos[:, 0]], ys[pos[:, 1]], gate


def kernel(x, c, ctx, c_ctx, w_ada, b_ada, norm_g, w_in, attn_sink, conv_w, conv_b,
           filt_w1, filt_b1, filt_w2, filt_b2, filt_w3, filt_b3, filt_freq, filt_w_out, hyena_bias,
           w_attn_out, w_hyena_out, w_out, ffn_w1, ffn_w3, ffn_w2,
           moe_router, moe_w1, moe_w3, moe_w2):
    batch, seq, d = x.shape
    n_ctx = ctx.shape[1]
    depth = w_in.shape[0]
    in_width = w_in.shape[-1]
    n_lat = batch * seq
    n_all = n_lat + batch * n_ctx
    n_fft = 2 * seq
    ga_off = HY_OFF + (HY_ORDER + 1) * HY_WIDTH
    gh_off = ga_off + d
    assert batch == 2 and FFT_NO * FFT_NI == n_fft

    x_all = jnp.concatenate([x.reshape(n_lat, d), ctx.reshape(batch * n_ctx, d)], axis=0)
    cond = jnp.concatenate([c, c_ctx[None]], axis=0)
    mod = _ada(cond, w_ada, b_ada)
    mod = mod.reshape(depth, batch + 1, 6, 1, d)
    tm_all = n_all // 8
    tm_lat = n_lat // 8

    def mods(l, j):
        return mod[l, :, j]

    h = _norm_mod(x_all, norm_g[0, 0], mods(0, 0), mods(0, 1), seq)
    for l in range(depth):
        last = l == depth - 1
        rows = n_lat if last else n_all
        tm = tm_lat if last else tm_all
        filt = (filt_w1[l], filt_b1[l], filt_w2[l], filt_b2[l], filt_w3[l], filt_b3[l],
                filt_freq[l], filt_w_out[l])

        u = _in_proj(h, w_in, l, _rope_tables_for_rows(seq, batch, rows - n_lat), rows, tm)
        if last:
            kv_ctx = _gmm(h[n_lat:], (w_in,), *_dense_ids(batch * n_ctx, batch * n_ctx, l), k=d,
                          n=HY_OFF - K_OFF, tm=batch * n_ctx, tn=512, out_dtype=BF16, w_col_off=K_OFF,
                          name="ctx_kv_proj")
            att = _win_attn(u, kv_ctx, 0, 0, attn_sink[l], batch, seq, n_ctx)
        else:
            att = _win_attn(u, u, n_lat // n_ctx, K_OFF, attn_sink[l], batch, seq, n_ctx)
        hy = _hyena(u, conv_w, conv_b, filt, hyena_bias, l, batch, seq, 0)
        if not last:
            att = jnp.concatenate([att, _ctx_attn(u, attn_sink[l], batch, seq, n_ctx)], axis=0)
            hy_c = _hyena(u, conv_w, conv_b, filt, hyena_bias, l, batch, n_ctx, n_lat // n_ctx)
            hy = jnp.concatenate([hy, hy_c], axis=0)
        mrg = _merge(att, hy, u, w_attn_out, w_hyena_out, l, rows, ga_off, gh_off, tm)
        y = _gmm(mrg, (w_out,), *_dense_ids(rows, tm, l), k=d, n=d, tm=tm, tn=512,
                 out_dtype=F32, name="out_proj")
        router = None
        if l % 2 == 1:
            router = jnp.pad(moe_router[l // 2], ((0, 0), (0, LANES - N_EXPERTS)))
        res = _post(x_all, y, mods(l, 2), norm_g[l, 1], seq, rows,
                    nxt=(norm_g[l, 2], mods(l, 3), mods(l, 4)), router=router)
        x_all, h2 = res[0], res[1]

        if l % 2 == 0:
            i = l // 2
            dff = ffn_w1.shape[-1]
            act = _gmm(h2, (ffn_w1, ffn_w3), *_dense_ids(rows, tm, i), k=d, n=dff, tm=tm, tn=256,
                       out_dtype=BF16, name="ffn_up")
            f = _gmm(act, (ffn_w2,), *_dense_ids(rows, tm // 4, i), k=dff, n=d, tm=tm // 4, tn=512,
                     out_dtype=F32, w_single_buffer=True, name="ffn_down")
        else:
            i = l // 2
            f = _moe(h2, res[2][:, :N_EXPERTS], moe_w1[i], moe_w3[i], moe_w2[i])
        if last:
            (x_all,) = _post(x_all, f, mods(l, 5), norm_g[l, 3], seq, rows)
        else:
            x_all, h = _post(x_all, f, mods(l, 5), norm_g[l, 3], seq, rows,
                             nxt=(norm_g[l + 1, 0], mods(l + 1, 0), mods(l + 1, 1)))
    return x_all[:n_lat].reshape(batch, seq, d)
```

```python
import functools
import math

import numpy as np
import jax
import jax.numpy as jnp
from jax import lax
from jax.experimental import pallas as pl
from jax.experimental.pallas import tpu as pltpu

F32 = jnp.float32
BF16 = jnp.bfloat16

GRID_W = 64
N_Q_HEADS = 16
N_KV_HEADS = 4
HEAD_DIM = 128
Q_GROUP = N_Q_HEADS // N_KV_HEADS
ATT_WIDTH = N_Q_HEADS * HEAD_DIM
KV_WIDTH = N_KV_HEADS * HEAD_DIM
BLOCK = 128
ROPE_BASE = 10000.0
ROPE_FREQS = HEAD_DIM // 4
MASK_VALUE = -1e30
HY_WIDTH = 2048
HY_ORDER = 2
SHORT_CONV = 3
FILTER_BANDS = 16
FILTER_HIDDEN = 64
DECAY_TARGET = 1e-2
MIN_DECAY = math.log(DECAY_TARGET) / 1.5
MAX_DECAY = math.log(DECAY_TARGET) / 0.3
K_OFF = ATT_WIDTH
V_OFF = K_OFF + KV_WIDTH
HY_OFF = V_OFF + KV_WIDTH
N_EXPERTS = 8
TOP_K = 2
MOE_ROWS = 512
RMS_EPS = 1e-6

LANES = 128
SUBLANES = 8
VMEM_LIMIT_BYTES = 56 * 1024 * 1024

FFT_NO = 128
FFT_NI = 64
HY_TC = 128


def _cparams(n_axes):
    return pltpu.CompilerParams(dimension_semantics=("arbitrary",) * n_axes,
                                vmem_limit_bytes=VMEM_LIMIT_BYTES)


ADA_CHUNK = 64


def _ada_kernel(c_ref, w_ref, b_ref, o_ref, act_ref):
    n_rows, k, _ = c_ref.shape
    tn = w_ref.shape[-1]

    @pl.when(jnp.logical_and(pl.program_id(0) == 0, pl.program_id(1) == 0))
    def _():
        c = c_ref[...]
        act_ref[...] = c * jax.nn.sigmoid(c)

    def body(i, accs):
        r0 = pl.multiple_of(i * ADA_CHUNK, ADA_CHUNK)
        w = w_ref[0, pl.ds(r0, ADA_CHUNK), :]
        out = []
        for r in range(n_rows):
            a = act_ref[r, pl.ds(r0, ADA_CHUNK), :]
            acc = accs[r]
            for s in range(ADA_CHUNK // SUBLANES):
                rows = slice(s * SUBLANES, (s + 1) * SUBLANES)
                acc = acc + w[rows] * jnp.concatenate([a[rows]] * (tn // LANES), axis=1)
            out.append(acc)
        return tuple(out)

    zero = jnp.zeros((SUBLANES, tn), F32)
    accs = lax.fori_loop(0, k // ADA_CHUNK, body, (zero,) * n_rows)
    bias = b_ref[0]
    for r in range(n_rows):
        o_ref[0, r:r + 1, :] = jnp.sum(accs[r], axis=0, keepdims=True) + bias


def _ada(cond, w_ada, b_ada, tn=512):
    n_layers, d, n = w_ada.shape
    rows = cond.shape[0]
    cb = jnp.broadcast_to(cond[:, :, None], (rows, d, LANES))
    return pl.pallas_call(
        _ada_kernel,
        grid=(n_layers, n // tn),
        in_specs=[pl.BlockSpec((rows, d, LANES), lambda l, j: (0, 0, 0)),
                  pl.BlockSpec((1, d, tn), lambda l, j: (l, 0, j)),
                  pl.BlockSpec((1, 1, tn), lambda l, j: (l, 0, j))],
        out_specs=pl.BlockSpec((1, rows, tn), lambda l, j: (l, 0, j)),
        out_shape=jax.ShapeDtypeStruct((n_layers, rows, n), F32),
        scratch_shapes=[pltpu.VMEM((rows, d, LANES), F32)],
        compiler_params=_cparams(2),
        name="ada_mod",
    )(cb, w_ada, b_ada.reshape(n_layers, 1, n))


def _rms(x):
    return x * lax.rsqrt(jnp.mean(x * x, axis=-1, keepdims=True) + RMS_EPS)


def _group_of_tile(tr, seq):
    return lambda i: (jnp.minimum((i * tr) // seq, 2), 0, 0)


def _two_source_specs(x_lat, x_ctx, tr):
    d = x_lat.shape[1]
    nl = x_lat.shape[0] // tr
    return [pl.BlockSpec((tr, d), lambda i: (jnp.minimum(i, nl - 1), 0)),
            pl.BlockSpec((tr, d), lambda i: (jnp.maximum(i - nl, 0), 0))], nl


def _two_source_rows(lat_ref, ctx_ref, n_lat_tiles):
    return jnp.where(pl.program_id(0) < n_lat_tiles, lat_ref[...], ctx_ref[...])


def _norm_mod2_kernel(lat_ref, ctx_ref, g_ref, sh_ref, sc_ref, o_ref, *, n_lat_tiles):
    y = _rms(_two_source_rows(lat_ref, ctx_ref, n_lat_tiles)) * g_ref[...]
    o_ref[...] = (y * (1.0 + sc_ref[0]) + sh_ref[0]).astype(o_ref.dtype)


def _norm_mod(x_lat, x_ctx, g, sh, sc, seq, tr=256):
    d = x_lat.shape[1]
    t = x_lat.shape[0] + x_ctx.shape[0]
    grp = _group_of_tile(tr, seq)
    x_specs, nl = _two_source_specs(x_lat, x_ctx, tr)
    return pl.pallas_call(
        functools.partial(_norm_mod2_kernel, n_lat_tiles=nl),
        grid=(t // tr,),
        in_specs=x_specs + [pl.BlockSpec((1, d), lambda i: (0, 0)),
                            pl.BlockSpec((1, 1, d), grp),
                            pl.BlockSpec((1, 1, d), grp)],
        out_specs=pl.BlockSpec((tr, d), lambda i: (i, 0)),
        out_shape=jax.ShapeDtypeStruct((t, d), BF16),
        compiler_params=_cparams(1),
        name="norm_mod",
    )(x_lat, x_ctx, g.reshape(1, d), sh, sc)


def _post_kernel(x_ref, *rest, with_next, with_router, with_pair, n_lat_tiles):
    if n_lat_tiles is None:
        x = x_ref[...]
    else:
        ctx_ref, *rest = rest
        x = _two_source_rows(x_ref, ctx_ref, n_lat_tiles)
    if with_pair:
        y0_ref, y1_ref, w_ref, gt_ref, gpost_ref, *rest = rest
        w = w_ref[...]
        y = w[:, 0:1] * y0_ref[...].astype(F32) + w[:, 1:2] * y1_ref[...].astype(F32)
    else:
        y_ref, gt_ref, gpost_ref, *rest = rest
        y = y_ref[...].astype(F32)
    xn = x + gt_ref[0] * (_rms(y) * gpost_ref[...])
    if not with_next:
        (xo_ref,) = rest
        xo_ref[...] = xn
        return
    if with_router:
        gpre_ref, sh_ref, sc_ref, wr_ref, xo_ref, ho_ref, lg_ref = rest
    else:
        gpre_ref, sh_ref, sc_ref, xo_ref, ho_ref = rest
    xo_ref[...] = xn
    h = (_rms(xn) * gpre_ref[...]) * (1.0 + sc_ref[0]) + sh_ref[0]
    ho_ref[...] = h.astype(ho_ref.dtype)
    if with_router:
        lg_ref[...] = jnp.dot(h, wr_ref[...], preferred_element_type=F32,
                              precision=lax.Precision.HIGHEST)


def _post(x_all, y, gt, g_post, seq, rows, nxt=None, router=None, tr=256):
    grp = _group_of_tile(tr, seq)
    if isinstance(x_all, tuple):
        d = x_all[0].shape[1]
        x_specs, n_lat_tiles = _two_source_specs(*x_all, tr)
        x_args = list(x_all)
    else:
        d = x_all.shape[1]
        x_specs, n_lat_tiles = [pl.BlockSpec((tr, d), lambda i: (i, 0))], None
        x_args = [x_all]
    row = pl.BlockSpec((tr, d), lambda i: (i, 0))
    vec = pl.BlockSpec((1, d), lambda i: (0, 0))
    mod = pl.BlockSpec((1, 1, d), grp)
    with_pair = isinstance(y, tuple)
    if with_pair:
        y0, y1, w = y
        in_specs = x_specs + [row, row, pl.BlockSpec((tr, w.shape[1]), lambda i: (i, 0)), mod, vec]
        args = x_args + [y0, y1, w, gt, g_post.reshape(1, d)]
    else:
        in_specs = x_specs + [row, mod, vec]
        args = x_args + [y, gt, g_post.reshape(1, d)]
    out_specs = [row]
    out_shape = [jax.ShapeDtypeStruct((rows, d), F32)]
    if nxt is not None:
        g_pre, sh, sc = nxt
        in_specs += [vec, mod, mod]
        args += [g_pre.reshape(1, d), sh, sc]
        out_specs.append(row)
        out_shape.append(jax.ShapeDtypeStruct((rows, d), BF16))
        if router is not None:
            in_specs.append(pl.BlockSpec((d, LANES), lambda i: (0, 0)))
            args.append(router)
            out_specs.append(pl.BlockSpec((tr, LANES), lambda i: (i, 0)))
            out_shape.append(jax.ShapeDtypeStruct((rows, LANES), F32))
    return pl.pallas_call(
        functools.partial(_post_kernel, with_next=nxt is not None, with_router=router is not None,
                          with_pair=with_pair, n_lat_tiles=n_lat_tiles),
        grid=(rows // tr,),
        in_specs=in_specs,
        out_specs=out_specs,
        out_shape=out_shape,
        compiler_params=_cparams(1),
        name="post_norm",
    )(*args)


def _weight_changed(be_ref, m):
    return jnp.logical_or(m == 0, be_ref[m] != be_ref[jnp.maximum(m - 1, 0)])


def _gmm_kernel(be_ref, nv_ref, a_ref, w_ref, o_ref, wb_ref):
    m = pl.program_id(1)

    @pl.when(m < nv_ref[0])
    def _():
        @pl.when(_weight_changed(be_ref, m))
        def _():
            wb_ref[...] = w_ref[0].astype(BF16)

        o_ref[...] = jnp.dot(a_ref[...], wb_ref[...],
                             preferred_element_type=F32).astype(o_ref.dtype)


def _swiglu_kernel(be_ref, nv_ref, a_ref, w1_ref, w3_ref, o_ref, w1b_ref, w3b_ref):
    m = pl.program_id(1)

    @pl.when(m < nv_ref[0])
    def _():
        @pl.when(_weight_changed(be_ref, m))
        def _():
            w1b_ref[...] = w1_ref[0].astype(BF16)
            w3b_ref[...] = w3_ref[0].astype(BF16)

        a = a_ref[...]
        g = jnp.dot(a, w1b_ref[...], preferred_element_type=F32)
        u = jnp.dot(a, w3b_ref[...], preferred_element_type=F32)
        o_ref[...] = (g * jax.nn.sigmoid(g) * u).astype(o_ref.dtype)


def _gmm(a, ws, be, nvalid, *, k, n, tm, tn, out_dtype, w_col_off=0, rows=None, w_single_buffer=False,
         name="gmm"):
    rows = a.shape[0] if rows is None else rows
    assert rows % tm == 0 and n % tn == 0 and w_col_off % tn == 0
    off = w_col_off // tn
    w_mode = dict(pipeline_mode=pl.Buffered(1)) if w_single_buffer else {}

    def a_map(j, m, be_ref, nv_ref):
        return (jnp.minimum(m, nv_ref[0] - 1), 0)

    def w_map(j, m, be_ref, nv_ref):
        return (be_ref[jnp.minimum(m, nv_ref[0] - 1)], 0, j + off)

    def o_map(j, m, be_ref, nv_ref):
        return (m, j)

    kernel = _gmm_kernel if len(ws) == 1 else _swiglu_kernel
    return pl.pallas_call(
        kernel,
        grid_spec=pltpu.PrefetchScalarGridSpec(
            num_scalar_prefetch=2,
            grid=(n // tn, rows // tm),
            in_specs=[pl.BlockSpec((tm, k), a_map)] + [pl.BlockSpec((1, k, tn), w_map, **w_mode)] * len(ws),
            out_specs=pl.BlockSpec((tm, tn), o_map),
            scratch_shapes=[pltpu.VMEM((k, tn), BF16)] * len(ws)),
        out_shape=jax.ShapeDtypeStruct((rows, n), out_dtype),
        compiler_params=_cparams(2),
        name=name,
    )(be, nvalid, a, *ws)


def _dense_ids(rows, tm, idx):
    nb = rows // tm
    return jnp.full((nb,), idx, jnp.int32), jnp.full((1,), nb, jnp.int32)


def _merge_kernel(att_ref, hy_ref, ga_ref, gh_ref, wa_ref, wh_ref, o_ref, wab_ref, whb_ref):
    @pl.when(pl.program_id(1) == 0)
    def _():
        wab_ref[...] = wa_ref[0].astype(BF16)
        whb_ref[...] = wh_ref[0].astype(BF16)

    pa = jnp.dot(att_ref[...], wab_ref[...], preferred_element_type=F32)
    ph = jnp.dot(hy_ref[...], whb_ref[...], preferred_element_type=F32)
    ga = jax.nn.sigmoid(ga_ref[...].astype(F32))
    gh = jax.nn.sigmoid(gh_ref[...].astype(F32))
    o_ref[...] = (ga * pa + gh * ph).astype(o_ref.dtype)


def _merge(att, hy, u, w_ao, w_ho, layer, rows, ga_off, gh_off, tm, tn=512):
    ka, d = w_ao.shape[1:]
    kh = w_ho.shape[1]
    return pl.pallas_call(
        _merge_kernel,
        grid=(d // tn, rows // tm),
        in_specs=[pl.BlockSpec((tm, ka), lambda j, m: (m, 0)),
                  pl.BlockSpec((tm, kh), lambda j, m: (m, 0)),
                  pl.BlockSpec((tm, tn), lambda j, m: (m, ga_off // tn + j)),
                  pl.BlockSpec((tm, tn), lambda j, m: (m, gh_off // tn + j)),
                  pl.BlockSpec((1, ka, tn), lambda j, m: (layer, 0, j)),
                  pl.BlockSpec((1, kh, tn), lambda j, m: (layer, 0, j))],
        out_specs=pl.BlockSpec((tm, tn), lambda j, m: (m, j)),
        out_shape=jax.ShapeDtypeStruct((rows, d), BF16),
        scratch_shapes=[pltpu.VMEM((ka, tn), BF16), pltpu.VMEM((kh, tn), BF16)],
        compiler_params=_cparams(2),
        name="gated_merge",
    )(att, hy, u, u, w_ao, w_ho)


def _rope_tables(seq):
    rows = seq // GRID_W
    row = jnp.repeat(jnp.arange(rows), GRID_W).astype(F32)
    col = jnp.tile(jnp.arange(GRID_W), rows).astype(F32)
    inv = ROPE_BASE ** (-jnp.arange(ROPE_FREQS, dtype=F32) / ROPE_FREQS)
    ang = jnp.stack([row[:, None] * inv, col[:, None] * inv], axis=1)
    cos, sin = jnp.cos(ang), jnp.sin(ang)
    zero = jnp.zeros_like(sin)
    cos_t = jnp.stack([cos, cos], axis=2).reshape(seq, HEAD_DIM)
    s_lo = jnp.stack([-sin, zero], axis=2).reshape(seq, HEAD_DIM)
    s_hi = jnp.stack([zero, sin], axis=2).reshape(seq, HEAD_DIM)
    return cos_t, s_lo, s_hi


def _rope_tables_for_rows(seq, batch, n_extra):
    cos_t, s_lo, s_hi = (jnp.tile(t, (batch, 1)) for t in _rope_tables(seq))
    if n_extra:
        cos_t = jnp.concatenate([cos_t, jnp.ones((n_extra, HEAD_DIM), F32)], axis=0)
        pad = jnp.zeros((n_extra, HEAD_DIM), F32)
        s_lo, s_hi = jnp.concatenate([s_lo, pad], axis=0), jnp.concatenate([s_hi, pad], axis=0)
    return cos_t, s_lo, s_hi


def _in_proj_kernel(a_ref, w_ref, c_ref, lo_ref, hi_ref, o_ref, wb_ref, *, n_rope_tiles):
    j = pl.program_id(0)

    @pl.when(pl.program_id(1) == 0)
    def _():
        wb_ref[...] = w_ref[0].astype(BF16)

    acc = jnp.dot(a_ref[...], wb_ref[...], preferred_element_type=F32)

    @pl.when(j < n_rope_tiles)
    def _():
        c, lo, hi = c_ref[...], lo_ref[...], hi_ref[...]
        for hd in range(acc.shape[1] // HEAD_DIM):
            sl = slice(hd * HEAD_DIM, (hd + 1) * HEAD_DIM)
            x = acc[:, sl]
            up = pltpu.roll(x, HEAD_DIM - ROPE_FREQS, 1)
            dn = pltpu.roll(x, ROPE_FREQS, 1)
            o_ref[:, sl] = (x * c + up * lo + dn * hi).astype(o_ref.dtype)

    @pl.when(j >= n_rope_tiles)
    def _():
        o_ref[...] = acc.astype(o_ref.dtype)


def _in_proj(h, w_in, layer, tables, rows, tm, tn=512):
    k, n = w_in.shape[1:]
    assert rows % tm == 0 and n % tn == 0 and V_OFF % tn == 0
    n_rope_tiles = V_OFF // tn
    tab = pl.BlockSpec((tm, HEAD_DIM), lambda j, m: (jnp.where(j < n_rope_tiles, m, 0), 0))
    return pl.pallas_call(
        functools.partial(_in_proj_kernel, n_rope_tiles=n_rope_tiles),
        grid=(n // tn, rows // tm),
        in_specs=[pl.BlockSpec((tm, k), lambda j, m: (m, 0)),
                  pl.BlockSpec((1, k, tn), lambda j, m: (layer, 0, j)),
                  tab, tab, tab],
        out_specs=pl.BlockSpec((tm, tn), lambda j, m: (m, j)),
        out_shape=jax.ShapeDtypeStruct((rows, n), BF16),
        scratch_shapes=[pltpu.VMEM((k, tn), BF16)],
        compiler_params=_cparams(2),
        name="in_proj",
    )(h, w_in, *tables)


def _scores(q, k):
    return lax.dot_general(q, k, (((1,), (1,)), ((), ())), preferred_element_type=F32)


def _win_attn_kernel(sink_ref, q_ref, kp_ref, kc_ref, kn_ref, vp_ref, vc_ref, vn_ref,
                     kx_ref, vx_ref, o_ref):
    n = pl.program_id(1)
    nb = pl.num_programs(1)
    scale = HEAD_DIM ** -0.5
    n_keys = 3 * BLOCK + kx_ref.shape[0]
    qi = lax.broadcasted_iota(jnp.int32, (BLOCK, n_keys), 0)
    kj = lax.broadcasted_iota(jnp.int32, (BLOCK, n_keys), 1)
    bad_prev = jnp.logical_and(kj < BLOCK, jnp.logical_or(kj < qi, n == 0))
    bad_next = jnp.logical_and(jnp.logical_and(kj >= 2 * BLOCK, kj < 3 * BLOCK),
                               jnp.logical_or(kj - 2 * BLOCK > qi, n == nb - 1))
    valid = jnp.logical_not(jnp.logical_or(bad_prev, bad_next))
    for h in range(N_KV_HEADS):
        hs = slice(h * HEAD_DIM, (h + 1) * HEAD_DIM)
        keys = jnp.concatenate([kp_ref[:, hs], kc_ref[:, hs], kn_ref[:, hs], kx_ref[:, hs]], axis=0)
        vals = jnp.concatenate([vp_ref[:, hs], vc_ref[:, hs], vn_ref[:, hs], vx_ref[:, hs]], axis=0)
        for g in range(Q_GROUP):
            head = h * Q_GROUP + g
            sl = slice(head * HEAD_DIM, (head + 1) * HEAD_DIM)
            s = jnp.where(valid, _scores(q_ref[:, sl], keys) * scale, MASK_VALUE)
            sink = sink_ref[h, g]
            mx = jnp.maximum(jnp.max(s, axis=-1, keepdims=True), sink)
            p = jnp.exp(s - mx)
            den = jnp.sum(p, axis=-1, keepdims=True) + jnp.exp(sink - mx)
            o = jnp.dot(p.astype(BF16), vals, preferred_element_type=F32)
            o_ref[:, sl] = (o / den).astype(o_ref.dtype)


def _win_attn(u, kv_ctx, ctx_row_blk, ctx_k_col, sink, batch, seq, n_ctx, out_rows):
    nb = seq // BLOCK
    assert K_OFF % KV_WIDTH == 0 and V_OFF % KV_WIDTH == 0 and ctx_k_col % KV_WIDTH == 0
    kcol = K_OFF // KV_WIDTH
    vcol = V_OFF // KV_WIDTH
    xk = ctx_k_col // KV_WIDTH

    def blk(shift, col):
        def index(b, n):
            return (b * nb + jnp.clip(n + shift, 0, nb - 1), col)
        return pl.BlockSpec((BLOCK, KV_WIDTH), index)

    return pl.pallas_call(
        _win_attn_kernel,
        grid=(batch, nb),
        in_specs=[pl.BlockSpec(memory_space=pltpu.SMEM),
                  pl.BlockSpec((BLOCK, ATT_WIDTH), lambda b, n: (b * nb + n, 0)),
                  blk(-1, kcol), blk(0, kcol), blk(1, kcol),
                  blk(-1, vcol), blk(0, vcol), blk(1, vcol),
                  pl.BlockSpec((n_ctx, KV_WIDTH), lambda b, n: (ctx_row_blk + b, xk)),
                  pl.BlockSpec((n_ctx, KV_WIDTH), lambda b, n: (ctx_row_blk + b, xk + 1))],
        out_specs=pl.BlockSpec((BLOCK, ATT_WIDTH), lambda b, n: (b * nb + n, 0)),
        out_shape=jax.ShapeDtypeStruct((out_rows, ATT_WIDTH), BF16),
        compiler_params=_cparams(2),
        name="window_attention",
    )(sink.reshape(N_KV_HEADS, Q_GROUP), u, u, u, u, u, u, u, kv_ctx, kv_ctx)


def _ctx_attn_kernel(sink_ref, q_ref, k_ref, v_ref, att_ref, o_ref):
    del att_ref
    h = pl.program_id(1)
    scale = HEAD_DIM ** -0.5
    k, v = k_ref[...], v_ref[...]
    for g in range(Q_GROUP):
        sl = slice(g * HEAD_DIM, (g + 1) * HEAD_DIM)
        s = _scores(q_ref[:, sl], k) * scale
        sink = sink_ref[h, g]
        mx = jnp.maximum(jnp.max(s, axis=-1, keepdims=True), sink)
        p = jnp.exp(s - mx)
        den = jnp.sum(p, axis=-1, keepdims=True) + jnp.exp(sink - mx)
        o = jnp.dot(p.astype(BF16), v, preferred_element_type=F32)
        o_ref[:, sl] = (o / den).astype(o_ref.dtype)


def _ctx_attn(u, att, sink, batch, seq, n_ctx):
    hd = HEAD_DIM
    qw = Q_GROUP * hd
    ctx_blk = (batch * seq) // n_ctx
    return pl.pallas_call(
        _ctx_attn_kernel,
        grid=(batch, N_KV_HEADS),
        in_specs=[pl.BlockSpec(memory_space=pltpu.SMEM),
                  pl.BlockSpec((n_ctx, qw), lambda b, h: (ctx_blk + b, h)),
                  pl.BlockSpec((n_ctx, hd), lambda b, h: (ctx_blk + b, K_OFF // hd + h)),
                  pl.BlockSpec((n_ctx, hd), lambda b, h: (ctx_blk + b, V_OFF // hd + h)),
                  pl.BlockSpec(memory_space=pl.ANY)],
        out_specs=pl.BlockSpec((n_ctx, qw), lambda b, h: (ctx_blk + b, h)),
        out_shape=jax.ShapeDtypeStruct(att.shape, att.dtype),
        input_output_aliases={4: 0},
        compiler_params=_cparams(2),
        name="context_attention",
    )(sink.reshape(N_KV_HEADS, Q_GROUP), u, u, u, att)


def _short_conv_kernel(u_ref, w_ref, b_ref, o_ref):
    x = u_ref[...].astype(F32)
    n = x.shape[0]
    r = lax.broadcasted_iota(jnp.int32, x.shape, 0)
    prev = jnp.where(r == 0, 0.0, pltpu.roll(x, 1, 0))
    nxt = jnp.where(r == n - 1, 0.0, pltpu.roll(x, n - 1, 0))
    w = w_ref[0]
    o_ref[...] = (prev * w[0:1] + x * w[1:2] + nxt * w[2:3] + b_ref[0]).astype(o_ref.dtype)


def _short_conv(u, conv_w, conv_b, layer, n_seq, seg, row_blk_off, tw=256):
    width = conv_w.shape[-1]
    cb = conv_b.reshape(conv_b.shape[0], 1, width)
    return pl.pallas_call(
        _short_conv_kernel,
        grid=(n_seq, width // tw),
        in_specs=[pl.BlockSpec((seg, tw), lambda s, j: (row_blk_off + s, HY_OFF // tw + j)),
                  pl.BlockSpec((1, SHORT_CONV, tw), lambda s, j: (layer, 0, j)),
                  pl.BlockSpec((1, 1, tw), lambda s, j: (layer, 0, j))],
        out_specs=pl.BlockSpec((seg, tw), lambda s, j: (s, j)),
        out_shape=jax.ShapeDtypeStruct((n_seq * seg, width), BF16),
        compiler_params=_cparams(2),
        name="short_conv",
    )(u, conv_w, cb)


def _filter_positions(n, n_fft):
    t = jnp.linspace(0.0, 1.0, n, dtype=F32)[:, None]
    w = 2.0 * math.pi * jnp.arange(n, dtype=F32)[:, None] / n
    bands = jnp.linspace(1e-4, FILTER_BANDS - 1, FILTER_BANDS, dtype=F32)[None, :]
    z = jnp.concatenate([t, jnp.cos(bands * w), -jnp.sin(bands * w)], axis=-1)
    zt = jnp.concatenate([z, t], axis=-1)
    mid = jnp.zeros((n_fft - 2 * n + 1, zt.shape[1]), F32)
    full = jnp.concatenate([zt, mid, zt[1:][::-1]], axis=0)
    feat = jnp.pad(full[:, :-1], ((0, 0), (0, FILTER_HIDDEN - (zt.shape[1] - 1))))
    return feat, full[:, -1:]


def _filter_kernel(z_ref, t_ref, w1_ref, b1_ref, w2_ref, b2_ref, w3_ref, b3_ref, fr_ref,
                   wo_ref, dl_ref, o_ref, hid_ref, *, n, n_fft):
    hi = lax.Precision.HIGHEST
    tr = z_ref.shape[0]

    @pl.when(pl.program_id(1) == 0)
    def _():
        fr = fr_ref[...]
        h = jnp.sin(fr * (jnp.dot(z_ref[...], w1_ref[...], preferred_element_type=F32, precision=hi)
                          + b1_ref[...]))
        h = jnp.sin(fr * (jnp.dot(h, w2_ref[...], preferred_element_type=F32, precision=hi)
                          + b2_ref[...]))
        hid_ref[...] = jnp.sin(fr * (jnp.dot(h, w3_ref[...], preferred_element_type=F32, precision=hi)
                                     + b3_ref[...]))

    h = hid_ref[...]
    row = pl.program_id(0) * tr + lax.broadcasted_iota(jnp.int32, (tr, 1), 0)
    live = jnp.logical_or(row < n, row > n_fft - n)
    decay = jnp.where(live, jnp.exp(-t_ref[...] * dl_ref[...]), 0.0)
    for o in range(HY_ORDER):
        o_ref[o] = jnp.dot(h, wo_ref[o, 0], preferred_element_type=F32, precision=hi) * decay


def _hyena_filter(n, n_fft, w1, b1, w2, b2, w3, b3, freq, w_out, tw=512):
    tr = min(512, n)
    assert n % tr == 0 and n_fft % tr == 0
    feat, tpos = _filter_positions(n, n_fft)
    hid = FILTER_HIDDEN
    w1p = jnp.pad(w1, ((0, hid - w1.shape[0]), (0, 0)))
    wo = w_out.reshape(hid, HY_ORDER, 2, HY_WIDTH).transpose(1, 2, 0, 3)
    deltas = jnp.abs(jnp.linspace(MIN_DECAY, MAX_DECAY, HY_WIDTH, dtype=F32)).reshape(1, HY_WIDTH)
    small = lambda shape: pl.BlockSpec(shape, lambda r, j: (0,) * len(shape))
    return pl.pallas_call(
        functools.partial(_filter_kernel, n=n, n_fft=n_fft),
        grid=(n_fft // tr, HY_WIDTH // tw),
        in_specs=[pl.BlockSpec((tr, hid), lambda r, j: (r, 0)),
                  pl.BlockSpec((tr, 1), lambda r, j: (r, 0)),
                  small((hid, hid)), small((1, hid)), small((hid, hid)), small((1, hid)),
                  small((hid, hid)), small((1, hid)), small((1, hid)),
                  pl.BlockSpec((HY_ORDER, 1, hid, tw), lambda r, j: (0, jnp.where(r * tr >= n, 1, 0), 0, j)),
                  pl.BlockSpec((1, tw), lambda r, j: (0, j))],
        out_specs=pl.BlockSpec((HY_ORDER, tr, tw), lambda r, j: (0, r, j)),
        out_shape=jax.ShapeDtypeStruct((HY_ORDER, n_fft, HY_WIDTH), F32),
        scratch_shapes=[pltpu.VMEM((tr, hid), F32)],
        compiler_params=_cparams(2),
        name="hyena_filter",
    )(feat, tpos, w1p, b1.reshape(1, hid), w2, b2.reshape(1, hid), w3, b3.reshape(1, hid),
      freq.reshape(1, hid), wo, deltas)


@functools.lru_cache(maxsize=None)
def _dft_constants(n_fft, no, ni):
    jo = np.arange(no)
    k1 = np.arange(no)
    half = no // 2
    f1r, f1d, f1i = [], [], []
    for i in range(ni):
        ang = 2.0 * np.pi * np.outer(k1, jo * ni + i) / n_fft
        c, s = np.cos(ang), np.sin(ang)
        f1r.append(np.concatenate([c, -s], axis=0))
        ch, sh = c[:, :half], s[:, :half]
        f1d.append(np.block([[ch, sh], [-sh, ch]]))
        angi = 2.0 * np.pi * np.outer(np.arange(half) * ni + i, k1) / n_fft
        ci, si = np.cos(angi), np.sin(angi)
        f1i.append(np.block([[ci, -si], [si, ci]]))
    ang3 = 2.0 * np.pi * np.outer(np.arange(ni), np.arange(ni)) / ni
    c3, s3 = np.cos(ang3), np.sin(ang3)
    m3 = np.block([[c3, s3], [-s3, c3]])
    m3c = np.block([[c3, -s3], [s3, c3]])
    as_bf16 = lambda a: np.asarray(a, np.float32).astype(BF16)
    return (as_bf16(np.stack(f1r)), as_bf16(np.stack(f1d)), as_bf16(np.stack(f1i)),
            as_bf16(m3), as_bf16(m3c))


def _const_spec(shape, n_axes):
    zeros = (0,) * len(shape)
    if n_axes == 2:
        index = lambda a, b: zeros
    else:
        index = lambda a: zeros
    return pl.BlockSpec(shape, index, pipeline_mode=pl.Buffered(1))


DFT_UNROLL = 32


def _pitch(size):
    return size + SUBLANES


def _block_at(i, size):
    return pl.ds(pl.multiple_of(i * _pitch(size), SUBLANES), size)


def _dft_stage3_rhs(a_ref, k1, no, ni):
    re = a_ref[pl.ds(k1, ni, stride=_pitch(2 * no)), :]
    im = a_ref[pl.ds(no + k1, ni, stride=_pitch(2 * no)), :]
    return jnp.concatenate([re, im], axis=0).astype(BF16)


def _dft_stage3_rhs_pair(a_ref, pair, no, ni):
    return jnp.concatenate([_dft_stage3_rhs(a_ref, 2 * pair, no, ni),
                            _dft_stage3_rhs(a_ref, 2 * pair + 1, no, ni)], axis=1)


def _spectrum_kernel(h_ref, f1r_ref, m3_ref, o_ref, a_ref, *, no, ni):
    inv_n = 1.0 / (no * ni)
    tc = h_ref.shape[-1]

    def stage1(ji, carry):
        rhs = h_ref[0, pl.ds(ji, no, stride=ni), :].astype(BF16)
        a_ref[_block_at(ji, 2 * no), :] = jnp.dot(f1r_ref[ji], rhs, preferred_element_type=F32)
        return carry

    lax.fori_loop(0, ni, stage1, 0, unroll=DFT_UNROLL)

    def stage3(pair, carry):
        x = jnp.dot(m3_ref[...], _dft_stage3_rhs_pair(a_ref, pair, no, ni),
                    preferred_element_type=F32) * inv_n
        r0 = pl.multiple_of(pair * 2 * ni, 2 * ni)
        for i in range(2):
            lanes = slice(i * tc, (i + 1) * tc)
            o_ref[0, 0, pl.ds(r0 + i * ni, ni), :] = x[:ni, lanes].astype(o_ref.dtype)
            o_ref[0, 1, pl.ds(r0 + i * ni, ni), :] = x[ni:, lanes].astype(o_ref.dtype)
        return carry

    lax.fori_loop(0, no // 2, stage3, 0, unroll=DFT_UNROLL // 2)


def _filter_spectrum(hfull, no=FFT_NO, ni=FFT_NI, tc=HY_TC):
    n_ord, n_fft, width = hfull.shape
    f1r, _, _, m3, _ = _dft_constants(n_fft, no, ni)
    return pl.pallas_call(
        functools.partial(_spectrum_kernel, no=no, ni=ni),
        grid=(n_ord, width // tc),
        in_specs=[pl.BlockSpec((1, n_fft, tc), lambda o, c: (o, 0, c)),
                  _const_spec(f1r.shape, 2), _const_spec(m3.shape, 2)],
        out_specs=pl.BlockSpec((1, 2, n_fft, tc), lambda o, c: (o, 0, 0, c)),
        out_shape=jax.ShapeDtypeStruct((n_ord, 2, n_fft, width), BF16),
        scratch_shapes=[pltpu.VMEM((ni * _pitch(2 * no), tc), F32)],
        compiler_params=_cparams(2),
        name="hyena_spectrum",
    )(hfull, f1r, m3)


def _long_conv_kernel(v_ref, g_ref, spec_ref, bias_ref, f1d_ref, f1i_ref, m3_ref, m3c_ref,
                      o_ref, z_ref, a_ref, b_ref, *, n, no, ni):
    order = pl.program_id(1)
    half = no // 2
    assert n == half * ni

    @pl.when(order == 0)
    def _():
        def load(jo, carry):
            r0 = pl.multiple_of(jo * ni, ni)
            for b in range(2):
                z_ref[b, _block_at(jo, ni), :] = v_ref[pl.ds(b * n + r0, ni), :].astype(F32)
            return carry

        lax.fori_loop(0, half, load, 0, unroll=DFT_UNROLL)

    def stage1(ji, carry):
        zr = z_ref[0, pl.ds(ji, half, stride=_pitch(ni)), :]
        zi = z_ref[1, pl.ds(ji, half, stride=_pitch(ni)), :]
        rhs = jnp.concatenate([zr, zi], axis=0).astype(BF16)
        a_ref[_block_at(ji, 2 * no), :] = jnp.dot(f1d_ref[ji], rhs, preferred_element_type=F32)
        return carry

    lax.fori_loop(0, ni, stage1, 0, unroll=DFT_UNROLL)

    def stage3(pair, carry):
        x = jnp.dot(m3_ref[...], _dft_stage3_rhs_pair(a_ref, pair, no, ni), preferred_element_type=F32)
        r0 = pl.multiple_of(pair * 2 * ni, 2 * ni)
        side_by_side = lambda s: jnp.concatenate([s[:ni], s[ni:]], axis=1).astype(F32)
        hr = side_by_side(spec_ref[0, 0, pl.ds(r0, 2 * ni), :])
        hi = side_by_side(spec_ref[0, 1, pl.ds(r0, 2 * ni), :])
        xr, xi = x[:ni], x[ni:]
        y = jnp.concatenate([xr * hr - xi * hi, xr * hi + xi * hr], axis=0).astype(BF16)
        b = jnp.dot(m3c_ref[...], y, preferred_element_type=F32)
        tc = b.shape[1] // 2
        b_ref[_block_at(2 * pair, 2 * ni), :] = b[:, :tc]
        b_ref[_block_at(2 * pair + 1, 2 * ni), :] = b[:, tc:]
        return carry

    lax.fori_loop(0, no // 2, stage3, 0, unroll=DFT_UNROLL // 2)

    def stage1_inv(t2, carry):
        br = b_ref[pl.ds(t2, no, stride=_pitch(2 * ni)), :]
        bi = b_ref[pl.ds(ni + t2, no, stride=_pitch(2 * ni)), :]
        rhs = jnp.concatenate([br, bi], axis=0).astype(BF16)
        a_ref[pl.ds(pl.multiple_of(t2 * _pitch(2 * no), SUBLANES), no), :] = jnp.dot(
            f1i_ref[t2], rhs, preferred_element_type=F32)
        return carry

    lax.fori_loop(0, ni, stage1_inv, 0, unroll=DFT_UNROLL)

    bias = bias_ref[0, 0]

    def finish(t1, carry):
        r0 = pl.multiple_of(t1 * ni, ni)
        conv = (a_ref[pl.ds(t1, ni, stride=_pitch(2 * no)), :],
                a_ref[pl.ds(half + t1, ni, stride=_pitch(2 * no)), :])
        for b in range(2):
            z = z_ref[b, _block_at(t1, ni), :]
            gate = g_ref[pl.ds(b * n + r0, ni), :].astype(F32)
            zn = gate * (conv[b] + z * bias)
            z_ref[b, _block_at(t1, ni), :] = zn
            o_ref[pl.ds(b * n + r0, ni), :] = zn.astype(o_ref.dtype)
        return carry

    lax.fori_loop(0, half, finish, 0, unroll=DFT_UNROLL)


def _long_conv(uc, spec, hy_bias, layer, n, out_rows, no=FFT_NO, ni=FFT_NI, tc=HY_TC):
    n_fft = no * ni
    assert 2 * n == n_fft
    _, f1d, f1i, m3, m3c = _dft_constants(n_fft, no, ni)
    nct = HY_WIDTH // tc
    bias = hy_bias.reshape(hy_bias.shape[0], HY_ORDER, 1, HY_WIDTH)
    return pl.pallas_call(
        functools.partial(_long_conv_kernel, n=n, no=no, ni=ni),
        grid=(nct, HY_ORDER),
        in_specs=[pl.BlockSpec((2 * n, tc), lambda c, o: (0, c)),
                  pl.BlockSpec((2 * n, tc), lambda c, o: (0, (1 + o) * nct + c)),
                  pl.BlockSpec((1, 2, n_fft, tc), lambda c, o: (o, 0, 0, c)),
                  pl.BlockSpec((1, 1, 1, tc), lambda c, o: (layer, o, 0, c)),
                  _const_spec(f1d.shape, 2), _const_spec(f1i.shape, 2),
                  _const_spec(m3.shape, 2), _const_spec(m3c.shape, 2)],
        out_specs=pl.BlockSpec((2 * n, tc), lambda c, o: (0, c)),
        out_shape=jax.ShapeDtypeStruct((out_rows, HY_WIDTH), BF16),
        scratch_shapes=[pltpu.VMEM((2, (no // 2) * _pitch(ni), tc), F32),
                        pltpu.VMEM((ni * _pitch(2 * no), tc), F32),
                        pltpu.VMEM((no * _pitch(2 * ni), tc), F32)],
        compiler_params=_cparams(2),
        name="hyena_long_conv",
    )(uc, uc, spec, bias, f1d, f1i, m3, m3c)


@functools.lru_cache(maxsize=None)
def _small_dft_constants(n):
    n_fft = 2 * n
    k = np.arange(n_fft)
    ang = 2.0 * np.pi * np.outer(k, np.arange(n_fft)) / n_fft
    c, s = np.cos(ang), np.sin(ang)
    fr = np.concatenate([c, -s], axis=0)
    ch, sh = c[:, :n], s[:, :n]
    fd = np.block([[ch, sh], [-sh, ch]])
    ci, si = c[:n, :], s[:n, :]
    fi = np.block([[ci, -si], [si, ci]])
    as_bf16 = lambda a: np.asarray(a, np.float32).astype(BF16)
    return as_bf16(fr), as_bf16(fd), as_bf16(fi)


def _small_conv_kernel(v_ref, g1_ref, g2_ref, h_ref, bias_ref, fr_ref, fd_ref, fi_ref, hy_ref, o_ref,
                       *, n):
    del hy_ref
    n_fft = 2 * n
    z = v_ref[...].astype(F32)
    for o, g_ref in enumerate((g1_ref, g2_ref)):
        hs = jnp.dot(fr_ref[...], h_ref[o].astype(BF16), preferred_element_type=F32) * (1.0 / n_fft)
        x = jnp.dot(fd_ref[...], z.astype(BF16), preferred_element_type=F32)
        xr, xi, hr, hi = x[:n_fft], x[n_fft:], hs[:n_fft], hs[n_fft:]
        y = jnp.concatenate([xr * hr - xi * hi, xr * hi + xi * hr], axis=0).astype(BF16)
        conv = jnp.dot(fi_ref[...], y, preferred_element_type=F32)
        z = g_ref[...].astype(F32) * (conv + z * bias_ref[0, o])
    o_ref[...] = z.astype(o_ref.dtype)


def _small_conv(uc, hfull, hy_bias, layer, n, hy, row_blk, tc=256):
    fr, fd, fi = _small_dft_constants(n)
    nct = HY_WIDTH // tc
    bias = hy_bias.reshape(hy_bias.shape[0], HY_ORDER, 1, HY_WIDTH)
    col = lambda k: pl.BlockSpec((2 * n, tc), lambda c: (0, k * nct + c))
    return pl.pallas_call(
        functools.partial(_small_conv_kernel, n=n),
        grid=(nct,),
        in_specs=[col(0), col(1), col(2),
                  pl.BlockSpec((HY_ORDER, 2 * n, tc), lambda c: (0, 0, c)),
                  pl.BlockSpec((1, HY_ORDER, 1, tc), lambda c: (layer, 0, 0, c)),
                  _const_spec(fr.shape, 1), _const_spec(fd.shape, 1), _const_spec(fi.shape, 1),
                  pl.BlockSpec(memory_space=pl.ANY)],
        out_specs=pl.BlockSpec((2 * n, tc), lambda c: (row_blk, c)),
        out_shape=jax.ShapeDtypeStruct(hy.shape, hy.dtype),
        input_output_aliases={8: 0},
        compiler_params=_cparams(1),
        name="hyena_small_conv",
    )(uc, uc, uc, hfull, bias, fr, fd, fi, hy)


def _hyena(u, conv_w, conv_b, filt, hy_bias, layer, seg, row_blk_off, hy=None, out_rows=None):
    uc = _short_conv(u, conv_w, conv_b, layer, 2, seg, row_blk_off)
    hfull = _hyena_filter(seg, 2 * seg, *filt)
    if 2 * seg == FFT_NO * FFT_NI:
        return _long_conv(uc, _filter_spectrum(hfull), hy_bias, layer, seg, out_rows)
    assert row_blk_off % 2 == 0
    return _small_conv(uc, hfull, hy_bias, layer, seg, hy, row_blk_off // 2)


def _moe(h2, logits, w1, w3, w2):
    t, d = h2.shape
    dff = w1.shape[-1]
    top_logit, top_idx = lax.top_k(logits, TOP_K)
    gate = jax.nn.softmax(top_logit, axis=-1)
    n_assign = t * TOP_K
    flat_e = top_idx.reshape(-1)
    flat_tok = jnp.repeat(jnp.arange(t, dtype=jnp.int32), TOP_K)
    onehot = (flat_e[:, None] == jnp.arange(N_EXPERTS, dtype=flat_e.dtype)[None, :]).astype(jnp.int32)
    counts = jnp.sum(onehot, axis=0)
    rank = jnp.sum((jnp.cumsum(onehot, axis=0) - onehot) * onehot, axis=1)
    padded = (counts + MOE_ROWS - 1) // MOE_ROWS * MOE_ROWS
    pad_end = jnp.cumsum(padded)
    pad_start = pad_end - padded
    dest = (jnp.sum(onehot * pad_start[None, :], axis=1) + rank).astype(jnp.int32)
    n_blocks = n_assign // MOE_ROWS + N_EXPERTS
    n_slots = n_blocks * MOE_ROWS
    slot_tok = (jnp.arange(n_slots, dtype=jnp.int32) % t).at[dest].set(flat_tok)
    block_start = jnp.arange(n_blocks, dtype=pad_end.dtype) * MOE_ROWS
    block_expert = jnp.minimum(jnp.sum((pad_end[None, :] <= block_start[:, None]).astype(jnp.int32), axis=1),
                               N_EXPERTS - 1).astype(jnp.int32)
    n_used = (pad_end[-1] // MOE_ROWS).astype(jnp.int32).reshape(1)
    xs = h2[slot_tok]
    act = _gmm(xs, (w1, w3), block_expert, n_used, k=d, n=dff, tm=MOE_ROWS, tn=512,
               out_dtype=BF16, name="moe_up")
    ys = _gmm(act, (w2,), block_expert, n_used, k=dff, n=d, tm=MOE_ROWS, tn=1024,
              out_dtype=BF16, name="moe_down")
    pos = dest.reshape(t, TOP_K)
    return ys[pos[:, 0]], ys[pos[:, 1]], gate


def kernel(x, c, ctx, c_ctx, w_ada, b_ada, norm_g, w_in, attn_sink, conv_w, conv_b,
           filt_w1, filt_b1, filt_w2, filt_b2, filt_w3, filt_b3, filt_freq, filt_w_out, hyena_bias,
           w_attn_out, w_hyena_out, w_out, ffn_w1, ffn_w3, ffn_w2,
           moe_router, moe_w1, moe_w3, moe_w2):
    batch, seq, d = x.shape
    n_ctx = ctx.shape[1]
    depth = w_in.shape[0]
    in_width = w_in.shape[-1]
    n_lat = batch * seq
    n_all = n_lat + batch * n_ctx
    n_fft = 2 * seq
    ga_off = HY_OFF + (HY_ORDER + 1) * HY_WIDTH
    gh_off = ga_off + d
    assert batch == 2 and FFT_NO * FFT_NI == n_fft

    x_all = (x.reshape(n_lat, d), ctx.reshape(batch * n_ctx, d))
    cond = jnp.concatenate([c, c_ctx[None]], axis=0)
    mod = _ada(cond, w_ada, b_ada)
    mod = mod.reshape(depth, batch + 1, 6, 1, d)
    tm_all = n_all // 8
    tm_lat = n_lat // 8

    def mods(l, j):
        return mod[l, :, j]

    h = _norm_mod(*x_all, norm_g[0, 0], mods(0, 0), mods(0, 1), seq)
    for l in range(depth):
        last = l == depth - 1
        rows = n_lat if last else n_all
        tm = tm_lat if last else tm_all
        filt = (filt_w1[l], filt_b1[l], filt_w2[l], filt_b2[l], filt_w3[l], filt_b3[l],
                filt_freq[l], filt_w_out[l])

        u = _in_proj(h, w_in, l, _rope_tables_for_rows(seq, batch, rows - n_lat), rows, tm)
        if last:
            kv_ctx = _gmm(h[n_lat:], (w_in,), *_dense_ids(batch * n_ctx, batch * n_ctx, l), k=d,
                          n=HY_OFF - K_OFF, tm=batch * n_ctx, tn=512, out_dtype=BF16, w_col_off=K_OFF,
                          name="ctx_kv_proj")
            att = _win_attn(u, kv_ctx, 0, 0, attn_sink[l], batch, seq, n_ctx, rows)
        else:
            att = _win_attn(u, u, n_lat // n_ctx, K_OFF, attn_sink[l], batch, seq, n_ctx, rows)
        hy = _hyena(u, conv_w, conv_b, filt, hyena_bias, l, seq, 0, out_rows=rows)
        if not last:
            att = _ctx_attn(u, att, attn_sink[l], batch, seq, n_ctx)
            hy = _hyena(u, conv_w, conv_b, filt, hyena_bias, l, n_ctx, n_lat // n_ctx, hy=hy)
        mrg = _merge(att, hy, u, w_attn_out, w_hyena_out, l, rows, ga_off, gh_off, tm)
        y = _gmm(mrg, (w_out,), *_dense_ids(rows, tm, l), k=d, n=d, tm=tm, tn=512,
                 out_dtype=BF16, name="out_proj")
        router = None
        if l % 2 == 1:
            router = jnp.pad(moe_router[l // 2], ((0, 0), (0, LANES - N_EXPERTS)))
        res = _post(x_all, y, mods(l, 2), norm_g[l, 1], seq, rows,
                    nxt=(norm_g[l, 2], mods(l, 3), mods(l, 4)), router=router)
        x_all, h2 = res[0], res[1]

        if l % 2 == 0:
            i = l // 2
            dff = ffn_w1.shape[-1]
            act = _gmm(h2, (ffn_w1, ffn_w3), *_dense_ids(rows, tm, i), k=d, n=dff, tm=tm, tn=256,
                       out_dtype=BF16, name="ffn_up")
            f = _gmm(act, (ffn_w2,), *_dense_ids(rows, tm // 4, i), k=dff, n=d, tm=tm // 4, tn=512,
                     out_dtype=BF16, w_single_buffer=True, name="ffn_down")
        else:
            i = l // 2
            f = _moe(h2, res[2][:, :N_EXPERTS], moe_w1[i], moe_w3[i], moe_w2[i])
        if last:
            (x_all,) = _post(x_all, f, mods(l, 5), norm_g[l, 3], seq, rows)
        else:
            x_all, h = _post(x_all, f, mods(l, 5), norm_g[l, 3], seq, rows,
                             nxt=(norm_g[l + 1, 0], mods(l + 1, 0), mods(l + 1, 1)))
    return x_all[:n_lat].reshape(batch, seq, d)
```

```python
import functools
import math

import numpy as np
import jax
import jax.numpy as jnp
from jax import lax
from jax.experimental import pallas as pl
from jax.experimental.pallas import tpu as pltpu

F32 = jnp.float32
BF16 = jnp.bfloat16

GRID_W = 64
N_Q_HEADS = 16
N_KV_HEADS = 4
HEAD_DIM = 128
Q_GROUP = N_Q_HEADS // N_KV_HEADS
ATT_WIDTH = N_Q_HEADS * HEAD_DIM
KV_WIDTH = N_KV_HEADS * HEAD_DIM
BLOCK = 128
ROPE_BASE = 10000.0
ROPE_FREQS = HEAD_DIM // 4
MASK_VALUE = -1e30
HY_WIDTH = 2048
HY_ORDER = 2
SHORT_CONV = 3
FILTER_BANDS = 16
FILTER_HIDDEN = 64
DECAY_TARGET = 1e-2
MIN_DECAY = math.log(DECAY_TARGET) / 1.5
MAX_DECAY = math.log(DECAY_TARGET) / 0.3
K_OFF = ATT_WIDTH
V_OFF = K_OFF + KV_WIDTH
HY_OFF = V_OFF + KV_WIDTH
N_EXPERTS = 8
TOP_K = 2
MOE_ROWS = 512
MOE_ROW_PARTS = 2
RMS_EPS = 1e-6

LANES = 128
SUBLANES = 8
VMEM_LIMIT_BYTES = 56 * 1024 * 1024

FFT_NO = 128
FFT_NI = 64
HY_TC = 128


def _cparams(n_axes):
    return pltpu.CompilerParams(dimension_semantics=("arbitrary",) * n_axes,
                                vmem_limit_bytes=VMEM_LIMIT_BYTES)


ADA_CHUNK = 64


def _ada_kernel(c_ref, w_ref, b_ref, o_ref, act_ref):
    n_rows, k, _ = c_ref.shape
    tn = w_ref.shape[-1]

    @pl.when(jnp.logical_and(pl.program_id(0) == 0, pl.program_id(1) == 0))
    def _():
        c = c_ref[...]
        act_ref[...] = c * jax.nn.sigmoid(c)

    def body(i, accs):
        r0 = pl.multiple_of(i * ADA_CHUNK, ADA_CHUNK)
        w = w_ref[0, pl.ds(r0, ADA_CHUNK), :]
        out = []
        for r in range(n_rows):
            a = act_ref[r, pl.ds(r0, ADA_CHUNK), :]
            acc = accs[r]
            for s in range(ADA_CHUNK // SUBLANES):
                rows = slice(s * SUBLANES, (s + 1) * SUBLANES)
                acc = acc + w[rows] * jnp.concatenate([a[rows]] * (tn // LANES), axis=1)
            out.append(acc)
        return tuple(out)

    zero = jnp.zeros((SUBLANES, tn), F32)
    accs = lax.fori_loop(0, k // ADA_CHUNK, body, (zero,) * n_rows)
    bias = b_ref[0]
    for r in range(n_rows):
        o_ref[0, r:r + 1, :] = jnp.sum(accs[r], axis=0, keepdims=True) + bias


def _ada(cond, w_ada, b_ada, tn=512):
    n_layers, d, n = w_ada.shape
    rows = cond.shape[0]
    cb = jnp.broadcast_to(cond[:, :, None], (rows, d, LANES))
    return pl.pallas_call(
        _ada_kernel,
        grid=(n_layers, n // tn),
        in_specs=[pl.BlockSpec((rows, d, LANES), lambda l, j: (0, 0, 0)),
                  pl.BlockSpec((1, d, tn), lambda l, j: (l, 0, j)),
                  pl.BlockSpec((1, 1, tn), lambda l, j: (l, 0, j))],
        out_specs=pl.BlockSpec((1, rows, tn), lambda l, j: (l, 0, j)),
        out_shape=jax.ShapeDtypeStruct((n_layers, rows, n), F32),
        scratch_shapes=[pltpu.VMEM((rows, d, LANES), F32)],
        compiler_params=_cparams(2),
        name="ada_mod",
    )(cb, w_ada, b_ada.reshape(n_layers, 1, n))


def _rms(x):
    return x * lax.rsqrt(jnp.mean(x * x, axis=-1, keepdims=True) + RMS_EPS)


def _group_of_tile(tr, seq):
    return lambda i: (jnp.minimum((i * tr) // seq, 2), 0, 0)


def _two_source_specs(x_lat, x_ctx, tr):
    d = x_lat.shape[1]
    nl = x_lat.shape[0] // tr
    return [pl.BlockSpec((tr, d), lambda i: (jnp.minimum(i, nl - 1), 0)),
            pl.BlockSpec((tr, d), lambda i: (jnp.maximum(i - nl, 0), 0))], nl


def _two_source_rows(lat_ref, ctx_ref, n_lat_tiles):
    return jnp.where(pl.program_id(0) < n_lat_tiles, lat_ref[...], ctx_ref[...])


def _norm_mod2_kernel(lat_ref, ctx_ref, g_ref, sh_ref, sc_ref, o_ref, *, n_lat_tiles):
    y = _rms(_two_source_rows(lat_ref, ctx_ref, n_lat_tiles)) * g_ref[...]
    o_ref[...] = (y * (1.0 + sc_ref[0]) + sh_ref[0]).astype(o_ref.dtype)


def _norm_mod(x_lat, x_ctx, g, sh, sc, seq, tr=256):
    d = x_lat.shape[1]
    t = x_lat.shape[0] + x_ctx.shape[0]
    grp = _group_of_tile(tr, seq)
    x_specs, nl = _two_source_specs(x_lat, x_ctx, tr)
    return pl.pallas_call(
        functools.partial(_norm_mod2_kernel, n_lat_tiles=nl),
        grid=(t // tr,),
        in_specs=x_specs + [pl.BlockSpec((1, d), lambda i: (0, 0)),
                            pl.BlockSpec((1, 1, d), grp),
                            pl.BlockSpec((1, 1, d), grp)],
        out_specs=pl.BlockSpec((tr, d), lambda i: (i, 0)),
        out_shape=jax.ShapeDtypeStruct((t, d), BF16),
        compiler_params=_cparams(1),
        name="norm_mod",
    )(x_lat, x_ctx, g.reshape(1, d), sh, sc)


def _post_kernel(x_ref, *rest, with_next, with_router, with_pair, n_lat_tiles):
    if n_lat_tiles is None:
        x = x_ref[...]
    else:
        ctx_ref, *rest = rest
        x = _two_source_rows(x_ref, ctx_ref, n_lat_tiles)
    if with_pair:
        y0_ref, y1_ref, w_ref, gt_ref, gpost_ref, *rest = rest
        w = w_ref[...]
        y = w[:, 0:1] * y0_ref[...].astype(F32) + w[:, 1:2] * y1_ref[...].astype(F32)
    else:
        y_ref, gt_ref, gpost_ref, *rest = rest
        y = y_ref[...].astype(F32)
    xn = x + gt_ref[0] * (_rms(y) * gpost_ref[...])
    if not with_next:
        (xo_ref,) = rest
        xo_ref[...] = xn
        return
    if with_router:
        gpre_ref, sh_ref, sc_ref, wr_ref, xo_ref, ho_ref, lg_ref = rest
    else:
        gpre_ref, sh_ref, sc_ref, xo_ref, ho_ref = rest
    xo_ref[...] = xn
    h = (_rms(xn) * gpre_ref[...]) * (1.0 + sc_ref[0]) + sh_ref[0]
    ho_ref[...] = h.astype(ho_ref.dtype)
    if with_router:
        lg_ref[...] = jnp.dot(h, wr_ref[...], preferred_element_type=F32,
                              precision=lax.Precision.HIGHEST)


def _post(x_all, y, gt, g_post, seq, rows, nxt=None, router=None, tr=256):
    grp = _group_of_tile(tr, seq)
    if isinstance(x_all, tuple):
        d = x_all[0].shape[1]
        x_specs, n_lat_tiles = _two_source_specs(*x_all, tr)
        x_args = list(x_all)
    else:
        d = x_all.shape[1]
        x_specs, n_lat_tiles = [pl.BlockSpec((tr, d), lambda i: (i, 0))], None
        x_args = [x_all]
    row = pl.BlockSpec((tr, d), lambda i: (i, 0))
    vec = pl.BlockSpec((1, d), lambda i: (0, 0))
    mod = pl.BlockSpec((1, 1, d), grp)
    with_pair = isinstance(y, tuple)
    if with_pair:
        y0, y1, w = y
        in_specs = x_specs + [row, row, pl.BlockSpec((tr, w.shape[1]), lambda i: (i, 0)), mod, vec]
        args = x_args + [y0, y1, w, gt, g_post.reshape(1, d)]
    else:
        in_specs = x_specs + [row, mod, vec]
        args = x_args + [y, gt, g_post.reshape(1, d)]
    out_specs = [row]
    out_shape = [jax.ShapeDtypeStruct((rows, d), F32)]
    if nxt is not None:
        g_pre, sh, sc = nxt
        in_specs += [vec, mod, mod]
        args += [g_pre.reshape(1, d), sh, sc]
        out_specs.append(row)
        out_shape.append(jax.ShapeDtypeStruct((rows, d), BF16))
        if router is not None:
            in_specs.append(pl.BlockSpec((d, LANES), lambda i: (0, 0)))
            args.append(router)
            out_specs.append(pl.BlockSpec((tr, LANES), lambda i: (i, 0)))
            out_shape.append(jax.ShapeDtypeStruct((rows, LANES), F32))
    return pl.pallas_call(
        functools.partial(_post_kernel, with_next=nxt is not None, with_router=router is not None,
                          with_pair=with_pair, n_lat_tiles=n_lat_tiles),
        grid=(rows // tr,),
        in_specs=in_specs,
        out_specs=out_specs,
        out_shape=out_shape,
        compiler_params=_cparams(1),
        name="post_norm",
    )(*args)


def _weight_changed(be_ref, m):
    return jnp.logical_or(m == 0, be_ref[m] != be_ref[jnp.maximum(m - 1, 0)])


def _for_live_row_parts(live_ref, m, tm, n_parts, fn):
    part = tm // n_parts
    for p in range(n_parts):
        pl.when(live_ref[m] > p * part)(functools.partial(fn, slice(p * part, (p + 1) * part)))


def _gmm_kernel(be_ref, nv_ref, live_ref, a_ref, w_ref, o_ref, wb_ref, *, n_parts):
    m = pl.program_id(1)

    @pl.when(jnp.logical_and(live_ref[m] > 0, _weight_changed(be_ref, m)))
    def _():
        wb_ref[...] = w_ref[0].astype(BF16)

    def part(rows):
        o_ref[rows, :] = jnp.dot(a_ref[rows, :], wb_ref[...],
                                 preferred_element_type=F32).astype(o_ref.dtype)

    _for_live_row_parts(live_ref, m, a_ref.shape[0], n_parts, part)


def _swiglu_kernel(be_ref, nv_ref, live_ref, a_ref, w1_ref, w3_ref, o_ref, w1b_ref, w3b_ref, *, n_parts):
    m = pl.program_id(1)

    @pl.when(jnp.logical_and(live_ref[m] > 0, _weight_changed(be_ref, m)))
    def _():
        w1b_ref[...] = w1_ref[0].astype(BF16)
        w3b_ref[...] = w3_ref[0].astype(BF16)

    def part(rows):
        a = a_ref[rows, :]
        g = jnp.dot(a, w1b_ref[...], preferred_element_type=F32)
        u = jnp.dot(a, w3b_ref[...], preferred_element_type=F32)
        o_ref[rows, :] = (g * jax.nn.sigmoid(g) * u).astype(o_ref.dtype)

    _for_live_row_parts(live_ref, m, a_ref.shape[0], n_parts, part)


def _gmm(a, ws, ids, *, k, n, tm, tn, out_dtype, w_col_off=0, rows=None, w_single_buffer=False,
         n_parts=1, name="gmm"):
    rows = a.shape[0] if rows is None else rows
    assert rows % tm == 0 and n % tn == 0 and w_col_off % tn == 0 and tm % n_parts == 0
    off = w_col_off // tn
    w_mode = dict(pipeline_mode=pl.Buffered(1)) if w_single_buffer else {}

    def a_map(j, m, be_ref, nv_ref, live_ref):
        return (jnp.minimum(m, nv_ref[0] - 1), 0)

    def w_map(j, m, be_ref, nv_ref, live_ref):
        return (be_ref[jnp.minimum(m, nv_ref[0] - 1)], 0, j + off)

    def o_map(j, m, be_ref, nv_ref, live_ref):
        return (m, j)

    kernel = _gmm_kernel if len(ws) == 1 else _swiglu_kernel
    return pl.pallas_call(
        functools.partial(kernel, n_parts=n_parts),
        grid_spec=pltpu.PrefetchScalarGridSpec(
            num_scalar_prefetch=3,
            grid=(n // tn, rows // tm),
            in_specs=[pl.BlockSpec((tm, k), a_map)] + [pl.BlockSpec((1, k, tn), w_map, **w_mode)] * len(ws),
            out_specs=pl.BlockSpec((tm, tn), o_map),
            scratch_shapes=[pltpu.VMEM((k, tn), BF16)] * len(ws)),
        out_shape=jax.ShapeDtypeStruct((rows, n), out_dtype),
        compiler_params=_cparams(2),
        name=name,
    )(*ids, a, *ws)


def _dense_ids(rows, tm, idx):
    nb = rows // tm
    return (jnp.full((nb,), idx, jnp.int32), jnp.full((1,), nb, jnp.int32),
            jnp.full((nb,), tm, jnp.int32))


def _merge_kernel(att_ref, hy_ref, ga_ref, gh_ref, wa_ref, wh_ref, o_ref, wab_ref, whb_ref):
    @pl.when(pl.program_id(1) == 0)
    def _():
        wab_ref[...] = wa_ref[0].astype(BF16)
        whb_ref[...] = wh_ref[0].astype(BF16)

    pa = jnp.dot(att_ref[...], wab_ref[...], preferred_element_type=F32)
    ph = jnp.dot(hy_ref[...], whb_ref[...], preferred_element_type=F32)
    ga = jax.nn.sigmoid(ga_ref[...].astype(F32))
    gh = jax.nn.sigmoid(gh_ref[...].astype(F32))
    o_ref[...] = (ga * pa + gh * ph).astype(o_ref.dtype)


def _merge(att, hy, u, w_ao, w_ho, layer, rows, ga_off, gh_off, tm, tn=512):
    ka, d = w_ao.shape[1:]
    kh = w_ho.shape[1]
    return pl.pallas_call(
        _merge_kernel,
        grid=(d // tn, rows // tm),
        in_specs=[pl.BlockSpec((tm, ka), lambda j, m: (m, 0)),
                  pl.BlockSpec((tm, kh), lambda j, m: (m, 0)),
                  pl.BlockSpec((tm, tn), lambda j, m: (m, ga_off // tn + j)),
                  pl.BlockSpec((tm, tn), lambda j, m: (m, gh_off // tn + j)),
                  pl.BlockSpec((1, ka, tn), lambda j, m: (layer, 0, j)),
                  pl.BlockSpec((1, kh, tn), lambda j, m: (layer, 0, j))],
        out_specs=pl.BlockSpec((tm, tn), lambda j, m: (m, j)),
        out_shape=jax.ShapeDtypeStruct((rows, d), BF16),
        scratch_shapes=[pltpu.VMEM((ka, tn), BF16), pltpu.VMEM((kh, tn), BF16)],
        compiler_params=_cparams(2),
        name="gated_merge",
    )(att, hy, u, u, w_ao, w_ho)


def _rope_tables(seq):
    rows = seq // GRID_W
    row = jnp.repeat(jnp.arange(rows), GRID_W).astype(F32)
    col = jnp.tile(jnp.arange(GRID_W), rows).astype(F32)
    inv = ROPE_BASE ** (-jnp.arange(ROPE_FREQS, dtype=F32) / ROPE_FREQS)
    ang = jnp.stack([row[:, None] * inv, col[:, None] * inv], axis=1)
    cos, sin = jnp.cos(ang), jnp.sin(ang)
    zero = jnp.zeros_like(sin)
    cos_t = jnp.stack([cos, cos], axis=2).reshape(seq, HEAD_DIM)
    s_lo = jnp.stack([-sin, zero], axis=2).reshape(seq, HEAD_DIM)
    s_hi = jnp.stack([zero, sin], axis=2).reshape(seq, HEAD_DIM)
    return cos_t, s_lo, s_hi


def _rope_tables_for_rows(seq, batch, n_extra):
    cos_t, s_lo, s_hi = (jnp.tile(t, (batch, 1)) for t in _rope_tables(seq))
    if n_extra:
        cos_t = jnp.concatenate([cos_t, jnp.ones((n_extra, HEAD_DIM), F32)], axis=0)
        pad = jnp.zeros((n_extra, HEAD_DIM), F32)
        s_lo, s_hi = jnp.concatenate([s_lo, pad], axis=0), jnp.concatenate([s_hi, pad], axis=0)
    return cos_t, s_lo, s_hi


def _in_proj_kernel(a_ref, w_ref, c_ref, lo_ref, hi_ref, o_ref, wb_ref, *, n_rope_tiles):
    j = pl.program_id(0)

    @pl.when(pl.program_id(1) == 0)
    def _():
        wb_ref[...] = w_ref[0].astype(BF16)

    acc = jnp.dot(a_ref[...], wb_ref[...], preferred_element_type=F32)

    @pl.when(j < n_rope_tiles)
    def _():
        c, lo, hi = c_ref[...], lo_ref[...], hi_ref[...]
        for hd in range(acc.shape[1] // HEAD_DIM):
            sl = slice(hd * HEAD_DIM, (hd + 1) * HEAD_DIM)
            x = acc[:, sl]
            up = pltpu.roll(x, HEAD_DIM - ROPE_FREQS, 1)
            dn = pltpu.roll(x, ROPE_FREQS, 1)
            o_ref[:, sl] = (x * c + up * lo + dn * hi).astype(o_ref.dtype)

    @pl.when(j >= n_rope_tiles)
    def _():
        o_ref[...] = acc.astype(o_ref.dtype)


def _in_proj(h, w_in, layer, tables, rows, tm, tn=512):
    k, n = w_in.shape[1:]
    assert rows % tm == 0 and n % tn == 0 and V_OFF % tn == 0
    n_rope_tiles = V_OFF // tn
    tab = pl.BlockSpec((tm, HEAD_DIM), lambda j, m: (jnp.where(j < n_rope_tiles, m, 0), 0))
    return pl.pallas_call(
        functools.partial(_in_proj_kernel, n_rope_tiles=n_rope_tiles),
        grid=(n // tn, rows // tm),
        in_specs=[pl.BlockSpec((tm, k), lambda j, m: (m, 0)),
                  pl.BlockSpec((1, k, tn), lambda j, m: (layer, 0, j)),
                  tab, tab, tab],
        out_specs=pl.BlockSpec((tm, tn), lambda j, m: (m, j)),
        out_shape=jax.ShapeDtypeStruct((rows, n), BF16),
        scratch_shapes=[pltpu.VMEM((k, tn), BF16)],
        compiler_params=_cparams(2),
        name="in_proj",
    )(h, w_in, *tables)


ATTN_ROW_CHUNK = 32
LOG2_E = math.log2(math.e)


def _scores(q, k):
    return lax.dot_general(q, k, (((1,), (1,)), ((), ())), preferred_element_type=F32)


def _win_attn_kernel(sink_ref, q_ref, kp_ref, kc_ref, kn_ref, vp_ref, vc_ref, vn_ref,
                     kx_ref, vx_ref, o_ref, s_ref, p_ref, den_ref):
    n = pl.program_id(1)
    nb = pl.num_programs(1)
    scale = HEAD_DIM ** -0.5
    n_keys = 3 * BLOCK + kx_ref.shape[0]
    qi = lax.broadcasted_iota(jnp.int32, (BLOCK, n_keys), 0)
    kj = lax.broadcasted_iota(jnp.int32, (BLOCK, n_keys), 1)
    bad_prev = jnp.logical_and(kj < BLOCK, jnp.logical_or(kj < qi, n == 0))
    bad_next = jnp.logical_and(jnp.logical_and(kj >= 2 * BLOCK, kj < 3 * BLOCK),
                               jnp.logical_or(kj - 2 * BLOCK > qi, n == nb - 1))
    valid = jnp.logical_not(jnp.logical_or(bad_prev, bad_next))
    head_lanes = lambda head: slice(head * HEAD_DIM, (head + 1) * HEAD_DIM)
    for h in range(N_KV_HEADS):
        hs = head_lanes(h)
        keys = jnp.concatenate([kp_ref[:, hs], kc_ref[:, hs], kn_ref[:, hs], kx_ref[:, hs]], axis=0)
        for g in range(Q_GROUP):
            head = h * Q_GROUP + g
            s_ref[head] = jnp.where(valid, _scores(q_ref[:, head_lanes(head)], keys) * (scale * LOG2_E),
                                    MASK_VALUE)
    for head in range(N_Q_HEADS):
        sink = sink_ref[head // Q_GROUP, head % Q_GROUP] * LOG2_E
        for r0 in range(0, BLOCK, ATTN_ROW_CHUNK):
            rows = slice(r0, r0 + ATTN_ROW_CHUNK)
            s = s_ref[head, rows, :]
            mx = jnp.maximum(jnp.max(s, axis=-1, keepdims=True), sink)
            p = jnp.exp2(s - mx)
            den_ref[head, rows, :] = jnp.sum(p, axis=-1, keepdims=True) + jnp.exp2(sink - mx)
            p_ref[head, rows, :] = p.astype(BF16)
    for h in range(N_KV_HEADS):
        hs = head_lanes(h)
        vals = jnp.concatenate([vp_ref[:, hs], vc_ref[:, hs], vn_ref[:, hs], vx_ref[:, hs]], axis=0)
        for g in range(Q_GROUP):
            head = h * Q_GROUP + g
            o = jnp.dot(p_ref[head], vals, preferred_element_type=F32)
            o_ref[:, head_lanes(head)] = (o / den_ref[head]).astype(o_ref.dtype)


def _win_attn(u, kv_ctx, ctx_row_blk, ctx_k_col, sink, batch, seq, n_ctx, out_rows):
    nb = seq // BLOCK
    assert K_OFF % KV_WIDTH == 0 and V_OFF % KV_WIDTH == 0 and ctx_k_col % KV_WIDTH == 0
    kcol = K_OFF // KV_WIDTH
    vcol = V_OFF // KV_WIDTH
    xk = ctx_k_col // KV_WIDTH

    def blk(shift, col):
        def index(b, n):
            return (b * nb + jnp.clip(n + shift, 0, nb - 1), col)
        return pl.BlockSpec((BLOCK, KV_WIDTH), index)

    return pl.pallas_call(
        _win_attn_kernel,
        grid=(batch, nb),
        in_specs=[pl.BlockSpec(memory_space=pltpu.SMEM),
                  pl.BlockSpec((BLOCK, ATT_WIDTH), lambda b, n: (b * nb + n, 0)),
                  blk(-1, kcol), blk(0, kcol), blk(1, kcol),
                  blk(-1, vcol), blk(0, vcol), blk(1, vcol),
                  pl.BlockSpec((n_ctx, KV_WIDTH), lambda b, n: (ctx_row_blk + b, xk)),
                  pl.BlockSpec((n_ctx, KV_WIDTH), lambda b, n: (ctx_row_blk + b, xk + 1))],
        out_specs=pl.BlockSpec((BLOCK, ATT_WIDTH), lambda b, n: (b * nb + n, 0)),
        out_shape=jax.ShapeDtypeStruct((out_rows, ATT_WIDTH), BF16),
        scratch_shapes=[pltpu.VMEM((N_Q_HEADS, BLOCK, 3 * BLOCK + n_ctx), F32),
                        pltpu.VMEM((N_Q_HEADS, BLOCK, 3 * BLOCK + n_ctx), BF16),
                        pltpu.VMEM((N_Q_HEADS, BLOCK, 1), F32)],
        compiler_params=_cparams(2),
        name="window_attention",
    )(sink.reshape(N_KV_HEADS, Q_GROUP), u, u, u, u, u, u, u, kv_ctx, kv_ctx)


def _ctx_attn_kernel(sink_ref, q_ref, k_ref, v_ref, att_ref, o_ref):
    del att_ref
    h = pl.program_id(1)
    scale = HEAD_DIM ** -0.5
    k, v = k_ref[...], v_ref[...]
    for g in range(Q_GROUP):
        sl = slice(g * HEAD_DIM, (g + 1) * HEAD_DIM)
        s = _scores(q_ref[:, sl], k) * scale
        sink = sink_ref[h, g]
        mx = jnp.maximum(jnp.max(s, axis=-1, keepdims=True), sink)
        p = jnp.exp(s - mx)
        den = jnp.sum(p, axis=-1, keepdims=True) + jnp.exp(sink - mx)
        o = jnp.dot(p.astype(BF16), v, preferred_element_type=F32)
        o_ref[:, sl] = (o / den).astype(o_ref.dtype)


def _ctx_attn(u, att, sink, batch, seq, n_ctx):
    hd = HEAD_DIM
    qw = Q_GROUP * hd
    ctx_blk = (batch * seq) // n_ctx
    return pl.pallas_call(
        _ctx_attn_kernel,
        grid=(batch, N_KV_HEADS),
        in_specs=[pl.BlockSpec(memory_space=pltpu.SMEM),
                  pl.BlockSpec((n_ctx, qw), lambda b, h: (ctx_blk + b, h)),
                  pl.BlockSpec((n_ctx, hd), lambda b, h: (ctx_blk + b, K_OFF // hd + h)),
                  pl.BlockSpec((n_ctx, hd), lambda b, h: (ctx_blk + b, V_OFF // hd + h)),
                  pl.BlockSpec(memory_space=pl.ANY)],
        out_specs=pl.BlockSpec((n_ctx, qw), lambda b, h: (ctx_blk + b, h)),
        out_shape=jax.ShapeDtypeStruct(att.shape, att.dtype),
        input_output_aliases={4: 0},
        compiler_params=_cparams(2),
        name="context_attention",
    )(sink.reshape(N_KV_HEADS, Q_GROUP), u, u, u, att)


def _short_conv_kernel(u_ref, w_ref, b_ref, o_ref):
    x = u_ref[...].astype(F32)
    n = x.shape[0]
    r = lax.broadcasted_iota(jnp.int32, x.shape, 0)
    prev = jnp.where(r == 0, 0.0, pltpu.roll(x, 1, 0))
    nxt = jnp.where(r == n - 1, 0.0, pltpu.roll(x, n - 1, 0))
    w = w_ref[0]
    o_ref[...] = (prev * w[0:1] + x * w[1:2] + nxt * w[2:3] + b_ref[0]).astype(o_ref.dtype)


def _short_conv(u, conv_w, conv_b, layer, n_seq, seg, row_blk_off, tw=256):
    width = conv_w.shape[-1]
    cb = conv_b.reshape(conv_b.shape[0], 1, width)
    return pl.pallas_call(
        _short_conv_kernel,
        grid=(n_seq, width // tw),
        in_specs=[pl.BlockSpec((seg, tw), lambda s, j: (row_blk_off + s, HY_OFF // tw + j)),
                  pl.BlockSpec((1, SHORT_CONV, tw), lambda s, j: (layer, 0, j)),
                  pl.BlockSpec((1, 1, tw), lambda s, j: (layer, 0, j))],
        out_specs=pl.BlockSpec((seg, tw), lambda s, j: (s, j)),
        out_shape=jax.ShapeDtypeStruct((n_seq * seg, width), BF16),
        compiler_params=_cparams(2),
        name="short_conv",
    )(u, conv_w, cb)


def _filter_positions(n, n_fft):
    t = jnp.linspace(0.0, 1.0, n, dtype=F32)[:, None]
    w = 2.0 * math.pi * jnp.arange(n, dtype=F32)[:, None] / n
    bands = jnp.linspace(1e-4, FILTER_BANDS - 1, FILTER_BANDS, dtype=F32)[None, :]
    z = jnp.concatenate([t, jnp.cos(bands * w), -jnp.sin(bands * w)], axis=-1)
    zt = jnp.concatenate([z, t], axis=-1)
    mid = jnp.zeros((n_fft - 2 * n + 1, zt.shape[1]), F32)
    full = jnp.concatenate([zt, mid, zt[1:][::-1]], axis=0)
    feat = jnp.pad(full[:, :-1], ((0, 0), (0, FILTER_HIDDEN - (zt.shape[1] - 1))))
    return feat, full[:, -1:]


def _filter_kernel(z_ref, t_ref, w1_ref, b1_ref, w2_ref, b2_ref, w3_ref, b3_ref, fr_ref,
                   wo_ref, dl_ref, o_ref, hid_ref, *, n, n_fft):
    hi = lax.Precision.HIGHEST
    tr = z_ref.shape[0]

    @pl.when(pl.program_id(1) == 0)
    def _():
        fr = fr_ref[...]
        h = jnp.sin(fr * (jnp.dot(z_ref[...], w1_ref[...], preferred_element_type=F32, precision=hi)
                          + b1_ref[...]))
        h = jnp.sin(fr * (jnp.dot(h, w2_ref[...], preferred_element_type=F32, precision=hi)
                          + b2_ref[...]))
        h = jnp.sin(fr * (jnp.dot(h, w3_ref[...], preferred_element_type=F32, precision=hi)
                          + b3_ref[...]))
        h_hi = h.astype(BF16)
        hid_ref[0] = h_hi
        hid_ref[1] = (h - h_hi.astype(F32)).astype(BF16)

    h_hi, h_lo = hid_ref[0], hid_ref[1]
    row = pl.program_id(0) * tr + lax.broadcasted_iota(jnp.int32, (tr, 1), 0)
    live = jnp.logical_or(row < n, row > n_fft - n)
    decay = jnp.where(live, jnp.exp(-t_ref[...] * dl_ref[...]), 0.0)
    for o in range(HY_ORDER):
        w = wo_ref[o, 0]
        w_hi = w.astype(BF16)
        w_lo = (w - w_hi.astype(F32)).astype(BF16)
        acc = (jnp.dot(h_hi, w_hi, preferred_element_type=F32)
               + jnp.dot(h_lo, w_hi, preferred_element_type=F32)
               + jnp.dot(h_hi, w_lo, preferred_element_type=F32))
        o_ref[o] = acc * decay


def _hyena_filter(n, n_fft, w1, b1, w2, b2, w3, b3, freq, w_out, tw=512):
    tr = min(512, n)
    assert n % tr == 0 and n_fft % tr == 0
    feat, tpos = _filter_positions(n, n_fft)
    hid = FILTER_HIDDEN
    w1p = jnp.pad(w1, ((0, hid - w1.shape[0]), (0, 0)))
    wo = w_out.reshape(hid, HY_ORDER, 2, HY_WIDTH).transpose(1, 2, 0, 3)
    deltas = jnp.abs(jnp.linspace(MIN_DECAY, MAX_DECAY, HY_WIDTH, dtype=F32)).reshape(1, HY_WIDTH)
    small = lambda shape: pl.BlockSpec(shape, lambda r, j: (0,) * len(shape))
    return pl.pallas_call(
        functools.partial(_filter_kernel, n=n, n_fft=n_fft),
        grid=(n_fft // tr, HY_WIDTH // tw),
        in_specs=[pl.BlockSpec((tr, hid), lambda r, j: (r, 0)),
                  pl.BlockSpec((tr, 1), lambda r, j: (r, 0)),
                  small((hid, hid)), small((1, hid)), small((hid, hid)), small((1, hid)),
                  small((hid, hid)), small((1, hid)), small((1, hid)),
                  pl.BlockSpec((HY_ORDER, 1, hid, tw), lambda r, j: (0, jnp.where(r * tr >= n, 1, 0), 0, j)),
                  pl.BlockSpec((1, tw), lambda r, j: (0, j))],
        out_specs=pl.BlockSpec((HY_ORDER, tr, tw), lambda r, j: (0, r, j)),
        out_shape=jax.ShapeDtypeStruct((HY_ORDER, n_fft, HY_WIDTH), F32),
        scratch_shapes=[pltpu.VMEM((2, tr, hid), BF16)],
        compiler_params=_cparams(2),
        name="hyena_filter",
    )(feat, tpos, w1p, b1.reshape(1, hid), w2, b2.reshape(1, hid), w3, b3.reshape(1, hid),
      freq.reshape(1, hid), wo, deltas)


@functools.lru_cache(maxsize=None)
def _dft_constants(n_fft, no, ni):
    jo = np.arange(no)
    k1 = np.arange(no)
    half = no // 2
    f1r, f1d, f1i = [], [], []
    for i in range(ni):
        ang = 2.0 * np.pi * np.outer(k1, jo * ni + i) / n_fft
        c, s = np.cos(ang), np.sin(ang)
        f1r.append(np.concatenate([c, -s], axis=0))
        ch, sh = c[:, :half], s[:, :half]
        f1d.append(np.block([[ch, sh], [-sh, ch]]))
        angi = 2.0 * np.pi * np.outer(np.arange(half) * ni + i, k1) / n_fft
        ci, si = np.cos(angi), np.sin(angi)
        f1i.append(np.block([[ci, -si], [si, ci]]))
    ang3 = 2.0 * np.pi * np.outer(np.arange(ni), np.arange(ni)) / ni
    c3, s3 = np.cos(ang3), np.sin(ang3)
    m3 = np.block([[c3, s3], [-s3, c3]])
    m3c = np.block([[c3, -s3], [s3, c3]])
    as_bf16 = lambda a: np.asarray(a, np.float32).astype(BF16)
    return (as_bf16(np.stack(f1r)), as_bf16(np.stack(f1d)), as_bf16(np.stack(f1i)),
            as_bf16(m3), as_bf16(m3c))


def _const_spec(shape, n_axes):
    zeros = (0,) * len(shape)
    if n_axes == 2:
        index = lambda a, b: zeros
    else:
        index = lambda a: zeros
    return pl.BlockSpec(shape, index, pipeline_mode=pl.Buffered(1))


DFT_UNROLL = 32


def _pitch(size):
    return size + SUBLANES


def _block_at(i, size):
    return pl.ds(pl.multiple_of(i * _pitch(size), SUBLANES), size)


def _dft_stage3_rhs(a_ref, k1, no, ni):
    re = a_ref[pl.ds(k1, ni, stride=_pitch(2 * no)), :]
    im = a_ref[pl.ds(no + k1, ni, stride=_pitch(2 * no)), :]
    return jnp.concatenate([re, im], axis=0).astype(BF16)


def _dft_stage3_rhs_pair(a_ref, pair, no, ni):
    return jnp.concatenate([_dft_stage3_rhs(a_ref, 2 * pair, no, ni),
                            _dft_stage3_rhs(a_ref, 2 * pair + 1, no, ni)], axis=1)


def _spectrum_kernel(h_ref, f1r_ref, m3_ref, o_ref, a_ref, *, no, ni):
    inv_n = 1.0 / (no * ni)
    tc = h_ref.shape[-1]

    def stage1(ji, carry):
        rhs = h_ref[0, pl.ds(ji, no, stride=ni), :].astype(BF16)
        a_ref[_block_at(ji, 2 * no), :] = jnp.dot(f1r_ref[ji], rhs, preferred_element_type=F32)
        return carry

    lax.fori_loop(0, ni, stage1, 0, unroll=DFT_UNROLL)

    def stage3(pair, carry):
        x = jnp.dot(m3_ref[...], _dft_stage3_rhs_pair(a_ref, pair, no, ni),
                    preferred_element_type=F32) * inv_n
        r0 = pl.multiple_of(pair * 2 * ni, 2 * ni)
        for i in range(2):
            lanes = slice(i * tc, (i + 1) * tc)
            o_ref[0, 0, pl.ds(r0 + i * ni, ni), :] = x[:ni, lanes].astype(o_ref.dtype)
            o_ref[0, 1, pl.ds(r0 + i * ni, ni), :] = x[ni:, lanes].astype(o_ref.dtype)
        return carry

    lax.fori_loop(0, no // 2, stage3, 0, unroll=DFT_UNROLL // 2)


def _filter_spectrum(hfull, no=FFT_NO, ni=FFT_NI, tc=HY_TC):
    n_ord, n_fft, width = hfull.shape
    f1r, _, _, m3, _ = _dft_constants(n_fft, no, ni)
    return pl.pallas_call(
        functools.partial(_spectrum_kernel, no=no, ni=ni),
        grid=(n_ord, width // tc),
        in_specs=[pl.BlockSpec((1, n_fft, tc), lambda o, c: (o, 0, c)),
                  _const_spec(f1r.shape, 2), _const_spec(m3.shape, 2)],
        out_specs=pl.BlockSpec((1, 2, n_fft, tc), lambda o, c: (o, 0, 0, c)),
        out_shape=jax.ShapeDtypeStruct((n_ord, 2, n_fft, width), BF16),
        scratch_shapes=[pltpu.VMEM((ni * _pitch(2 * no), tc), F32)],
        compiler_params=_cparams(2),
        name="hyena_spectrum",
    )(hfull, f1r, m3)


def _long_conv_kernel(v_ref, g_ref, spec_ref, bias_ref, f1d_ref, f1i_ref, m3_ref, m3c_ref,
                      o_ref, z_ref, a_ref, b_ref, *, n, no, ni):
    order = pl.program_id(1)
    half = no // 2
    assert n == half * ni

    @pl.when(order == 0)
    def _():
        def load(jo, carry):
            r0 = pl.multiple_of(jo * ni, ni)
            for b in range(2):
                z_ref[b, _block_at(jo, ni), :] = v_ref[pl.ds(b * n + r0, ni), :].astype(F32)
            return carry

        lax.fori_loop(0, half, load, 0, unroll=DFT_UNROLL)

    def stage1(ji, carry):
        zr = z_ref[0, pl.ds(ji, half, stride=_pitch(ni)), :]
        zi = z_ref[1, pl.ds(ji, half, stride=_pitch(ni)), :]
        rhs = jnp.concatenate([zr, zi], axis=0).astype(BF16)
        a_ref[_block_at(ji, 2 * no), :] = jnp.dot(f1d_ref[ji], rhs, preferred_element_type=F32)
        return carry

    lax.fori_loop(0, ni, stage1, 0, unroll=DFT_UNROLL)

    def stage3(pair, carry):
        x = jnp.dot(m3_ref[...], _dft_stage3_rhs_pair(a_ref, pair, no, ni), preferred_element_type=F32)
        r0 = pl.multiple_of(pair * 2 * ni, 2 * ni)
        side_by_side = lambda s: jnp.concatenate([s[:ni], s[ni:]], axis=1).astype(F32)
        hr = side_by_side(spec_ref[0, 0, pl.ds(r0, 2 * ni), :])
        hi = side_by_side(spec_ref[0, 1, pl.ds(r0, 2 * ni), :])
        xr, xi = x[:ni], x[ni:]
        y = jnp.concatenate([xr * hr - xi * hi, xr * hi + xi * hr], axis=0).astype(BF16)
        b = jnp.dot(m3c_ref[...], y, preferred_element_type=F32)
        tc = b.shape[1] // 2
        b_ref[_block_at(2 * pair, 2 * ni), :] = b[:, :tc]
        b_ref[_block_at(2 * pair + 1, 2 * ni), :] = b[:, tc:]
        return carry

    lax.fori_loop(0, no // 2, stage3, 0, unroll=DFT_UNROLL // 2)

    def stage1_inv(t2, carry):
        br = b_ref[pl.ds(t2, no, stride=_pitch(2 * ni)), :]
        bi = b_ref[pl.ds(ni + t2, no, stride=_pitch(2 * ni)), :]
        rhs = jnp.concatenate([br, bi], axis=0).astype(BF16)
        a_ref[pl.ds(pl.multiple_of(t2 * _pitch(2 * no), SUBLANES), no), :] = jnp.dot(
            f1i_ref[t2], rhs, preferred_element_type=F32)
        return carry

    lax.fori_loop(0, ni, stage1_inv, 0, unroll=DFT_UNROLL)

    bias = bias_ref[0, 0]

    def finish(t1, carry):
        r0 = pl.multiple_of(t1 * ni, ni)
        conv = (a_ref[pl.ds(t1, ni, stride=_pitch(2 * no)), :],
                a_ref[pl.ds(half + t1, ni, stride=_pitch(2 * no)), :])
        for b in range(2):
            z = z_ref[b, _block_at(t1, ni), :]
            gate = g_ref[pl.ds(b * n + r0, ni), :].astype(F32)
            zn = gate * (conv[b] + z * bias)
            z_ref[b, _block_at(t1, ni), :] = zn
            o_ref[pl.ds(b * n + r0, ni), :] = zn.astype(o_ref.dtype)
        return carry

    lax.fori_loop(0, half, finish, 0, unroll=DFT_UNROLL)


def _long_conv(uc, spec, hy_bias, layer, n, out_rows, no=FFT_NO, ni=FFT_NI, tc=HY_TC):
    n_fft = no * ni
    assert 2 * n == n_fft
    _, f1d, f1i, m3, m3c = _dft_constants(n_fft, no, ni)
    nct = HY_WIDTH // tc
    bias = hy_bias.reshape(hy_bias.shape[0], HY_ORDER, 1, HY_WIDTH)
    return pl.pallas_call(
        functools.partial(_long_conv_kernel, n=n, no=no, ni=ni),
        grid=(nct, HY_ORDER),
        in_specs=[pl.BlockSpec((2 * n, tc), lambda c, o: (0, c)),
                  pl.BlockSpec((2 * n, tc), lambda c, o: (0, (1 + o) * nct + c)),
                  pl.BlockSpec((1, 2, n_fft, tc), lambda c, o: (o, 0, 0, c)),
                  pl.BlockSpec((1, 1, 1, tc), lambda c, o: (layer, o, 0, c)),
                  _const_spec(f1d.shape, 2), _const_spec(f1i.shape, 2),
                  _const_spec(m3.shape, 2), _const_spec(m3c.shape, 2)],
        out_specs=pl.BlockSpec((2 * n, tc), lambda c, o: (0, c)),
        out_shape=jax.ShapeDtypeStruct((out_rows, HY_WIDTH), BF16),
        scratch_shapes=[pltpu.VMEM((2, (no // 2) * _pitch(ni), tc), F32),
                        pltpu.VMEM((ni * _pitch(2 * no), tc), F32),
                        pltpu.VMEM((no * _pitch(2 * ni), tc), F32)],
        compiler_params=_cparams(2),
        name="hyena_long_conv",
    )(uc, uc, spec, bias, f1d, f1i, m3, m3c)


@functools.lru_cache(maxsize=None)
def _small_dft_constants(n):
    n_fft = 2 * n
    k = np.arange(n_fft)
    ang = 2.0 * np.pi * np.outer(k, np.arange(n_fft)) / n_fft
    c, s = np.cos(ang), np.sin(ang)
    fr = np.concatenate([c, -s], axis=0)
    ch, sh = c[:, :n], s[:, :n]
    fd = np.block([[ch, sh], [-sh, ch]])
    ci, si = c[:n, :], s[:n, :]
    fi = np.block([[ci, -si], [si, ci]])
    as_bf16 = lambda a: np.asarray(a, np.float32).astype(BF16)
    return as_bf16(fr), as_bf16(fd), as_bf16(fi)


def _small_conv_kernel(v_ref, g1_ref, g2_ref, h_ref, bias_ref, fr_ref, fd_ref, fi_ref, hy_ref, o_ref,
                       *, n):
    del hy_ref
    n_fft = 2 * n
    z = v_ref[...].astype(F32)
    for o, g_ref in enumerate((g1_ref, g2_ref)):
        hs = jnp.dot(fr_ref[...], h_ref[o].astype(BF16), preferred_element_type=F32) * (1.0 / n_fft)
        x = jnp.dot(fd_ref[...], z.astype(BF16), preferred_element_type=F32)
        xr, xi, hr, hi = x[:n_fft], x[n_fft:], hs[:n_fft], hs[n_fft:]
        y = jnp.concatenate([xr * hr - xi * hi, xr * hi + xi * hr], axis=0).astype(BF16)
        conv = jnp.dot(fi_ref[...], y, preferred_element_type=F32)
        z = g_ref[...].astype(F32) * (conv + z * bias_ref[0, o])
    o_ref[...] = z.astype(o_ref.dtype)


def _small_conv(uc, hfull, hy_bias, layer, n, hy, row_blk, tc=256):
    fr, fd, fi = _small_dft_constants(n)
    nct = HY_WIDTH // tc
    bias = hy_bias.reshape(hy_bias.shape[0], HY_ORDER, 1, HY_WIDTH)
    col = lambda k: pl.BlockSpec((2 * n, tc), lambda c: (0, k * nct + c))
    return pl.pallas_call(
        functools.partial(_small_conv_kernel, n=n),
        grid=(nct,),
        in_specs=[col(0), col(1), col(2),
                  pl.BlockSpec((HY_ORDER, 2 * n, tc), lambda c: (0, 0, c)),
                  pl.BlockSpec((1, HY_ORDER, 1, tc), lambda c: (layer, 0, 0, c)),
                  _const_spec(fr.shape, 1), _const_spec(fd.shape, 1), _const_spec(fi.shape, 1),
                  pl.BlockSpec(memory_space=pl.ANY)],
        out_specs=pl.BlockSpec((2 * n, tc), lambda c: (row_blk, c)),
        out_shape=jax.ShapeDtypeStruct(hy.shape, hy.dtype),
        input_output_aliases={8: 0},
        compiler_params=_cparams(1),
        name="hyena_small_conv",
    )(uc, uc, uc, hfull, bias, fr, fd, fi, hy)


def _hyena(u, conv_w, conv_b, filt, hy_bias, layer, seg, row_blk_off, hy=None, out_rows=None):
    uc = _short_conv(u, conv_w, conv_b, layer, 2, seg, row_blk_off)
    hfull = _hyena_filter(seg, 2 * seg, *filt)
    if 2 * seg == FFT_NO * FFT_NI:
        return _long_conv(uc, _filter_spectrum(hfull), hy_bias, layer, seg, out_rows)
    assert row_blk_off % 2 == 0
    return _small_conv(uc, hfull, hy_bias, layer, seg, hy, row_blk_off // 2)


def _moe(h2, logits, w1, w3, w2):
    t, d = h2.shape
    dff = w1.shape[-1]
    top_logit, top_idx = lax.top_k(logits, TOP_K)
    gate = jax.nn.softmax(top_logit, axis=-1)
    n_assign = t * TOP_K
    flat_e = top_idx.reshape(-1)
    flat_tok = jnp.repeat(jnp.arange(t, dtype=jnp.int32), TOP_K)
    onehot = (flat_e[:, None] == jnp.arange(N_EXPERTS, dtype=flat_e.dtype)[None, :]).astype(jnp.int32)
    counts = jnp.sum(onehot, axis=0)
    rank = jnp.sum((jnp.cumsum(onehot, axis=0) - onehot) * onehot, axis=1)
    padded = (counts + MOE_ROWS - 1) // MOE_ROWS * MOE_ROWS
    pad_end = jnp.cumsum(padded)
    pad_start = pad_end - padded
    dest = (jnp.sum(onehot * pad_start[None, :], axis=1) + rank).astype(jnp.int32)
    n_blocks = n_assign // MOE_ROWS + N_EXPERTS
    n_slots = n_blocks * MOE_ROWS
    slot_tok = (jnp.arange(n_slots, dtype=jnp.int32) % t).at[dest].set(flat_tok)
    block_start = jnp.arange(n_blocks, dtype=pad_end.dtype) * MOE_ROWS
    block_expert = jnp.minimum(jnp.sum((pad_end[None, :] <= block_start[:, None]).astype(jnp.int32), axis=1),
                               N_EXPERTS - 1).astype(jnp.int32)
    n_used = (pad_end[-1] // MOE_ROWS).astype(jnp.int32).reshape(1)
    group_end = pad_start + counts
    block_end = jnp.sum((block_expert[:, None] == jnp.arange(N_EXPERTS, dtype=jnp.int32)[None, :])
                        * group_end[None, :], axis=1)
    live = jnp.clip(block_end - block_start, 0, MOE_ROWS).astype(jnp.int32)
    ids = (block_expert, n_used, live)
    xs = h2[slot_tok]
    act = _gmm(xs, (w1, w3), ids, k=d, n=dff, tm=MOE_ROWS, tn=512, out_dtype=BF16,
               n_parts=MOE_ROW_PARTS, name="moe_up")
    ys = _gmm(act, (w2,), ids, k=dff, n=d, tm=MOE_ROWS, tn=1024, out_dtype=BF16,
              n_parts=MOE_ROW_PARTS, name="moe_down")
    pos = dest.reshape(t, TOP_K)
    return ys[pos[:, 0]], ys[pos[:, 1]], gate


def kernel(x, c, ctx, c_ctx, w_ada, b_ada, norm_g, w_in, attn_sink, conv_w, conv_b,
           filt_w1, filt_b1, filt_w2, filt_b2, filt_w3, filt_b3, filt_freq, filt_w_out, hyena_bias,
           w_attn_out, w_hyena_out, w_out, ffn_w1, ffn_w3, ffn_w2,
           moe_router, moe_w1, moe_w3, moe_w2):
    batch, seq, d = x.shape
    n_ctx = ctx.shape[1]
    depth = w_in.shape[0]
    in_width = w_in.shape[-1]
    n_lat = batch * seq
    n_all = n_lat + batch * n_ctx
    n_fft = 2 * seq
    ga_off = HY_OFF + (HY_ORDER + 1) * HY_WIDTH
    gh_off = ga_off + d
    assert batch == 2 and FFT_NO * FFT_NI == n_fft

    x_all = (x.reshape(n_lat, d), ctx.reshape(batch * n_ctx, d))
    cond = jnp.concatenate([c, c_ctx[None]], axis=0)
    mod = _ada(cond, w_ada, b_ada)
    mod = mod.reshape(depth, batch + 1, 6, 1, d)
    tm_all = n_all // 8
    tm_lat = n_lat // 8

    def mods(l, j):
        return mod[l, :, j]

    h = _norm_mod(*x_all, norm_g[0, 0], mods(0, 0), mods(0, 1), seq)
    for l in range(depth):
        last = l == depth - 1
        rows = n_lat if last else n_all
        tm = tm_lat if last else tm_all
        filt = (filt_w1[l], filt_b1[l], filt_w2[l], filt_b2[l], filt_w3[l], filt_b3[l],
                filt_freq[l], filt_w_out[l])

        u = _in_proj(h, w_in, l, _rope_tables_for_rows(seq, batch, rows - n_lat), rows, tm)
        if last:
            kv_ctx = _gmm(h[n_lat:], (w_in,), _dense_ids(batch * n_ctx, batch * n_ctx, l), k=d,
                          n=HY_OFF - K_OFF, tm=batch * n_ctx, tn=512, out_dtype=BF16, w_col_off=K_OFF,
                          name="ctx_kv_proj")
            att = _win_attn(u, kv_ctx, 0, 0, attn_sink[l], batch, seq, n_ctx, rows)
        else:
            att = _win_attn(u, u, n_lat // n_ctx, K_OFF, attn_sink[l], batch, seq, n_ctx, rows)
        hy = _hyena(u, conv_w, conv_b, filt, hyena_bias, l, seq, 0, out_rows=rows)
        if not last:
            att = _ctx_attn(u, att, attn_sink[l], batch, seq, n_ctx)
            hy = _hyena(u, conv_w, conv_b, filt, hyena_bias, l, n_ctx, n_lat // n_ctx, hy=hy)
        mrg = _merge(att, hy, u, w_attn_out, w_hyena_out, l, rows, ga_off, gh_off, tm)
        y = _gmm(mrg, (w_out,), _dense_ids(rows, tm, l), k=d, n=d, tm=tm, tn=512,
                 out_dtype=BF16, name="out_proj")
        router = None
        if l % 2 == 1:
            router = jnp.pad(moe_router[l // 2], ((0, 0), (0, LANES - N_EXPERTS)))
        res = _post(x_all, y, mods(l, 2), norm_g[l, 1], seq, rows,
                    nxt=(norm_g[l, 2], mods(l, 3), mods(l, 4)), router=router)
        x_all, h2 = res[0], res[1]

        if l % 2 == 0:
            i = l // 2
            dff = ffn_w1.shape[-1]
            act = _gmm(h2, (ffn_w1, ffn_w3), _dense_ids(rows, tm, i), k=d, n=dff, tm=tm, tn=256,
                       out_dtype=BF16, name="ffn_up")
            f = _gmm(act, (ffn_w2,), _dense_ids(rows, tm // 4, i), k=dff, n=d, tm=tm // 4, tn=512,
                     out_dtype=BF16, w_single_buffer=True, name="ffn_down")
        else:
            i = l // 2
            f = _moe(h2, res[2][:, :N_EXPERTS], moe_w1[i], moe_w3[i], moe_w2[i])
        if last:
            (x_all,) = _post(x_all, f, mods(l, 5), norm_g[l, 3], seq, rows)
        else:
            x_all, h = _post(x_all, f, mods(l, 5), norm_g[l, 3], seq, rows,
                             nxt=(norm_g[l + 1, 0], mods(l + 1, 0), mods(l + 1, 1)))
    return x_all[:n_lat].reshape(batch, seq, d)
```

```python
import functools
import math

import numpy as np
import jax
import jax.numpy as jnp
from jax import lax
from jax.experimental import pallas as pl
from jax.experimental.pallas import tpu as pltpu

F32 = jnp.float32
BF16 = jnp.bfloat16

GRID_W = 64
N_Q_HEADS = 16
N_KV_HEADS = 4
HEAD_DIM = 128
Q_GROUP = N_Q_HEADS // N_KV_HEADS
ATT_WIDTH = N_Q_HEADS * HEAD_DIM
KV_WIDTH = N_KV_HEADS * HEAD_DIM
BLOCK = 128
ROPE_BASE = 10000.0
ROPE_FREQS = HEAD_DIM // 4
MASK_VALUE = -1e30
HY_WIDTH = 2048
HY_ORDER = 2
SHORT_CONV = 3
FILTER_BANDS = 16
FILTER_HIDDEN = 64
DECAY_TARGET = 1e-2
MIN_DECAY = math.log(DECAY_TARGET) / 1.5
MAX_DECAY = math.log(DECAY_TARGET) / 0.3
K_OFF = ATT_WIDTH
V_OFF = K_OFF + KV_WIDTH
HY_OFF = V_OFF + KV_WIDTH
N_EXPERTS = 8
TOP_K = 2
MOE_ROWS = 512
RMS_EPS = 1e-6

LANES = 128
SUBLANES = 8
VMEM_LIMIT_BYTES = 56 * 1024 * 1024

FFT_NO = 128
FFT_NI = 64
HY_TC = 128


def _cparams(n_axes):
    return pltpu.CompilerParams(dimension_semantics=("arbitrary",) * n_axes,
                                vmem_limit_bytes=VMEM_LIMIT_BYTES)


ADA_CHUNK = 64


def _ada_kernel(c_ref, w_ref, b_ref, o_ref, act_ref):
    n_rows, k, _ = c_ref.shape
    tn = w_ref.shape[-1]

    @pl.when(jnp.logical_and(pl.program_id(0) == 0, pl.program_id(1) == 0))
    def _():
        c = c_ref[...]
        act_ref[...] = c * jax.nn.sigmoid(c)

    def body(i, accs):
        r0 = pl.multiple_of(i * ADA_CHUNK, ADA_CHUNK)
        w = w_ref[0, pl.ds(r0, ADA_CHUNK), :]
        out = []
        for r in range(n_rows):
            a = act_ref[r, pl.ds(r0, ADA_CHUNK), :]
            acc = accs[r]
            for s in range(ADA_CHUNK // SUBLANES):
                rows = slice(s * SUBLANES, (s + 1) * SUBLANES)
                acc = acc + w[rows] * jnp.concatenate([a[rows]] * (tn // LANES), axis=1)
            out.append(acc)
        return tuple(out)

    zero = jnp.zeros((SUBLANES, tn), F32)
    accs = lax.fori_loop(0, k // ADA_CHUNK, body, (zero,) * n_rows)
    bias = b_ref[0]
    for r in range(n_rows):
        o_ref[0, r:r + 1, :] = jnp.sum(accs[r], axis=0, keepdims=True) + bias


def _ada(cond, w_ada, b_ada, tn=512):
    n_layers, d, n = w_ada.shape
    rows = cond.shape[0]
    cb = jnp.broadcast_to(cond[:, :, None], (rows, d, LANES))
    return pl.pallas_call(
        _ada_kernel,
        grid=(n_layers, n // tn),
        in_specs=[pl.BlockSpec((rows, d, LANES), lambda l, j: (0, 0, 0)),
                  pl.BlockSpec((1, d, tn), lambda l, j: (l, 0, j)),
                  pl.BlockSpec((1, 1, tn), lambda l, j: (l, 0, j))],
        out_specs=pl.BlockSpec((1, rows, tn), lambda l, j: (l, 0, j)),
        out_shape=jax.ShapeDtypeStruct((n_layers, rows, n), F32),
        scratch_shapes=[pltpu.VMEM((rows, d, LANES), F32)],
        compiler_params=_cparams(2),
        name="ada_mod",
    )(cb, w_ada, b_ada.reshape(n_layers, 1, n))


def _rms(x):
    return x * lax.rsqrt(jnp.mean(x * x, axis=-1, keepdims=True) + RMS_EPS)


def _group_of_tile(tr, seq):
    return lambda i: (jnp.minimum((i * tr) // seq, 2), 0, 0)


def _two_source_specs(x_lat, x_ctx, tr):
    d = x_lat.shape[1]
    nl = x_lat.shape[0] // tr
    return [pl.BlockSpec((tr, d), lambda i: (jnp.minimum(i, nl - 1), 0)),
            pl.BlockSpec((tr, d), lambda i: (jnp.maximum(i - nl, 0), 0))], nl


def _two_source_rows(lat_ref, ctx_ref, n_lat_tiles):
    return jnp.where(pl.program_id(0) < n_lat_tiles, lat_ref[...], ctx_ref[...])


def _norm_mod2_kernel(lat_ref, ctx_ref, g_ref, sh_ref, sc_ref, o_ref, *, n_lat_tiles):
    y = _rms(_two_source_rows(lat_ref, ctx_ref, n_lat_tiles)) * g_ref[...]
    o_ref[...] = (y * (1.0 + sc_ref[0]) + sh_ref[0]).astype(o_ref.dtype)


def _norm_mod(x_lat, x_ctx, g, sh, sc, seq, tr=256):
    d = x_lat.shape[1]
    t = x_lat.shape[0] + x_ctx.shape[0]
    grp = _group_of_tile(tr, seq)
    x_specs, nl = _two_source_specs(x_lat, x_ctx, tr)
    return pl.pallas_call(
        functools.partial(_norm_mod2_kernel, n_lat_tiles=nl),
        grid=(t // tr,),
        in_specs=x_specs + [pl.BlockSpec((1, d), lambda i: (0, 0)),
                            pl.BlockSpec((1, 1, d), grp),
                            pl.BlockSpec((1, 1, d), grp)],
        out_specs=pl.BlockSpec((tr, d), lambda i: (i, 0)),
        out_shape=jax.ShapeDtypeStruct((t, d), BF16),
        compiler_params=_cparams(1),
        name="norm_mod",
    )(x_lat, x_ctx, g.reshape(1, d), sh, sc)


def _post_kernel(x_ref, *rest, with_next, with_router, with_pair, n_lat_tiles):
    if n_lat_tiles is None:
        x = x_ref[...]
    else:
        ctx_ref, *rest = rest
        x = _two_source_rows(x_ref, ctx_ref, n_lat_tiles)
    if with_pair:
        y0_ref, y1_ref, w_ref, gt_ref, gpost_ref, *rest = rest
        w = w_ref[...]
        y = w[:, 0:1] * y0_ref[...].astype(F32) + w[:, 1:2] * y1_ref[...].astype(F32)
    else:
        y_ref, gt_ref, gpost_ref, *rest = rest
        y = y_ref[...].astype(F32)
    xn = x + gt_ref[0] * (_rms(y) * gpost_ref[...])
    if not with_next:
        (xo_ref,) = rest
        xo_ref[...] = xn
        return
    if with_router:
        gpre_ref, sh_ref, sc_ref, wr_ref, xo_ref, ho_ref, lg_ref = rest
    else:
        gpre_ref, sh_ref, sc_ref, xo_ref, ho_ref = rest
    xo_ref[...] = xn
    h = (_rms(xn) * gpre_ref[...]) * (1.0 + sc_ref[0]) + sh_ref[0]
    ho_ref[...] = h.astype(ho_ref.dtype)
    if with_router:
        lg_ref[...] = jnp.dot(h, wr_ref[...], preferred_element_type=F32,
                              precision=lax.Precision.HIGHEST)


def _post(x_all, y, gt, g_post, seq, rows, nxt=None, router=None, tr=256):
    grp = _group_of_tile(tr, seq)
    if isinstance(x_all, tuple):
        d = x_all[0].shape[1]
        x_specs, n_lat_tiles = _two_source_specs(*x_all, tr)
        x_args = list(x_all)
    else:
        d = x_all.shape[1]
        x_specs, n_lat_tiles = [pl.BlockSpec((tr, d), lambda i: (i, 0))], None
        x_args = [x_all]
    row = pl.BlockSpec((tr, d), lambda i: (i, 0))
    vec = pl.BlockSpec((1, d), lambda i: (0, 0))
    mod = pl.BlockSpec((1, 1, d), grp)
    with_pair = isinstance(y, tuple)
    if with_pair:
        y0, y1, w = y
        in_specs = x_specs + [row, row, pl.BlockSpec((tr, w.shape[1]), lambda i: (i, 0)), mod, vec]
        args = x_args + [y0, y1, w, gt, g_post.reshape(1, d)]
    else:
        in_specs = x_specs + [row, mod, vec]
        args = x_args + [y, gt, g_post.reshape(1, d)]
    out_specs = [row]
    out_shape = [jax.ShapeDtypeStruct((rows, d), F32)]
    if nxt is not None:
        g_pre, sh, sc = nxt
        in_specs += [vec, mod, mod]
        args += [g_pre.reshape(1, d), sh, sc]
        out_specs.append(row)
        out_shape.append(jax.ShapeDtypeStruct((rows, d), BF16))
        if router is not None:
            in_specs.append(pl.BlockSpec((d, LANES), lambda i: (0, 0)))
            args.append(router)
            out_specs.append(pl.BlockSpec((tr, LANES), lambda i: (i, 0)))
            out_shape.append(jax.ShapeDtypeStruct((rows, LANES), F32))
    return pl.pallas_call(
        functools.partial(_post_kernel, with_next=nxt is not None, with_router=router is not None,
                          with_pair=with_pair, n_lat_tiles=n_lat_tiles),
        grid=(rows // tr,),
        in_specs=in_specs,
        out_specs=out_specs,
        out_shape=out_shape,
        compiler_params=_cparams(1),
        name="post_norm",
    )(*args)


def _weight_changed(be_ref, m):
    return jnp.logical_or(m == 0, be_ref[m] != be_ref[jnp.maximum(m - 1, 0)])


A_STREAMS = 2


def _column_slabs(refs):
    return refs[0][...] if len(refs) == 1 else jnp.concatenate([r[...] for r in refs], axis=1)


def _gmm_kernel(be_ref, nv_ref, *refs):
    *a_refs, w_ref, o_ref, wb_ref = refs
    m = pl.program_id(1)

    @pl.when(m < nv_ref[0])
    def _():
        @pl.when(_weight_changed(be_ref, m))
        def _():
            wb_ref[...] = w_ref[0].astype(BF16)

        o_ref[...] = jnp.dot(_column_slabs(a_refs), wb_ref[...],
                             preferred_element_type=F32).astype(o_ref.dtype)


def _swiglu_kernel(be_ref, nv_ref, *refs):
    *a_refs, w1_ref, w3_ref, o_ref, w1b_ref, w3b_ref = refs
    m = pl.program_id(1)

    @pl.when(m < nv_ref[0])
    def _():
        @pl.when(_weight_changed(be_ref, m))
        def _():
            w1b_ref[...] = w1_ref[0].astype(BF16)
            w3b_ref[...] = w3_ref[0].astype(BF16)

        a = _column_slabs(a_refs)
        g = jnp.dot(a, w1b_ref[...], preferred_element_type=F32)
        u = jnp.dot(a, w3b_ref[...], preferred_element_type=F32)
        o_ref[...] = (g * jax.nn.sigmoid(g) * u).astype(o_ref.dtype)


def _slab_specs(tm, k, row_of):
    assert k % (A_STREAMS * LANES) == 0
    slab = lambda s: pl.BlockSpec((tm, k // A_STREAMS), lambda *g: (row_of(*g), s))
    return [slab(s) for s in range(A_STREAMS)]


def _gmm(a, ws, ids, *, k, n, tm, tn, out_dtype, w_col_off=0, rows=None, w_single_buffer=False,
         name="gmm"):
    rows = a.shape[0] if rows is None else rows
    assert rows % tm == 0 and n % tn == 0 and w_col_off % tn == 0
    off = w_col_off // tn
    w_mode = dict(pipeline_mode=pl.Buffered(1)) if w_single_buffer else {}

    def a_row(j, m, be_ref, nv_ref):
        return jnp.minimum(m, nv_ref[0] - 1)

    def w_map(j, m, be_ref, nv_ref):
        return (be_ref[jnp.minimum(m, nv_ref[0] - 1)], 0, j + off)

    def o_map(j, m, be_ref, nv_ref):
        return (m, j)

    kernel = _gmm_kernel if len(ws) == 1 else _swiglu_kernel
    return pl.pallas_call(
        kernel,
        grid_spec=pltpu.PrefetchScalarGridSpec(
            num_scalar_prefetch=2,
            grid=(n // tn, rows // tm),
            in_specs=_slab_specs(tm, k, a_row) + [pl.BlockSpec((1, k, tn), w_map, **w_mode)] * len(ws),
            out_specs=pl.BlockSpec((tm, tn), o_map),
            scratch_shapes=[pltpu.VMEM((k, tn), BF16)] * len(ws)),
        out_shape=jax.ShapeDtypeStruct((rows, n), out_dtype),
        compiler_params=_cparams(2),
        name=name,
    )(*ids, *([a] * A_STREAMS), *ws)


def _dense_ids(rows, tm, idx):
    nb = rows // tm
    return jnp.full((nb,), idx, jnp.int32), jnp.full((1,), nb, jnp.int32)


def _merge_kernel(att_ref, hy_ref, ga_ref, gh_ref, wa_ref, wh_ref, o_ref, wab_ref, whb_ref):
    @pl.when(pl.program_id(1) == 0)
    def _():
        wab_ref[...] = wa_ref[0].astype(BF16)
        whb_ref[...] = wh_ref[0].astype(BF16)

    pa = jnp.dot(att_ref[...], wab_ref[...], preferred_element_type=F32)
    ph = jnp.dot(hy_ref[...], whb_ref[...], preferred_element_type=F32)
    ga = jax.nn.sigmoid(ga_ref[...].astype(F32))
    gh = jax.nn.sigmoid(gh_ref[...].astype(F32))
    o_ref[...] = (ga * pa + gh * ph).astype(o_ref.dtype)


def _merge(att, hy, u, w_ao, w_ho, layer, rows, ga_off, gh_off, tm, tn=512):
    ka, d = w_ao.shape[1:]
    kh = w_ho.shape[1]
    return pl.pallas_call(
        _merge_kernel,
        grid=(d // tn, rows // tm),
        in_specs=[pl.BlockSpec((tm, ka), lambda j, m: (m, 0)),
                  pl.BlockSpec((tm, kh), lambda j, m: (m, 0)),
                  pl.BlockSpec((tm, tn), lambda j, m: (m, ga_off // tn + j)),
                  pl.BlockSpec((tm, tn), lambda j, m: (m, gh_off // tn + j)),
                  pl.BlockSpec((1, ka, tn), lambda j, m: (layer, 0, j)),
                  pl.BlockSpec((1, kh, tn), lambda j, m: (layer, 0, j))],
        out_specs=pl.BlockSpec((tm, tn), lambda j, m: (m, j)),
        out_shape=jax.ShapeDtypeStruct((rows, d), BF16),
        scratch_shapes=[pltpu.VMEM((ka, tn), BF16), pltpu.VMEM((kh, tn), BF16)],
        compiler_params=_cparams(2),
        name="gated_merge",
    )(att, hy, u, u, w_ao, w_ho)


def _rope_tables(seq):
    rows = seq // GRID_W
    row = jnp.repeat(jnp.arange(rows), GRID_W).astype(F32)
    col = jnp.tile(jnp.arange(GRID_W), rows).astype(F32)
    inv = ROPE_BASE ** (-jnp.arange(ROPE_FREQS, dtype=F32) / ROPE_FREQS)
    ang = jnp.stack([row[:, None] * inv, col[:, None] * inv], axis=1)
    cos, sin = jnp.cos(ang), jnp.sin(ang)
    zero = jnp.zeros_like(sin)
    cos_t = jnp.stack([cos, cos], axis=2).reshape(seq, HEAD_DIM)
    s_lo = jnp.stack([-sin, zero], axis=2).reshape(seq, HEAD_DIM)
    s_hi = jnp.stack([zero, sin], axis=2).reshape(seq, HEAD_DIM)
    return cos_t, s_lo, s_hi


def _rope_tables_for_rows(seq, batch, n_extra):
    cos_t, s_lo, s_hi = (jnp.tile(t, (batch, 1)) for t in _rope_tables(seq))
    if n_extra:
        cos_t = jnp.concatenate([cos_t, jnp.ones((n_extra, HEAD_DIM), F32)], axis=0)
        pad = jnp.zeros((n_extra, HEAD_DIM), F32)
        s_lo, s_hi = jnp.concatenate([s_lo, pad], axis=0), jnp.concatenate([s_hi, pad], axis=0)
    return cos_t, s_lo, s_hi


def _in_proj_kernel(*refs, n_rope_tiles):
    *a_refs, w_ref, c_ref, lo_ref, hi_ref, o_ref, wb_ref = refs
    j = pl.program_id(0)

    @pl.when(pl.program_id(1) == 0)
    def _():
        wb_ref[...] = w_ref[0].astype(BF16)

    acc = jnp.dot(_column_slabs(a_refs), wb_ref[...], preferred_element_type=F32)

    @pl.when(j < n_rope_tiles)
    def _():
        c, lo, hi = c_ref[...], lo_ref[...], hi_ref[...]
        for hd in range(acc.shape[1] // HEAD_DIM):
            sl = slice(hd * HEAD_DIM, (hd + 1) * HEAD_DIM)
            x = acc[:, sl]
            up = pltpu.roll(x, HEAD_DIM - ROPE_FREQS, 1)
            dn = pltpu.roll(x, ROPE_FREQS, 1)
            o_ref[:, sl] = (x * c + up * lo + dn * hi).astype(o_ref.dtype)

    @pl.when(j >= n_rope_tiles)
    def _():
        o_ref[...] = acc.astype(o_ref.dtype)


def _in_proj(h, w_in, layer, tables, rows, tm, tn=512):
    k, n = w_in.shape[1:]
    assert rows % tm == 0 and n % tn == 0 and V_OFF % tn == 0
    n_rope_tiles = V_OFF // tn
    tab = pl.BlockSpec((tm, HEAD_DIM), lambda j, m: (jnp.where(j < n_rope_tiles, m, 0), 0))
    return pl.pallas_call(
        functools.partial(_in_proj_kernel, n_rope_tiles=n_rope_tiles),
        grid=(n // tn, rows // tm),
        in_specs=_slab_specs(tm, k, lambda j, m: m) + [
            pl.BlockSpec((1, k, tn), lambda j, m: (layer, 0, j)), tab, tab, tab],
        out_specs=pl.BlockSpec((tm, tn), lambda j, m: (m, j)),
        out_shape=jax.ShapeDtypeStruct((rows, n), BF16),
        scratch_shapes=[pltpu.VMEM((k, tn), BF16)],
        compiler_params=_cparams(2),
        name="in_proj",
    )(*([h] * A_STREAMS), w_in, *tables)


ATTN_ROW_CHUNK = 32
LOG2_E = math.log2(math.e)


def _scores(q, k):
    return lax.dot_general(q, k, (((1,), (1,)), ((), ())), preferred_element_type=F32)


def _win_attn_kernel(sink_ref, q_ref, kp_ref, kc_ref, kn_ref, vp_ref, vc_ref, vn_ref,
                     kx_ref, vx_ref, o_ref, s_ref, p_ref, den_ref):
    n = pl.program_id(1)
    nb = pl.num_programs(1)
    scale = HEAD_DIM ** -0.5
    n_keys = 3 * BLOCK + kx_ref.shape[0]
    qi = lax.broadcasted_iota(jnp.int32, (BLOCK, n_keys), 0)
    kj = lax.broadcasted_iota(jnp.int32, (BLOCK, n_keys), 1)
    bad_prev = jnp.logical_and(kj < BLOCK, jnp.logical_or(kj < qi, n == 0))
    bad_next = jnp.logical_and(jnp.logical_and(kj >= 2 * BLOCK, kj < 3 * BLOCK),
                               jnp.logical_or(kj - 2 * BLOCK > qi, n == nb - 1))
    valid = jnp.logical_not(jnp.logical_or(bad_prev, bad_next))
    head_lanes = lambda head: slice(head * HEAD_DIM, (head + 1) * HEAD_DIM)
    for h in range(N_KV_HEADS):
        hs = head_lanes(h)
        keys = jnp.concatenate([kp_ref[:, hs], kc_ref[:, hs], kn_ref[:, hs], kx_ref[:, hs]], axis=0)
        for g in range(Q_GROUP):
            head = h * Q_GROUP + g
            s_ref[head] = jnp.where(valid, _scores(q_ref[:, head_lanes(head)], keys) * (scale * LOG2_E),
                                    MASK_VALUE)
    for head in range(N_Q_HEADS):
        sink = sink_ref[head // Q_GROUP, head % Q_GROUP] * LOG2_E
        for r0 in range(0, BLOCK, ATTN_ROW_CHUNK):
            rows = slice(r0, r0 + ATTN_ROW_CHUNK)
            s = s_ref[head, rows, :]
            mx = jnp.maximum(jnp.max(s, axis=-1, keepdims=True), sink)
            p = jnp.exp2(s - mx)
            den_ref[head, rows, :] = jnp.sum(p, axis=-1, keepdims=True) + jnp.exp2(sink - mx)
            p_ref[head, rows, :] = p.astype(BF16)
    for h in range(N_KV_HEADS):
        hs = head_lanes(h)
        vals = jnp.concatenate([vp_ref[:, hs], vc_ref[:, hs], vn_ref[:, hs], vx_ref[:, hs]], axis=0)
        for g in range(Q_GROUP):
            head = h * Q_GROUP + g
            o = jnp.dot(p_ref[head], vals, preferred_element_type=F32)
            o_ref[:, head_lanes(head)] = (o / den_ref[head]).astype(o_ref.dtype)


def _win_attn(u, kv_ctx, ctx_row_blk, ctx_k_col, sink, batch, seq, n_ctx, out_rows):
    nb = seq // BLOCK
    assert K_OFF % KV_WIDTH == 0 and V_OFF % KV_WIDTH == 0 and ctx_k_col % KV_WIDTH == 0
    kcol = K_OFF // KV_WIDTH
    vcol = V_OFF // KV_WIDTH
    xk = ctx_k_col // KV_WIDTH

    def blk(shift, col):
        def index(b, n):
            return (b * nb + jnp.clip(n + shift, 0, nb - 1), col)
        return pl.BlockSpec((BLOCK, KV_WIDTH), index)

    return pl.pallas_call(
        _win_attn_kernel,
        grid=(batch, nb),
        in_specs=[pl.BlockSpec(memory_space=pltpu.SMEM),
                  pl.BlockSpec((BLOCK, ATT_WIDTH), lambda b, n: (b * nb + n, 0)),
                  blk(-1, kcol), blk(0, kcol), blk(1, kcol),
                  blk(-1, vcol), blk(0, vcol), blk(1, vcol),
                  pl.BlockSpec((n_ctx, KV_WIDTH), lambda b, n: (ctx_row_blk + b, xk)),
                  pl.BlockSpec((n_ctx, KV_WIDTH), lambda b, n: (ctx_row_blk + b, xk + 1))],
        out_specs=pl.BlockSpec((BLOCK, ATT_WIDTH), lambda b, n: (b * nb + n, 0)),
        out_shape=jax.ShapeDtypeStruct((out_rows, ATT_WIDTH), BF16),
        scratch_shapes=[pltpu.VMEM((N_Q_HEADS, BLOCK, 3 * BLOCK + n_ctx), F32),
                        pltpu.VMEM((N_Q_HEADS, BLOCK, 3 * BLOCK + n_ctx), BF16),
                        pltpu.VMEM((N_Q_HEADS, BLOCK, 1), F32)],
        compiler_params=_cparams(2),
        name="window_attention",
    )(sink.reshape(N_KV_HEADS, Q_GROUP), u, u, u, u, u, u, u, kv_ctx, kv_ctx)


def _ctx_attn_kernel(sink_ref, q_ref, k_ref, v_ref, att_ref, o_ref):
    del att_ref
    h = pl.program_id(1)
    scale = HEAD_DIM ** -0.5
    k, v = k_ref[...], v_ref[...]
    for g in range(Q_GROUP):
        sl = slice(g * HEAD_DIM, (g + 1) * HEAD_DIM)
        s = _scores(q_ref[:, sl], k) * scale
        sink = sink_ref[h, g]
        mx = jnp.maximum(jnp.max(s, axis=-1, keepdims=True), sink)
        p = jnp.exp(s - mx)
        den = jnp.sum(p, axis=-1, keepdims=True) + jnp.exp(sink - mx)
        o = jnp.dot(p.astype(BF16), v, preferred_element_type=F32)
        o_ref[:, sl] = (o / den).astype(o_ref.dtype)


def _ctx_attn(u, att, sink, batch, seq, n_ctx):
    hd = HEAD_DIM
    qw = Q_GROUP * hd
    ctx_blk = (batch * seq) // n_ctx
    return pl.pallas_call(
        _ctx_attn_kernel,
        grid=(batch, N_KV_HEADS),
        in_specs=[pl.BlockSpec(memory_space=pltpu.SMEM),
                  pl.BlockSpec((n_ctx, qw), lambda b, h: (ctx_blk + b, h)),
                  pl.BlockSpec((n_ctx, hd), lambda b, h: (ctx_blk + b, K_OFF // hd + h)),
                  pl.BlockSpec((n_ctx, hd), lambda b, h: (ctx_blk + b, V_OFF // hd + h)),
                  pl.BlockSpec(memory_space=pl.ANY)],
        out_specs=pl.BlockSpec((n_ctx, qw), lambda b, h: (ctx_blk + b, h)),
        out_shape=jax.ShapeDtypeStruct(att.shape, att.dtype),
        input_output_aliases={4: 0},
        compiler_params=_cparams(2),
        name="context_attention",
    )(sink.reshape(N_KV_HEADS, Q_GROUP), u, u, u, att)


def _short_conv_kernel(u_ref, w_ref, b_ref, o_ref):
    x = u_ref[...].astype(F32)
    n = x.shape[0]
    r = lax.broadcasted_iota(jnp.int32, x.shape, 0)
    prev = jnp.where(r == 0, 0.0, pltpu.roll(x, 1, 0))
    nxt = jnp.where(r == n - 1, 0.0, pltpu.roll(x, n - 1, 0))
    w = w_ref[0]
    o_ref[...] = (prev * w[0:1] + x * w[1:2] + nxt * w[2:3] + b_ref[0]).astype(o_ref.dtype)


def _short_conv(u, conv_w, conv_b, layer, n_seq, seg, row_blk_off, tw=256):
    width = conv_w.shape[-1]
    cb = conv_b.reshape(conv_b.shape[0], 1, width)
    return pl.pallas_call(
        _short_conv_kernel,
        grid=(n_seq, width // tw),
        in_specs=[pl.BlockSpec((seg, tw), lambda s, j: (row_blk_off + s, HY_OFF // tw + j)),
                  pl.BlockSpec((1, SHORT_CONV, tw), lambda s, j: (layer, 0, j)),
                  pl.BlockSpec((1, 1, tw), lambda s, j: (layer, 0, j))],
        out_specs=pl.BlockSpec((seg, tw), lambda s, j: (s, j)),
        out_shape=jax.ShapeDtypeStruct((n_seq * seg, width), BF16),
        compiler_params=_cparams(2),
        name="short_conv",
    )(u, conv_w, cb)


def _filter_positions(n, n_fft):
    t = jnp.linspace(0.0, 1.0, n, dtype=F32)[:, None]
    w = 2.0 * math.pi * jnp.arange(n, dtype=F32)[:, None] / n
    bands = jnp.linspace(1e-4, FILTER_BANDS - 1, FILTER_BANDS, dtype=F32)[None, :]
    z = jnp.concatenate([t, jnp.cos(bands * w), -jnp.sin(bands * w)], axis=-1)
    zt = jnp.concatenate([z, t], axis=-1)
    mid = jnp.zeros((n_fft - 2 * n + 1, zt.shape[1]), F32)
    full = jnp.concatenate([zt, mid, zt[1:][::-1]], axis=0)
    feat = jnp.pad(full[:, :-1], ((0, 0), (0, FILTER_HIDDEN - (zt.shape[1] - 1))))
    return feat, full[:, -1:]


def _filter_kernel(z_ref, t_ref, w1_ref, b1_ref, w2_ref, b2_ref, w3_ref, b3_ref, fr_ref,
                   wo_ref, dl_ref, o_ref, hid_ref, *, n, n_fft):
    hi = lax.Precision.HIGHEST
    tr = z_ref.shape[0]

    @pl.when(pl.program_id(1) == 0)
    def _():
        fr = fr_ref[...]
        h = jnp.sin(fr * (jnp.dot(z_ref[...], w1_ref[...], preferred_element_type=F32, precision=hi)
                          + b1_ref[...]))
        h = jnp.sin(fr * (jnp.dot(h, w2_ref[...], preferred_element_type=F32, precision=hi)
                          + b2_ref[...]))
        h = jnp.sin(fr * (jnp.dot(h, w3_ref[...], preferred_element_type=F32, precision=hi)
                          + b3_ref[...]))
        h_hi = h.astype(BF16)
        hid_ref[0] = h_hi
        hid_ref[1] = (h - h_hi.astype(F32)).astype(BF16)

    h_hi, h_lo = hid_ref[0], hid_ref[1]
    row = pl.program_id(0) * tr + lax.broadcasted_iota(jnp.int32, (tr, 1), 0)
    live = jnp.logical_or(row < n, row > n_fft - n)
    decay = jnp.where(live, jnp.exp(-t_ref[...] * dl_ref[...]), 0.0)
    for o in range(HY_ORDER):
        w = wo_ref[o, 0]
        w_hi = w.astype(BF16)
        w_lo = (w - w_hi.astype(F32)).astype(BF16)
        acc = (jnp.dot(h_hi, w_hi, preferred_element_type=F32)
               + jnp.dot(h_lo, w_hi, preferred_element_type=F32)
               + jnp.dot(h_hi, w_lo, preferred_element_type=F32))
        o_ref[o] = acc * decay


def _hyena_filter(n, n_fft, w1, b1, w2, b2, w3, b3, freq, w_out, tw=512):
    tr = min(512, n)
    assert n % tr == 0 and n_fft % tr == 0
    feat, tpos = _filter_positions(n, n_fft)
    hid = FILTER_HIDDEN
    w1p = jnp.pad(w1, ((0, hid - w1.shape[0]), (0, 0)))
    wo = w_out.reshape(hid, HY_ORDER, 2, HY_WIDTH).transpose(1, 2, 0, 3)
    deltas = jnp.abs(jnp.linspace(MIN_DECAY, MAX_DECAY, HY_WIDTH, dtype=F32)).reshape(1, HY_WIDTH)
    small = lambda shape: pl.BlockSpec(shape, lambda r, j: (0,) * len(shape))
    return pl.pallas_call(
        functools.partial(_filter_kernel, n=n, n_fft=n_fft),
        grid=(n_fft // tr, HY_WIDTH // tw),
        in_specs=[pl.BlockSpec((tr, hid), lambda r, j: (r, 0)),
                  pl.BlockSpec((tr, 1), lambda r, j: (r, 0)),
                  small((hid, hid)), small((1, hid)), small((hid, hid)), small((1, hid)),
                  small((hid, hid)), small((1, hid)), small((1, hid)),
                  pl.BlockSpec((HY_ORDER, 1, hid, tw), lambda r, j: (0, jnp.where(r * tr >= n, 1, 0), 0, j)),
                  pl.BlockSpec((1, tw), lambda r, j: (0, j))],
        out_specs=pl.BlockSpec((HY_ORDER, tr, tw), lambda r, j: (0, r, j)),
        out_shape=jax.ShapeDtypeStruct((HY_ORDER, n_fft, HY_WIDTH), F32),
        scratch_shapes=[pltpu.VMEM((2, tr, hid), BF16)],
        compiler_params=_cparams(2),
        name="hyena_filter",
    )(feat, tpos, w1p, b1.reshape(1, hid), w2, b2.reshape(1, hid), w3, b3.reshape(1, hid),
      freq.reshape(1, hid), wo, deltas)


@functools.lru_cache(maxsize=None)
def _dft_constants(n_fft, no, ni):
    jo = np.arange(no)
    k1 = np.arange(no)
    half = no // 2
    f1r, f1d, f1i = [], [], []
    for i in range(ni):
        ang = 2.0 * np.pi * np.outer(k1, jo * ni + i) / n_fft
        c, s = np.cos(ang), np.sin(ang)
        f1r.append(np.concatenate([c, -s], axis=0))
        ch, sh = c[:, :half], s[:, :half]
        f1d.append(np.block([[ch, sh], [-sh, ch]]))
        angi = 2.0 * np.pi * np.outer(np.arange(half) * ni + i, k1) / n_fft
        ci, si = np.cos(angi), np.sin(angi)
        f1i.append(np.block([[ci, -si], [si, ci]]))
    ang3 = 2.0 * np.pi * np.outer(np.arange(ni), np.arange(ni)) / ni
    c3, s3 = np.cos(ang3), np.sin(ang3)
    m3 = np.block([[c3, s3], [-s3, c3]])
    m3c = np.block([[c3, -s3], [s3, c3]])
    as_bf16 = lambda a: np.asarray(a, np.float32).astype(BF16)
    return (as_bf16(np.stack(f1r)), as_bf16(np.stack(f1d)), as_bf16(np.stack(f1i)),
            as_bf16(m3), as_bf16(m3c))


def _const_spec(shape, n_axes):
    zeros = (0,) * len(shape)
    if n_axes == 2:
        index = lambda a, b: zeros
    else:
        index = lambda a: zeros
    return pl.BlockSpec(shape, index, pipeline_mode=pl.Buffered(1))


DFT_UNROLL = 32


def _pitch(size):
    return size + SUBLANES


def _block_at(i, size):
    return pl.ds(pl.multiple_of(i * _pitch(size), SUBLANES), size)


def _dft_stage3_rhs(a_ref, k1, no, ni):
    re = a_ref[pl.ds(k1, ni, stride=_pitch(2 * no)), :]
    im = a_ref[pl.ds(no + k1, ni, stride=_pitch(2 * no)), :]
    return jnp.concatenate([re, im], axis=0).astype(BF16)


def _dft_stage3_rhs_pair(a_ref, pair, no, ni):
    return jnp.concatenate([_dft_stage3_rhs(a_ref, 2 * pair, no, ni),
                            _dft_stage3_rhs(a_ref, 2 * pair + 1, no, ni)], axis=1)


def _spectrum_kernel(h_ref, f1r_ref, m3_ref, o_ref, a_ref, *, no, ni):
    inv_n = 1.0 / (no * ni)
    tc = h_ref.shape[-1]

    def stage1(ji, carry):
        rhs = h_ref[0, pl.ds(ji, no, stride=ni), :].astype(BF16)
        a_ref[_block_at(ji, 2 * no), :] = jnp.dot(f1r_ref[ji], rhs, preferred_element_type=F32)
        return carry

    lax.fori_loop(0, ni, stage1, 0, unroll=DFT_UNROLL)

    def stage3(pair, carry):
        x = jnp.dot(m3_ref[...], _dft_stage3_rhs_pair(a_ref, pair, no, ni),
                    preferred_element_type=F32) * inv_n
        r0 = pl.multiple_of(pair * 2 * ni, 2 * ni)
        for i in range(2):
            lanes = slice(i * tc, (i + 1) * tc)
            o_ref[0, 0, pl.ds(r0 + i * ni, ni), :] = x[:ni, lanes].astype(o_ref.dtype)
            o_ref[0, 1, pl.ds(r0 + i * ni, ni), :] = x[ni:, lanes].astype(o_ref.dtype)
        return carry

    lax.fori_loop(0, no // 2, stage3, 0, unroll=DFT_UNROLL // 2)


def _filter_spectrum(hfull, no=FFT_NO, ni=FFT_NI, tc=HY_TC):
    n_ord, n_fft, width = hfull.shape
    f1r, _, _, m3, _ = _dft_constants(n_fft, no, ni)
    return pl.pallas_call(
        functools.partial(_spectrum_kernel, no=no, ni=ni),
        grid=(n_ord, width // tc),
        in_specs=[pl.BlockSpec((1, n_fft, tc), lambda o, c: (o, 0, c)),
                  _const_spec(f1r.shape, 2), _const_spec(m3.shape, 2)],
        out_specs=pl.BlockSpec((1, 2, n_fft, tc), lambda o, c: (o, 0, 0, c)),
        out_shape=jax.ShapeDtypeStruct((n_ord, 2, n_fft, width), BF16),
        scratch_shapes=[pltpu.VMEM((ni * _pitch(2 * no), tc), F32)],
        compiler_params=_cparams(2),
        name="hyena_spectrum",
    )(hfull, f1r, m3)


def _long_conv_kernel(v_ref, g_ref, spec_ref, bias_ref, f1d_ref, f1i_ref, m3_ref, m3c_ref,
                      o_ref, z_ref, a_ref, b_ref, *, n, no, ni):
    order = pl.program_id(1)
    half = no // 2
    assert n == half * ni

    @pl.when(order == 0)
    def _():
        def load(jo, carry):
            r0 = pl.multiple_of(jo * ni, ni)
            for b in range(2):
                z_ref[b, _block_at(jo, ni), :] = v_ref[pl.ds(b * n + r0, ni), :].astype(F32)
            return carry

        lax.fori_loop(0, half, load, 0, unroll=DFT_UNROLL)

    def stage1(ji, carry):
        zr = z_ref[0, pl.ds(ji, half, stride=_pitch(ni)), :]
        zi = z_ref[1, pl.ds(ji, half, stride=_pitch(ni)), :]
        rhs = jnp.concatenate([zr, zi], axis=0).astype(BF16)
        a_ref[_block_at(ji, 2 * no), :] = jnp.dot(f1d_ref[ji], rhs, preferred_element_type=F32)
        return carry

    lax.fori_loop(0, ni, stage1, 0, unroll=DFT_UNROLL)

    def stage3(pair, carry):
        x = jnp.dot(m3_ref[...], _dft_stage3_rhs_pair(a_ref, pair, no, ni), preferred_element_type=F32)
        r0 = pl.multiple_of(pair * 2 * ni, 2 * ni)
        side_by_side = lambda s: jnp.concatenate([s[:ni], s[ni:]], axis=1).astype(F32)
        hr = side_by_side(spec_ref[0, 0, pl.ds(r0, 2 * ni), :])
        hi = side_by_side(spec_ref[0, 1, pl.ds(r0, 2 * ni), :])
        xr, xi = x[:ni], x[ni:]
        y = jnp.concatenate([xr * hr - xi * hi, xr * hi + xi * hr], axis=0).astype(BF16)
        b = jnp.dot(m3c_ref[...], y, preferred_element_type=F32)
        tc = b.shape[1] // 2
        b_ref[_block_at(2 * pair, 2 * ni), :] = b[:, :tc]
        b_ref[_block_at(2 * pair + 1, 2 * ni), :] = b[:, tc:]
        return carry

    lax.fori_loop(0, no // 2, stage3, 0, unroll=DFT_UNROLL // 2)

    def stage1_inv(t2, carry):
        br = b_ref[pl.ds(t2, no, stride=_pitch(2 * ni)), :]
        bi = b_ref[pl.ds(ni + t2, no, stride=_pitch(2 * ni)), :]
        rhs = jnp.concatenate([br, bi], axis=0).astype(BF16)
        a_ref[pl.ds(pl.multiple_of(t2 * _pitch(2 * no), SUBLANES), no), :] = jnp.dot(
            f1i_ref[t2], rhs, preferred_element_type=F32)
        return carry

    lax.fori_loop(0, ni, stage1_inv, 0, unroll=DFT_UNROLL)

    bias = bias_ref[0, 0]

    def finish(t1, carry):
        r0 = pl.multiple_of(t1 * ni, ni)
        conv = (a_ref[pl.ds(t1, ni, stride=_pitch(2 * no)), :],
                a_ref[pl.ds(half + t1, ni, stride=_pitch(2 * no)), :])
        for b in range(2):
            z = z_ref[b, _block_at(t1, ni), :]
            gate = g_ref[pl.ds(b * n + r0, ni), :].astype(F32)
            zn = gate * (conv[b] + z * bias)
            z_ref[b, _block_at(t1, ni), :] = zn
            o_ref[pl.ds(b * n + r0, ni), :] = zn.astype(o_ref.dtype)
        return carry

    lax.fori_loop(0, half, finish, 0, unroll=DFT_UNROLL)


def _long_conv(uc, spec, hy_bias, layer, n, out_rows, no=FFT_NO, ni=FFT_NI, tc=HY_TC):
    n_fft = no * ni
    assert 2 * n == n_fft
    _, f1d, f1i, m3, m3c = _dft_constants(n_fft, no, ni)
    nct = HY_WIDTH // tc
    bias = hy_bias.reshape(hy_bias.shape[0], HY_ORDER, 1, HY_WIDTH)
    return pl.pallas_call(
        functools.partial(_long_conv_kernel, n=n, no=no, ni=ni),
        grid=(nct, HY_ORDER),
        in_specs=[pl.BlockSpec((2 * n, tc), lambda c, o: (0, c)),
                  pl.BlockSpec((2 * n, tc), lambda c, o: (0, (1 + o) * nct + c)),
                  pl.BlockSpec((1, 2, n_fft, tc), lambda c, o: (o, 0, 0, c)),
                  pl.BlockSpec((1, 1, 1, tc), lambda c, o: (layer, o, 0, c)),
                  _const_spec(f1d.shape, 2), _const_spec(f1i.shape, 2),
                  _const_spec(m3.shape, 2), _const_spec(m3c.shape, 2)],
        out_specs=pl.BlockSpec((2 * n, tc), lambda c, o: (0, c)),
        out_shape=jax.ShapeDtypeStruct((out_rows, HY_WIDTH), BF16),
        scratch_shapes=[pltpu.VMEM((2, (no // 2) * _pitch(ni), tc), F32),
                        pltpu.VMEM((ni * _pitch(2 * no), tc), F32),
                        pltpu.VMEM((no * _pitch(2 * ni), tc), F32)],
        compiler_params=_cparams(2),
        name="hyena_long_conv",
    )(uc, uc, spec, bias, f1d, f1i, m3, m3c)


@functools.lru_cache(maxsize=None)
def _small_dft_constants(n):
    n_fft = 2 * n
    k = np.arange(n_fft)
    ang = 2.0 * np.pi * np.outer(k, np.arange(n_fft)) / n_fft
    c, s = np.cos(ang), np.sin(ang)
    fr = np.concatenate([c, -s], axis=0)
    ch, sh = c[:, :n], s[:, :n]
    fd = np.block([[ch, sh], [-sh, ch]])
    ci, si = c[:n, :], s[:n, :]
    fi = np.block([[ci, -si], [si, ci]])
    as_bf16 = lambda a: np.asarray(a, np.float32).astype(BF16)
    return as_bf16(fr), as_bf16(fd), as_bf16(fi)


def _small_conv_kernel(v_ref, g1_ref, g2_ref, h_ref, bias_ref, fr_ref, fd_ref, fi_ref, hy_ref, o_ref,
                       *, n):
    del hy_ref
    n_fft = 2 * n
    z = v_ref[...].astype(F32)
    for o, g_ref in enumerate((g1_ref, g2_ref)):
        hs = jnp.dot(fr_ref[...], h_ref[o].astype(BF16), preferred_element_type=F32) * (1.0 / n_fft)
        x = jnp.dot(fd_ref[...], z.astype(BF16), preferred_element_type=F32)
        xr, xi, hr, hi = x[:n_fft], x[n_fft:], hs[:n_fft], hs[n_fft:]
        y = jnp.concatenate([xr * hr - xi * hi, xr * hi + xi * hr], axis=0).astype(BF16)
        conv = jnp.dot(fi_ref[...], y, preferred_element_type=F32)
        z = g_ref[...].astype(F32) * (conv + z * bias_ref[0, o])
    o_ref[...] = z.astype(o_ref.dtype)


def _small_conv(uc, hfull, hy_bias, layer, n, hy, row_blk, tc=256):
    fr, fd, fi = _small_dft_constants(n)
    nct = HY_WIDTH // tc
    bias = hy_bias.reshape(hy_bias.shape[0], HY_ORDER, 1, HY_WIDTH)
    col = lambda k: pl.BlockSpec((2 * n, tc), lambda c: (0, k * nct + c))
    return pl.pallas_call(
        functools.partial(_small_conv_kernel, n=n),
        grid=(nct,),
        in_specs=[col(0), col(1), col(2),
                  pl.BlockSpec((HY_ORDER, 2 * n, tc), lambda c: (0, 0, c)),
                  pl.BlockSpec((1, HY_ORDER, 1, tc), lambda c: (layer, 0, 0, c)),
                  _const_spec(fr.shape, 1), _const_spec(fd.shape, 1), _const_spec(fi.shape, 1),
                  pl.BlockSpec(memory_space=pl.ANY)],
        out_specs=pl.BlockSpec((2 * n, tc), lambda c: (row_blk, c)),
        out_shape=jax.ShapeDtypeStruct(hy.shape, hy.dtype),
        input_output_aliases={8: 0},
        compiler_params=_cparams(1),
        name="hyena_small_conv",
    )(uc, uc, uc, hfull, bias, fr, fd, fi, hy)


def _hyena(u, conv_w, conv_b, filt, hy_bias, layer, seg, row_blk_off, hy=None, out_rows=None):
    uc = _short_conv(u, conv_w, conv_b, layer, 2, seg, row_blk_off)
    hfull = _hyena_filter(seg, 2 * seg, *filt)
    if 2 * seg == FFT_NO * FFT_NI:
        return _long_conv(uc, _filter_spectrum(hfull), hy_bias, layer, seg, out_rows)
    assert row_blk_off % 2 == 0
    return _small_conv(uc, hfull, hy_bias, layer, seg, hy, row_blk_off // 2)


def _moe(h2, logits, w1, w3, w2):
    t, d = h2.shape
    dff = w1.shape[-1]
    top_logit, top_idx = lax.top_k(logits, TOP_K)
    gate = jax.nn.softmax(top_logit, axis=-1)
    n_assign = t * TOP_K
    flat_e = top_idx.reshape(-1)
    flat_tok = jnp.repeat(jnp.arange(t, dtype=jnp.int32), TOP_K)
    onehot = (flat_e[:, None] == jnp.arange(N_EXPERTS, dtype=flat_e.dtype)[None, :]).astype(jnp.int32)
    counts = jnp.sum(onehot, axis=0)
    rank = jnp.sum((jnp.cumsum(onehot, axis=0) - onehot) * onehot, axis=1)
    padded = (counts + MOE_ROWS - 1) // MOE_ROWS * MOE_ROWS
    pad_end = jnp.cumsum(padded)
    pad_start = pad_end - padded
    dest = (jnp.sum(onehot * pad_start[None, :], axis=1) + rank).astype(jnp.int32)
    n_blocks = n_assign // MOE_ROWS + N_EXPERTS
    n_slots = n_blocks * MOE_ROWS
    slot_tok = (jnp.arange(n_slots, dtype=jnp.int32) % t).at[dest].set(flat_tok)
    block_start = jnp.arange(n_blocks, dtype=pad_end.dtype) * MOE_ROWS
    block_expert = jnp.minimum(jnp.sum((pad_end[None, :] <= block_start[:, None]).astype(jnp.int32), axis=1),
                               N_EXPERTS - 1).astype(jnp.int32)
    n_used = (pad_end[-1] // MOE_ROWS).astype(jnp.int32).reshape(1)
    ids = (block_expert, n_used)
    xs = h2[slot_tok]
    act = _gmm(xs, (w1, w3), ids, k=d, n=dff, tm=MOE_ROWS, tn=512, out_dtype=BF16, name="moe_up")
    ys = _gmm(act, (w2,), ids, k=dff, n=d, tm=MOE_ROWS, tn=1024, out_dtype=BF16, name="moe_down")
    pos = dest.reshape(t, TOP_K)
    return ys[pos[:, 0]], ys[pos[:, 1]], gate


def kernel(x, c, ctx, c_ctx, w_ada, b_ada, norm_g, w_in, attn_sink, conv_w, conv_b,
           filt_w1, filt_b1, filt_w2, filt_b2, filt_w3, filt_b3, filt_freq, filt_w_out, hyena_bias,
           w_attn_out, w_hyena_out, w_out, ffn_w1, ffn_w3, ffn_w2,
           moe_router, moe_w1, moe_w3, moe_w2):
    batch, seq, d = x.shape
    n_ctx = ctx.shape[1]
    depth = w_in.shape[0]
    in_width = w_in.shape[-1]
    n_lat = batch * seq
    n_all = n_lat + batch * n_ctx
    n_fft = 2 * seq
    ga_off = HY_OFF + (HY_ORDER + 1) * HY_WIDTH
    gh_off = ga_off + d
    assert batch == 2 and FFT_NO * FFT_NI == n_fft

    x_all = (x.reshape(n_lat, d), ctx.reshape(batch * n_ctx, d))
    cond = jnp.concatenate([c, c_ctx[None]], axis=0)
    mod = _ada(cond, w_ada, b_ada)
    mod = mod.reshape(depth, batch + 1, 6, 1, d)
    tm_all = n_all // 8
    tm_lat = n_lat // 8

    def mods(l, j):
        return mod[l, :, j]

    h = _norm_mod(*x_all, norm_g[0, 0], mods(0, 0), mods(0, 1), seq)
    for l in range(depth):
        last = l == depth - 1
        rows = n_lat if last else n_all
        tm = tm_lat if last else tm_all
        filt = (filt_w1[l], filt_b1[l], filt_w2[l], filt_b2[l], filt_w3[l], filt_b3[l],
                filt_freq[l], filt_w_out[l])

        u = _in_proj(h, w_in, l, _rope_tables_for_rows(seq, batch, rows - n_lat), rows, tm)
        if last:
            kv_ctx = _gmm(h[n_lat:], (w_in,), _dense_ids(batch * n_ctx, batch * n_ctx, l), k=d,
                          n=HY_OFF - K_OFF, tm=batch * n_ctx, tn=512, out_dtype=BF16, w_col_off=K_OFF,
                          name="ctx_kv_proj")
            att = _win_attn(u, kv_ctx, 0, 0, attn_sink[l], batch, seq, n_ctx, rows)
        else:
            att = _win_attn(u, u, n_lat // n_ctx, K_OFF, attn_sink[l], batch, seq, n_ctx, rows)
        hy = _hyena(u, conv_w, conv_b, filt, hyena_bias, l, seq, 0, out_rows=rows)
        if not last:
            att = _ctx_attn(u, att, attn_sink[l], batch, seq, n_ctx)
            hy = _hyena(u, conv_w, conv_b, filt, hyena_bias, l, n_ctx, n_lat // n_ctx, hy=hy)
        mrg = _merge(att, hy, u, w_attn_out, w_hyena_out, l, rows, ga_off, gh_off, tm)
        y = _gmm(mrg, (w_out,), _dense_ids(rows, tm, l), k=d, n=d, tm=tm, tn=512,
                 out_dtype=BF16, name="out_proj")
        router = None
        if l % 2 == 1:
            router = jnp.pad(moe_router[l // 2], ((0, 0), (0, LANES - N_EXPERTS)))
        res = _post(x_all, y, mods(l, 2), norm_g[l, 1], seq, rows,
                    nxt=(norm_g[l, 2], mods(l, 3), mods(l, 4)), router=router)
        x_all, h2 = res[0], res[1]

        if l % 2 == 0:
            i = l // 2
            dff = ffn_w1.shape[-1]
            act = _gmm(h2, (ffn_w1, ffn_w3), _dense_ids(rows, tm, i), k=d, n=dff, tm=tm, tn=256,
                       out_dtype=BF16, name="ffn_up")
            f = _gmm(act, (ffn_w2,), _dense_ids(rows, tm // 4, i), k=dff, n=d, tm=tm // 4, tn=512,
                     out_dtype=BF16, w_single_buffer=True, name="ffn_down")
        else:
            i = l // 2
            f = _moe(h2, res[2][:, :N_EXPERTS], moe_w1[i], moe_w3[i], moe_w2[i])
        if last:
            (x_all,) = _post(x_all, f, mods(l, 5), norm_g[l, 3], seq, rows)
        else:
            x_all, h = _post(x_all, f, mods(l, 5), norm_g[l, 3], seq, rows,
                             nxt=(norm_g[l + 1, 0], mods(l + 1, 0), mods(l + 1, 1)))
    return x_all[:n_lat].reshape(batch, seq, d)
```

```python
import functools
import math

import numpy as np
import jax
import jax.numpy as jnp
from jax import lax
from jax.experimental import pallas as pl
from jax.experimental.pallas import tpu as pltpu

F32 = jnp.float32
BF16 = jnp.bfloat16

GRID_W = 64
N_Q_HEADS = 16
N_KV_HEADS = 4
HEAD_DIM = 128
Q_GROUP = N_Q_HEADS // N_KV_HEADS
ATT_WIDTH = N_Q_HEADS * HEAD_DIM
KV_WIDTH = N_KV_HEADS * HEAD_DIM
BLOCK = 128
ROPE_BASE = 10000.0
ROPE_FREQS = HEAD_DIM // 4
MASK_VALUE = -1e30
HY_WIDTH = 2048
HY_ORDER = 2
SHORT_CONV = 3
FILTER_BANDS = 16
FILTER_HIDDEN = 64
DECAY_TARGET = 1e-2
MIN_DECAY = math.log(DECAY_TARGET) / 1.5
MAX_DECAY = math.log(DECAY_TARGET) / 0.3
K_OFF = ATT_WIDTH
V_OFF = K_OFF + KV_WIDTH
HY_OFF = V_OFF + KV_WIDTH
N_EXPERTS = 8
TOP_K = 2
MOE_ROWS = 512
RMS_EPS = 1e-6

LANES = 128
SUBLANES = 8
VMEM_LIMIT_BYTES = 56 * 1024 * 1024

FFT_NO = 128
FFT_NI = 64
HY_TC = 128


def _cparams(n_axes):
    return pltpu.CompilerParams(dimension_semantics=("arbitrary",) * n_axes,
                                vmem_limit_bytes=VMEM_LIMIT_BYTES)


ADA_CHUNK = 64


def _ada_kernel(c_ref, w_ref, b_ref, o_ref, act_ref):
    n_rows, k, _ = c_ref.shape
    tn = w_ref.shape[-1]

    @pl.when(jnp.logical_and(pl.program_id(0) == 0, pl.program_id(1) == 0))
    def _():
        c = c_ref[...]
        act_ref[...] = c * jax.nn.sigmoid(c)

    def body(i, accs):
        r0 = pl.multiple_of(i * ADA_CHUNK, ADA_CHUNK)
        w = w_ref[0, pl.ds(r0, ADA_CHUNK), :]
        out = []
        for r in range(n_rows):
            a = act_ref[r, pl.ds(r0, ADA_CHUNK), :]
            acc = accs[r]
            for s in range(ADA_CHUNK // SUBLANES):
                rows = slice(s * SUBLANES, (s + 1) * SUBLANES)
                acc = acc + w[rows] * jnp.concatenate([a[rows]] * (tn // LANES), axis=1)
            out.append(acc)
        return tuple(out)

    zero = jnp.zeros((SUBLANES, tn), F32)
    accs = lax.fori_loop(0, k // ADA_CHUNK, body, (zero,) * n_rows)
    bias = b_ref[0]
    for r in range(n_rows):
        o_ref[0, r:r + 1, :] = jnp.sum(accs[r], axis=0, keepdims=True) + bias


def _ada(cond, w_ada, b_ada, tn=512):
    n_layers, d, n = w_ada.shape
    rows = cond.shape[0]
    cb = jnp.broadcast_to(cond[:, :, None], (rows, d, LANES))
    return pl.pallas_call(
        _ada_kernel,
        grid=(n_layers, n // tn),
        in_specs=[pl.BlockSpec((rows, d, LANES), lambda l, j: (0, 0, 0)),
                  pl.BlockSpec((1, d, tn), lambda l, j: (l, 0, j)),
                  pl.BlockSpec((1, 1, tn), lambda l, j: (l, 0, j))],
        out_specs=pl.BlockSpec((1, rows, tn), lambda l, j: (l, 0, j)),
        out_shape=jax.ShapeDtypeStruct((n_layers, rows, n), F32),
        scratch_shapes=[pltpu.VMEM((rows, d, LANES), F32)],
        compiler_params=_cparams(2),
        name="ada_mod",
    )(cb, w_ada, b_ada.reshape(n_layers, 1, n))


def _rms(x):
    return x * lax.rsqrt(jnp.mean(x * x, axis=-1, keepdims=True) + RMS_EPS)


def _group_of_tile(tr, seq):
    return lambda i: (jnp.minimum((i * tr) // seq, 2), 0, 0)


def _two_source_specs(x_lat, x_ctx, tr):
    d = x_lat.shape[1]
    nl = x_lat.shape[0] // tr
    return [pl.BlockSpec((tr, d), lambda i: (jnp.minimum(i, nl - 1), 0)),
            pl.BlockSpec((tr, d), lambda i: (jnp.maximum(i - nl, 0), 0))], nl


def _two_source_rows(lat_ref, ctx_ref, n_lat_tiles):
    return jnp.where(pl.program_id(0) < n_lat_tiles, lat_ref[...], ctx_ref[...])


def _norm_mod2_kernel(lat_ref, ctx_ref, g_ref, sh_ref, sc_ref, o_ref, *, n_lat_tiles):
    y = _rms(_two_source_rows(lat_ref, ctx_ref, n_lat_tiles)) * g_ref[...]
    o_ref[...] = (y * (1.0 + sc_ref[0]) + sh_ref[0]).astype(o_ref.dtype)


def _norm_mod(x_lat, x_ctx, g, sh, sc, seq, tr=256):
    d = x_lat.shape[1]
    t = x_lat.shape[0] + x_ctx.shape[0]
    grp = _group_of_tile(tr, seq)
    x_specs, nl = _two_source_specs(x_lat, x_ctx, tr)
    return pl.pallas_call(
        functools.partial(_norm_mod2_kernel, n_lat_tiles=nl),
        grid=(t // tr,),
        in_specs=x_specs + [pl.BlockSpec((1, d), lambda i: (0, 0)),
                            pl.BlockSpec((1, 1, d), grp),
                            pl.BlockSpec((1, 1, d), grp)],
        out_specs=pl.BlockSpec((tr, d), lambda i: (i, 0)),
        out_shape=jax.ShapeDtypeStruct((t, d), BF16),
        compiler_params=_cparams(1),
        name="norm_mod",
    )(x_lat, x_ctx, g.reshape(1, d), sh, sc)


def _post_kernel(x_ref, *rest, with_next, with_router, with_pair, n_lat_tiles):
    if n_lat_tiles is None:
        x = x_ref[...]
    else:
        ctx_ref, *rest = rest
        x = _two_source_rows(x_ref, ctx_ref, n_lat_tiles)
    if with_pair:
        y0_ref, y1_ref, w_ref, gt_ref, gpost_ref, *rest = rest
        w = w_ref[...]
        y = w[:, 0:1] * y0_ref[...].astype(F32) + w[:, 1:2] * y1_ref[...].astype(F32)
    else:
        y_ref, gt_ref, gpost_ref, *rest = rest
        y = y_ref[...].astype(F32)
    xn = x + gt_ref[0] * (_rms(y) * gpost_ref[...])
    if not with_next:
        (xo_ref,) = rest
        xo_ref[...] = xn
        return
    if with_router:
        gpre_ref, sh_ref, sc_ref, wr_ref, xo_ref, ho_ref, lg_ref = rest
    else:
        gpre_ref, sh_ref, sc_ref, xo_ref, ho_ref = rest
    xo_ref[...] = xn
    h = (_rms(xn) * gpre_ref[...]) * (1.0 + sc_ref[0]) + sh_ref[0]
    ho_ref[...] = h.astype(ho_ref.dtype)
    if with_router:
        lg_ref[...] = jnp.dot(h, wr_ref[...], preferred_element_type=F32,
                              precision=lax.Precision.HIGHEST)


def _post(x_all, y, gt, g_post, seq, rows, nxt=None, router=None, tr=256):
    grp = _group_of_tile(tr, seq)
    if isinstance(x_all, tuple):
        d = x_all[0].shape[1]
        x_specs, n_lat_tiles = _two_source_specs(*x_all, tr)
        x_args = list(x_all)
    else:
        d = x_all.shape[1]
        x_specs, n_lat_tiles = [pl.BlockSpec((tr, d), lambda i: (i, 0))], None
        x_args = [x_all]
    row = pl.BlockSpec((tr, d), lambda i: (i, 0))
    vec = pl.BlockSpec((1, d), lambda i: (0, 0))
    mod = pl.BlockSpec((1, 1, d), grp)
    with_pair = isinstance(y, tuple)
    if with_pair:
        y0, y1, w = y
        in_specs = x_specs + [row, row, pl.BlockSpec((tr, w.shape[1]), lambda i: (i, 0)), mod, vec]
        args = x_args + [y0, y1, w, gt, g_post.reshape(1, d)]
    else:
        in_specs = x_specs + [row, mod, vec]
        args = x_args + [y, gt, g_post.reshape(1, d)]
    out_specs = [row]
    out_shape = [jax.ShapeDtypeStruct((rows, d), F32)]
    if nxt is not None:
        g_pre, sh, sc = nxt
        in_specs += [vec, mod, mod]
        args += [g_pre.reshape(1, d), sh, sc]
        out_specs.append(row)
        out_shape.append(jax.ShapeDtypeStruct((rows, d), BF16))
        if router is not None:
            in_specs.append(pl.BlockSpec((d, LANES), lambda i: (0, 0)))
            args.append(router)
            out_specs.append(pl.BlockSpec((tr, LANES), lambda i: (i, 0)))
            out_shape.append(jax.ShapeDtypeStruct((rows, LANES), F32))
    return pl.pallas_call(
        functools.partial(_post_kernel, with_next=nxt is not None, with_router=router is not None,
                          with_pair=with_pair, n_lat_tiles=n_lat_tiles),
        grid=(rows // tr,),
        in_specs=in_specs,
        out_specs=out_specs,
        out_shape=out_shape,
        compiler_params=_cparams(1),
        name="post_norm",
    )(*args)


def _weight_changed(be_ref, m):
    return jnp.logical_or(m == 0, be_ref[m] != be_ref[jnp.maximum(m - 1, 0)])


def _gmm_kernel(be_ref, nv_ref, a_ref, w_ref, o_ref, wb_ref):
    m = pl.program_id(1)

    @pl.when(m < nv_ref[0])
    def _():
        @pl.when(_weight_changed(be_ref, m))
        def _():
            wb_ref[...] = w_ref[0].astype(BF16)

        o_ref[...] = jnp.dot(a_ref[...], wb_ref[...],
                             preferred_element_type=F32).astype(o_ref.dtype)


def _swiglu_kernel(be_ref, nv_ref, a_ref, w1_ref, w3_ref, o_ref, w1b_ref, w3b_ref):
    m = pl.program_id(1)

    @pl.when(m < nv_ref[0])
    def _():
        @pl.when(_weight_changed(be_ref, m))
        def _():
            w1b_ref[...] = w1_ref[0].astype(BF16)
            w3b_ref[...] = w3_ref[0].astype(BF16)

        a = a_ref[...]
        g = jnp.dot(a, w1b_ref[...], preferred_element_type=F32)
        u = jnp.dot(a, w3b_ref[...], preferred_element_type=F32)
        o_ref[...] = (g * jax.nn.sigmoid(g) * u).astype(o_ref.dtype)


def _gmm(a, ws, ids, *, k, n, tm, tn, out_dtype, w_col_off=0, rows=None, w_single_buffer=False,
         name="gmm"):
    rows = a.shape[0] if rows is None else rows
    assert rows % tm == 0 and n % tn == 0 and w_col_off % tn == 0
    off = w_col_off // tn
    w_mode = dict(pipeline_mode=pl.Buffered(1)) if w_single_buffer else {}

    def a_map(j, m, be_ref, nv_ref):
        return (jnp.minimum(m, nv_ref[0] - 1), 0)

    def w_map(j, m, be_ref, nv_ref):
        return (be_ref[jnp.minimum(m, nv_ref[0] - 1)], 0, j + off)

    def o_map(j, m, be_ref, nv_ref):
        return (m, j)

    kernel = _gmm_kernel if len(ws) == 1 else _swiglu_kernel
    return pl.pallas_call(
        kernel,
        grid_spec=pltpu.PrefetchScalarGridSpec(
            num_scalar_prefetch=2,
            grid=(n // tn, rows // tm),
            in_specs=[pl.BlockSpec((tm, k), a_map)] + [pl.BlockSpec((1, k, tn), w_map, **w_mode)] * len(ws),
            out_specs=pl.BlockSpec((tm, tn), o_map),
            scratch_shapes=[pltpu.VMEM((k, tn), BF16)] * len(ws)),
        out_shape=jax.ShapeDtypeStruct((rows, n), out_dtype),
        compiler_params=_cparams(2),
        name=name,
    )(*ids, a, *ws)


def _dense_ids(rows, tm, idx):
    nb = rows // tm
    return jnp.full((nb,), idx, jnp.int32), jnp.full((1,), nb, jnp.int32)


def _merge_kernel(att_ref, hy_ref, ga_ref, gh_ref, wa_ref, wh_ref, o_ref, wab_ref, whb_ref):
    @pl.when(pl.program_id(1) == 0)
    def _():
        wab_ref[...] = wa_ref[0].astype(BF16)
        whb_ref[...] = wh_ref[0].astype(BF16)

    pa = jnp.dot(att_ref[...], wab_ref[...], preferred_element_type=F32)
    ph = jnp.dot(hy_ref[...], whb_ref[...], preferred_element_type=F32)
    ga = jax.nn.sigmoid(ga_ref[...].astype(F32))
    gh = jax.nn.sigmoid(gh_ref[...].astype(F32))
    o_ref[...] = (ga * pa + gh * ph).astype(o_ref.dtype)


def _merge(att, hy, u, w_ao, w_ho, layer, rows, ga_off, gh_off, tm, tn=512):
    ka, d = w_ao.shape[1:]
    kh = w_ho.shape[1]
    return pl.pallas_call(
        _merge_kernel,
        grid=(d // tn, rows // tm),
        in_specs=[pl.BlockSpec((tm, ka), lambda j, m: (m, 0)),
                  pl.BlockSpec((tm, kh), lambda j, m: (m, 0)),
                  pl.BlockSpec((tm, tn), lambda j, m: (m, ga_off // tn + j)),
                  pl.BlockSpec((tm, tn), lambda j, m: (m, gh_off // tn + j)),
                  pl.BlockSpec((1, ka, tn), lambda j, m: (layer, 0, j)),
                  pl.BlockSpec((1, kh, tn), lambda j, m: (layer, 0, j))],
        out_specs=pl.BlockSpec((tm, tn), lambda j, m: (m, j)),
        out_shape=jax.ShapeDtypeStruct((rows, d), BF16),
        scratch_shapes=[pltpu.VMEM((ka, tn), BF16), pltpu.VMEM((kh, tn), BF16)],
        compiler_params=_cparams(2),
        name="gated_merge",
    )(att, hy, u, u, w_ao, w_ho)


def _rope_tables(seq):
    rows = seq // GRID_W
    row = jnp.repeat(jnp.arange(rows), GRID_W).astype(F32)
    col = jnp.tile(jnp.arange(GRID_W), rows).astype(F32)
    inv = ROPE_BASE ** (-jnp.arange(ROPE_FREQS, dtype=F32) / ROPE_FREQS)
    ang = jnp.stack([row[:, None] * inv, col[:, None] * inv], axis=1)
    cos, sin = jnp.cos(ang), jnp.sin(ang)
    zero = jnp.zeros_like(sin)
    cos_t = jnp.stack([cos, cos], axis=2).reshape(seq, HEAD_DIM)
    s_lo = jnp.stack([-sin, zero], axis=2).reshape(seq, HEAD_DIM)
    s_hi = jnp.stack([zero, sin], axis=2).reshape(seq, HEAD_DIM)
    return cos_t, s_lo, s_hi


def _rope_tables_for_rows(seq, batch, n_extra):
    cos_t, s_lo, s_hi = (jnp.tile(t, (batch, 1)) for t in _rope_tables(seq))
    if n_extra:
        cos_t = jnp.concatenate([cos_t, jnp.ones((n_extra, HEAD_DIM), F32)], axis=0)
        pad = jnp.zeros((n_extra, HEAD_DIM), F32)
        s_lo, s_hi = jnp.concatenate([s_lo, pad], axis=0), jnp.concatenate([s_hi, pad], axis=0)
    return cos_t, s_lo, s_hi


def _in_proj_kernel(a_ref, w_ref, c_ref, lo_ref, hi_ref, o_ref, wb_ref, *, n_rope_tiles):
    j = pl.program_id(0)

    @pl.when(pl.program_id(1) == 0)
    def _():
        wb_ref[...] = w_ref[0].astype(BF16)

    acc = jnp.dot(a_ref[...], wb_ref[...], preferred_element_type=F32)

    @pl.when(j < n_rope_tiles)
    def _():
        c, lo, hi = c_ref[...], lo_ref[...], hi_ref[...]
        for hd in range(acc.shape[1] // HEAD_DIM):
            sl = slice(hd * HEAD_DIM, (hd + 1) * HEAD_DIM)
            x = acc[:, sl]
            up = pltpu.roll(x, HEAD_DIM - ROPE_FREQS, 1)
            dn = pltpu.roll(x, ROPE_FREQS, 1)
            o_ref[:, sl] = (x * c + up * lo + dn * hi).astype(o_ref.dtype)

    @pl.when(j >= n_rope_tiles)
    def _():
        o_ref[...] = acc.astype(o_ref.dtype)


def _in_proj(h, w_in, layer, tables, rows, tm, tn=512):
    k, n = w_in.shape[1:]
    assert rows % tm == 0 and n % tn == 0 and V_OFF % tn == 0
    n_rope_tiles = V_OFF // tn
    tab = pl.BlockSpec((tm, HEAD_DIM), lambda j, m: (jnp.where(j < n_rope_tiles, m, 0), 0))
    return pl.pallas_call(
        functools.partial(_in_proj_kernel, n_rope_tiles=n_rope_tiles),
        grid=(n // tn, rows // tm),
        in_specs=[pl.BlockSpec((tm, k), lambda j, m: (m, 0)),
                  pl.BlockSpec((1, k, tn), lambda j, m: (layer, 0, j)), tab, tab, tab],
        out_specs=pl.BlockSpec((tm, tn), lambda j, m: (m, j)),
        out_shape=jax.ShapeDtypeStruct((rows, n), BF16),
        scratch_shapes=[pltpu.VMEM((k, tn), BF16)],
        compiler_params=_cparams(2),
        name="in_proj",
    )(h, w_in, *tables)


ATTN_ROW_CHUNK = 32
LOG2_E = math.log2(math.e)


def _scores(q, k):
    return lax.dot_general(q, k, (((1,), (1,)), ((), ())), preferred_element_type=F32)


def _win_attn_kernel(sink_ref, q_ref, kp_ref, kc_ref, kn_ref, vp_ref, vc_ref, vn_ref,
                     kx_ref, vx_ref, o_ref, s_ref, p_ref, den_ref):
    n = pl.program_id(1)
    nb = pl.num_programs(1)
    scale = HEAD_DIM ** -0.5
    n_keys = 3 * BLOCK + kx_ref.shape[0]
    qi = lax.broadcasted_iota(jnp.int32, (BLOCK, n_keys), 0)
    kj = lax.broadcasted_iota(jnp.int32, (BLOCK, n_keys), 1)
    bad_prev = jnp.logical_and(kj < BLOCK, jnp.logical_or(kj < qi, n == 0))
    bad_next = jnp.logical_and(jnp.logical_and(kj >= 2 * BLOCK, kj < 3 * BLOCK),
                               jnp.logical_or(kj - 2 * BLOCK > qi, n == nb - 1))
    valid = jnp.logical_not(jnp.logical_or(bad_prev, bad_next))
    head_lanes = lambda head: slice(head * HEAD_DIM, (head + 1) * HEAD_DIM)
    for h in range(N_KV_HEADS):
        hs = head_lanes(h)
        keys = jnp.concatenate([kp_ref[:, hs], kc_ref[:, hs], kn_ref[:, hs], kx_ref[:, hs]], axis=0)
        for g in range(Q_GROUP):
            head = h * Q_GROUP + g
            s_ref[head] = jnp.where(valid, _scores(q_ref[:, head_lanes(head)], keys) * (scale * LOG2_E),
                                    MASK_VALUE)
    for head in range(N_Q_HEADS):
        sink = sink_ref[head // Q_GROUP, head % Q_GROUP] * LOG2_E
        for r0 in range(0, BLOCK, ATTN_ROW_CHUNK):
            rows = slice(r0, r0 + ATTN_ROW_CHUNK)
            s = s_ref[head, rows, :]
            mx = jnp.maximum(jnp.max(s, axis=-1, keepdims=True), sink)
            p = jnp.exp2(s - mx)
            den_ref[head, rows, :] = jnp.sum(p, axis=-1, keepdims=True) + jnp.exp2(sink - mx)
            p_ref[head, rows, :] = p.astype(BF16)
    for h in range(N_KV_HEADS):
        hs = head_lanes(h)
        vals = jnp.concatenate([vp_ref[:, hs], vc_ref[:, hs], vn_ref[:, hs], vx_ref[:, hs]], axis=0)
        for g in range(Q_GROUP):
            head = h * Q_GROUP + g
            o = jnp.dot(p_ref[head], vals, preferred_element_type=F32)
            o_ref[:, head_lanes(head)] = (o / den_ref[head]).astype(o_ref.dtype)


def _win_attn(u, kv_ctx, ctx_row_blk, ctx_k_col, sink, batch, seq, n_ctx, out_rows):
    nb = seq // BLOCK
    assert K_OFF % KV_WIDTH == 0 and V_OFF % KV_WIDTH == 0 and ctx_k_col % KV_WIDTH == 0
    kcol = K_OFF // KV_WIDTH
    vcol = V_OFF // KV_WIDTH
    xk = ctx_k_col // KV_WIDTH

    def blk(shift, col):
        def index(b, n):
            return (b * nb + jnp.clip(n + shift, 0, nb - 1), col)
        return pl.BlockSpec((BLOCK, KV_WIDTH), index)

    return pl.pallas_call(
        _win_attn_kernel,
        grid=(batch, nb),
        in_specs=[pl.BlockSpec(memory_space=pltpu.SMEM),
                  pl.BlockSpec((BLOCK, ATT_WIDTH), lambda b, n: (b * nb + n, 0)),
                  blk(-1, kcol), blk(0, kcol), blk(1, kcol),
                  blk(-1, vcol), blk(0, vcol), blk(1, vcol),
                  pl.BlockSpec((n_ctx, KV_WIDTH), lambda b, n: (ctx_row_blk + b, xk)),
                  pl.BlockSpec((n_ctx, KV_WIDTH), lambda b, n: (ctx_row_blk + b, xk + 1))],
        out_specs=pl.BlockSpec((BLOCK, ATT_WIDTH), lambda b, n: (b * nb + n, 0)),
        out_shape=jax.ShapeDtypeStruct((out_rows, ATT_WIDTH), BF16),
        scratch_shapes=[pltpu.VMEM((N_Q_HEADS, BLOCK, 3 * BLOCK + n_ctx), F32),
                        pltpu.VMEM((N_Q_HEADS, BLOCK, 3 * BLOCK + n_ctx), BF16),
                        pltpu.VMEM((N_Q_HEADS, BLOCK, 1), F32)],
        compiler_params=_cparams(2),
        name="window_attention",
    )(sink.reshape(N_KV_HEADS, Q_GROUP), u, u, u, u, u, u, u, kv_ctx, kv_ctx)


def _ctx_attn_kernel(sink_ref, q_ref, k_ref, v_ref, att_ref, o_ref):
    del att_ref
    h = pl.program_id(1)
    scale = HEAD_DIM ** -0.5
    k, v = k_ref[...], v_ref[...]
    for g in range(Q_GROUP):
        sl = slice(g * HEAD_DIM, (g + 1) * HEAD_DIM)
        s = _scores(q_ref[:, sl], k) * scale
        sink = sink_ref[h, g]
        mx = jnp.maximum(jnp.max(s, axis=-1, keepdims=True), sink)
        p = jnp.exp(s - mx)
        den = jnp.sum(p, axis=-1, keepdims=True) + jnp.exp(sink - mx)
        o = jnp.dot(p.astype(BF16), v, preferred_element_type=F32)
        o_ref[:, sl] = (o / den).astype(o_ref.dtype)


def _ctx_attn(u, att, sink, batch, seq, n_ctx):
    hd = HEAD_DIM
    qw = Q_GROUP * hd
    ctx_blk = (batch * seq) // n_ctx
    return pl.pallas_call(
        _ctx_attn_kernel,
        grid=(batch, N_KV_HEADS),
        in_specs=[pl.BlockSpec(memory_space=pltpu.SMEM),
                  pl.BlockSpec((n_ctx, qw), lambda b, h: (ctx_blk + b, h)),
                  pl.BlockSpec((n_ctx, hd), lambda b, h: (ctx_blk + b, K_OFF // hd + h)),
                  pl.BlockSpec((n_ctx, hd), lambda b, h: (ctx_blk + b, V_OFF // hd + h)),
                  pl.BlockSpec(memory_space=pl.ANY)],
        out_specs=pl.BlockSpec((n_ctx, qw), lambda b, h: (ctx_blk + b, h)),
        out_shape=jax.ShapeDtypeStruct(att.shape, att.dtype),
        input_output_aliases={4: 0},
        compiler_params=_cparams(2),
        name="context_attention",
    )(sink.reshape(N_KV_HEADS, Q_GROUP), u, u, u, att)


def _short_conv_kernel(u_ref, w_ref, b_ref, o_ref):
    x = u_ref[...].astype(F32)
    n = x.shape[0]
    r = lax.broadcasted_iota(jnp.int32, x.shape, 0)
    prev = jnp.where(r == 0, 0.0, pltpu.roll(x, 1, 0))
    nxt = jnp.where(r == n - 1, 0.0, pltpu.roll(x, n - 1, 0))
    w = w_ref[0]
    o_ref[...] = (prev * w[0:1] + x * w[1:2] + nxt * w[2:3] + b_ref[0]).astype(o_ref.dtype)


def _short_conv(u, conv_w, conv_b, layer, n_seq, seg, row_blk_off, tw=256):
    width = conv_w.shape[-1]
    cb = conv_b.reshape(conv_b.shape[0], 1, width)
    return pl.pallas_call(
        _short_conv_kernel,
        grid=(n_seq, width // tw),
        in_specs=[pl.BlockSpec((seg, tw), lambda s, j: (row_blk_off + s, HY_OFF // tw + j)),
                  pl.BlockSpec((1, SHORT_CONV, tw), lambda s, j: (layer, 0, j)),
                  pl.BlockSpec((1, 1, tw), lambda s, j: (layer, 0, j))],
        out_specs=pl.BlockSpec((seg, tw), lambda s, j: (s, j)),
        out_shape=jax.ShapeDtypeStruct((n_seq * seg, width), BF16),
        compiler_params=_cparams(2),
        name="short_conv",
    )(u, conv_w, cb)


def _filter_positions(n, n_fft):
    t = jnp.linspace(0.0, 1.0, n, dtype=F32)[:, None]
    w = 2.0 * math.pi * jnp.arange(n, dtype=F32)[:, None] / n
    bands = jnp.linspace(1e-4, FILTER_BANDS - 1, FILTER_BANDS, dtype=F32)[None, :]
    z = jnp.concatenate([t, jnp.cos(bands * w), -jnp.sin(bands * w)], axis=-1)
    zt = jnp.concatenate([z, t], axis=-1)
    mid = jnp.zeros((n_fft - 2 * n + 1, zt.shape[1]), F32)
    full = jnp.concatenate([zt, mid, zt[1:][::-1]], axis=0)
    feat = jnp.pad(full[:, :-1], ((0, 0), (0, FILTER_HIDDEN - (zt.shape[1] - 1))))
    return feat, full[:, -1:]


def _filter_kernel(z_ref, t_ref, w1_ref, b1_ref, w2_ref, b2_ref, w3_ref, b3_ref, fr_ref,
                   wo_ref, dl_ref, o_ref, hid_ref, *, n, n_fft, row_group):
    hi = lax.Precision.HIGHEST
    tr = z_ref.shape[0]

    @pl.when(pl.program_id(1) == 0)
    def _():
        fr = fr_ref[...]
        h = jnp.sin(fr * (jnp.dot(z_ref[...], w1_ref[...], preferred_element_type=F32, precision=hi)
                          + b1_ref[...]))
        h = jnp.sin(fr * (jnp.dot(h, w2_ref[...], preferred_element_type=F32, precision=hi)
                          + b2_ref[...]))
        h = jnp.sin(fr * (jnp.dot(h, w3_ref[...], preferred_element_type=F32, precision=hi)
                          + b3_ref[...]))
        h_hi = h.astype(BF16)
        hid_ref[0] = h_hi
        hid_ref[1] = (h - h_hi.astype(F32)).astype(BF16)

    h_hi, h_lo = hid_ref[0], hid_ref[1]
    row = pl.program_id(0) * tr + lax.broadcasted_iota(jnp.int32, (tr, 1), 0)
    live = jnp.logical_or(row < n, row > n_fft - n)
    decay = jnp.where(live, jnp.exp(-t_ref[...] * dl_ref[...]), 0.0)
    for o in range(HY_ORDER):
        w = wo_ref[o, 0]
        w_hi = w.astype(BF16)
        w_lo = (w - w_hi.astype(F32)).astype(BF16)
        acc = (jnp.dot(h_hi, w_hi, preferred_element_type=F32)
               + jnp.dot(h_lo, w_hi, preferred_element_type=F32)
               + jnp.dot(h_hi, w_lo, preferred_element_type=F32)) * decay
        if row_group is None:
            o_ref[o] = acc
        else:
            for g in range(tr // row_group):
                o_ref[o, g * _pitch(row_group):g * _pitch(row_group) + row_group, :] = (
                    acc[g * row_group:(g + 1) * row_group])


def _hyena_filter(n, n_fft, w1, b1, w2, b2, w3, b3, freq, w_out, row_group=None, tw=512):
    tr = min(512, n)
    assert n % tr == 0 and n_fft % tr == 0
    tr_out, rows_out = tr, n_fft
    if row_group is not None:
        assert tr % row_group == 0
        tr_out, rows_out = (tr // row_group) * _pitch(row_group), (n_fft // row_group) * _pitch(row_group)
    feat, tpos = _filter_positions(n, n_fft)
    hid = FILTER_HIDDEN
    w1p = jnp.pad(w1, ((0, hid - w1.shape[0]), (0, 0)))
    wo = w_out.reshape(hid, HY_ORDER, 2, HY_WIDTH).transpose(1, 2, 0, 3)
    deltas = jnp.abs(jnp.linspace(MIN_DECAY, MAX_DECAY, HY_WIDTH, dtype=F32)).reshape(1, HY_WIDTH)
    small = lambda shape: pl.BlockSpec(shape, lambda r, j: (0,) * len(shape))
    return pl.pallas_call(
        functools.partial(_filter_kernel, n=n, n_fft=n_fft, row_group=row_group),
        grid=(n_fft // tr, HY_WIDTH // tw),
        in_specs=[pl.BlockSpec((tr, hid), lambda r, j: (r, 0)),
                  pl.BlockSpec((tr, 1), lambda r, j: (r, 0)),
                  small((hid, hid)), small((1, hid)), small((hid, hid)), small((1, hid)),
                  small((hid, hid)), small((1, hid)), small((1, hid)),
                  pl.BlockSpec((HY_ORDER, 1, hid, tw), lambda r, j: (0, jnp.where(r * tr >= n, 1, 0), 0, j)),
                  pl.BlockSpec((1, tw), lambda r, j: (0, j))],
        out_specs=pl.BlockSpec((HY_ORDER, tr_out, tw), lambda r, j: (0, r, j)),
        out_shape=jax.ShapeDtypeStruct((HY_ORDER, rows_out, HY_WIDTH), F32),
        scratch_shapes=[pltpu.VMEM((2, tr, hid), BF16)],
        compiler_params=_cparams(2),
        name="hyena_filter",
    )(feat, tpos, w1p, b1.reshape(1, hid), w2, b2.reshape(1, hid), w3, b3.reshape(1, hid),
      freq.reshape(1, hid), wo, deltas)


@functools.lru_cache(maxsize=None)
def _dft_constants(n_fft, no, ni):
    jo = np.arange(no)
    k1 = np.arange(no)
    half = no // 2
    f1r, f1d, f1i = [], [], []
    for i in range(ni):
        ang = 2.0 * np.pi * np.outer(k1, jo * ni + i) / n_fft
        c, s = np.cos(ang), np.sin(ang)
        f1r.append(np.concatenate([c, -s], axis=0))
        ch, sh = c[:, :half], s[:, :half]
        f1d.append(np.block([[ch, sh], [-sh, ch]]))
        angi = 2.0 * np.pi * np.outer(np.arange(half) * ni + i, k1) / n_fft
        ci, si = np.cos(angi), np.sin(angi)
        f1i.append(np.block([[ci, -si], [si, ci]]))
    ang3 = 2.0 * np.pi * np.outer(np.arange(ni), np.arange(ni)) / ni
    c3, s3 = np.cos(ang3), np.sin(ang3)
    m3 = np.block([[c3, s3], [-s3, c3]])
    m3c = np.block([[c3, -s3], [s3, c3]])
    as_f32 = lambda a: np.asarray(a, np.float32)
    return (as_f32(np.stack(f1r)), as_f32(np.stack(f1d)), as_f32(np.stack(f1i)),
            as_f32(m3), as_f32(m3c))


def _mxu_operand(const):
    return jnp.asarray(const, F32).astype(BF16)


def _const_spec(shape, n_axes):
    zeros = (0,) * len(shape)
    if n_axes == 2:
        index = lambda a, b: zeros
    else:
        index = lambda a: zeros
    return pl.BlockSpec(shape, index, pipeline_mode=pl.Buffered(1))


DFT_UNROLL = 32


def _pitch(size):
    return size + SUBLANES


def _block_at(i, size):
    return pl.ds(pl.multiple_of(i * _pitch(size), SUBLANES), size)


def _dft_stage3_rhs(a_ref, k1, no, ni):
    re = a_ref[pl.ds(k1, ni, stride=_pitch(2 * no)), :]
    im = a_ref[pl.ds(no + k1, ni, stride=_pitch(2 * no)), :]
    return jnp.concatenate([re, im], axis=0).astype(BF16)


def _dft_stage3_rhs_pair(a_ref, pair, no, ni):
    return jnp.concatenate([_dft_stage3_rhs(a_ref, 2 * pair, no, ni),
                            _dft_stage3_rhs(a_ref, 2 * pair + 1, no, ni)], axis=1)


def _spectrum_kernel(h_ref, f1r_ref, m3_ref, o_ref, a_ref, *, no, ni):
    inv_n = 1.0 / (no * ni)
    tc = h_ref.shape[-1]

    def stage1(ji, carry):
        rhs = h_ref[0, pl.ds(ji, no, stride=_pitch(ni)), :].astype(BF16)
        a_ref[_block_at(ji, 2 * no), :] = jnp.dot(f1r_ref[ji], rhs, preferred_element_type=F32)
        return carry

    lax.fori_loop(0, ni, stage1, 0, unroll=DFT_UNROLL)

    def stage3(pair, carry):
        x = jnp.dot(m3_ref[...], _dft_stage3_rhs_pair(a_ref, pair, no, ni),
                    preferred_element_type=F32) * inv_n
        r0 = pl.multiple_of(pair * 2 * ni, 2 * ni)
        for i in range(2):
            lanes = slice(i * tc, (i + 1) * tc)
            o_ref[0, 0, pl.ds(r0 + i * ni, ni), :] = x[:ni, lanes].astype(o_ref.dtype)
            o_ref[0, 1, pl.ds(r0 + i * ni, ni), :] = x[ni:, lanes].astype(o_ref.dtype)
        return carry

    lax.fori_loop(0, no // 2, stage3, 0, unroll=DFT_UNROLL // 2)


def _filter_spectrum(hfull, no=FFT_NO, ni=FFT_NI, tc=HY_TC):
    n_ord, rows, width = hfull.shape
    n_fft = no * ni
    assert rows == no * _pitch(ni)
    f1r, _, _, m3, _ = map(_mxu_operand, _dft_constants(n_fft, no, ni))
    return pl.pallas_call(
        functools.partial(_spectrum_kernel, no=no, ni=ni),
        grid=(n_ord, width // tc),
        in_specs=[pl.BlockSpec((1, rows, tc), lambda o, c: (o, 0, c)),
                  _const_spec(f1r.shape, 2), _const_spec(m3.shape, 2)],
        out_specs=pl.BlockSpec((1, 2, n_fft, tc), lambda o, c: (o, 0, 0, c)),
        out_shape=jax.ShapeDtypeStruct((n_ord, 2, n_fft, width), BF16),
        scratch_shapes=[pltpu.VMEM((ni * _pitch(2 * no), tc), F32)],
        compiler_params=_cparams(2),
        name="hyena_spectrum",
    )(hfull, f1r, m3)


def _long_conv_kernel(v_ref, g_ref, spec_ref, bias_ref, f1d_ref, f1i_ref, m3_ref, m3c_ref,
                      o_ref, z_ref, a_ref, b_ref, *, n, no, ni):
    order = pl.program_id(1)
    half = no // 2
    assert n == half * ni

    @pl.when(order == 0)
    def _():
        def load(jo, carry):
            r0 = pl.multiple_of(jo * ni, ni)
            for b in range(2):
                z_ref[b, _block_at(jo, ni), :] = v_ref[pl.ds(b * n + r0, ni), :].astype(F32)
            return carry

        lax.fori_loop(0, half, load, 0, unroll=DFT_UNROLL)

    def stage1(ji, carry):
        zr = z_ref[0, pl.ds(ji, half, stride=_pitch(ni)), :]
        zi = z_ref[1, pl.ds(ji, half, stride=_pitch(ni)), :]
        rhs = jnp.concatenate([zr, zi], axis=0).astype(BF16)
        a_ref[_block_at(ji, 2 * no), :] = jnp.dot(f1d_ref[ji], rhs, preferred_element_type=F32)
        return carry

    lax.fori_loop(0, ni, stage1, 0, unroll=DFT_UNROLL)

    def stage3(pair, carry):
        x = jnp.dot(m3_ref[...], _dft_stage3_rhs_pair(a_ref, pair, no, ni), preferred_element_type=F32)
        r0 = pl.multiple_of(pair * 2 * ni, 2 * ni)
        side_by_side = lambda s: jnp.concatenate([s[:ni], s[ni:]], axis=1).astype(F32)
        hr = side_by_side(spec_ref[0, 0, pl.ds(r0, 2 * ni), :])
        hi = side_by_side(spec_ref[0, 1, pl.ds(r0, 2 * ni), :])
        xr, xi = x[:ni], x[ni:]
        y = jnp.concatenate([xr * hr - xi * hi, xr * hi + xi * hr], axis=0).astype(BF16)
        b = jnp.dot(m3c_ref[...], y, preferred_element_type=F32)
        tc = b.shape[1] // 2
        b_ref[_block_at(2 * pair, 2 * ni), :] = b[:, :tc]
        b_ref[_block_at(2 * pair + 1, 2 * ni), :] = b[:, tc:]
        return carry

    lax.fori_loop(0, no // 2, stage3, 0, unroll=DFT_UNROLL // 2)

    def stage1_inv(t2, carry):
        br = b_ref[pl.ds(t2, no, stride=_pitch(2 * ni)), :]
        bi = b_ref[pl.ds(ni + t2, no, stride=_pitch(2 * ni)), :]
        rhs = jnp.concatenate([br, bi], axis=0).astype(BF16)
        a_ref[pl.ds(pl.multiple_of(t2 * _pitch(2 * no), SUBLANES), no), :] = jnp.dot(
            f1i_ref[t2], rhs, preferred_element_type=F32)
        return carry

    lax.fori_loop(0, ni, stage1_inv, 0, unroll=DFT_UNROLL)

    bias = bias_ref[0, 0]

    def finish(t1, carry):
        r0 = pl.multiple_of(t1 * ni, ni)
        conv = (a_ref[pl.ds(t1, ni, stride=_pitch(2 * no)), :],
                a_ref[pl.ds(half + t1, ni, stride=_pitch(2 * no)), :])
        for b in range(2):
            z = z_ref[b, _block_at(t1, ni), :]
            gate = g_ref[pl.ds(b * n + r0, ni), :].astype(F32)
            zn = gate * (conv[b] + z * bias)
            z_ref[b, _block_at(t1, ni), :] = zn
            o_ref[pl.ds(b * n + r0, ni), :] = zn.astype(o_ref.dtype)
        return carry

    lax.fori_loop(0, half, finish, 0, unroll=DFT_UNROLL)


def _long_conv(uc, spec, hy_bias, layer, n, out_rows, no=FFT_NO, ni=FFT_NI, tc=HY_TC):
    n_fft = no * ni
    assert 2 * n == n_fft
    _, f1d, f1i, m3, m3c = map(_mxu_operand, _dft_constants(n_fft, no, ni))
    nct = HY_WIDTH // tc
    bias = hy_bias.reshape(hy_bias.shape[0], HY_ORDER, 1, HY_WIDTH)
    return pl.pallas_call(
        functools.partial(_long_conv_kernel, n=n, no=no, ni=ni),
        grid=(nct, HY_ORDER),
        in_specs=[pl.BlockSpec((2 * n, tc), lambda c, o: (0, c)),
                  pl.BlockSpec((2 * n, tc), lambda c, o: (0, (1 + o) * nct + c)),
                  pl.BlockSpec((1, 2, n_fft, tc), lambda c, o: (o, 0, 0, c)),
                  pl.BlockSpec((1, 1, 1, tc), lambda c, o: (layer, o, 0, c)),
                  _const_spec(f1d.shape, 2), _const_spec(f1i.shape, 2),
                  _const_spec(m3.shape, 2), _const_spec(m3c.shape, 2)],
        out_specs=pl.BlockSpec((2 * n, tc), lambda c, o: (0, c)),
        out_shape=jax.ShapeDtypeStruct((out_rows, HY_WIDTH), BF16),
        scratch_shapes=[pltpu.VMEM((2, (no // 2) * _pitch(ni), tc), F32),
                        pltpu.VMEM((ni * _pitch(2 * no), tc), F32),
                        pltpu.VMEM((no * _pitch(2 * ni), tc), F32)],
        compiler_params=_cparams(2),
        name="hyena_long_conv",
    )(uc, uc, spec, bias, f1d, f1i, m3, m3c)


@functools.lru_cache(maxsize=None)
def _small_dft_constants(n):
    n_fft = 2 * n
    k = np.arange(n_fft)
    ang = 2.0 * np.pi * np.outer(k, np.arange(n_fft)) / n_fft
    c, s = np.cos(ang), np.sin(ang)
    fr = np.concatenate([c, -s], axis=0)
    ch, sh = c[:, :n], s[:, :n]
    fd = np.block([[ch, sh], [-sh, ch]])
    ci, si = c[:n, :], s[:n, :]
    fi = np.block([[ci, -si], [si, ci]])
    as_f32 = lambda a: np.asarray(a, np.float32)
    return as_f32(fr), as_f32(fd), as_f32(fi)


def _small_conv_kernel(v_ref, g1_ref, g2_ref, h_ref, bias_ref, fr_ref, fd_ref, fi_ref, hy_ref, o_ref,
                       *, n):
    del hy_ref
    n_fft = 2 * n
    z = v_ref[...].astype(F32)
    for o, g_ref in enumerate((g1_ref, g2_ref)):
        hs = jnp.dot(fr_ref[...], h_ref[o].astype(BF16), preferred_element_type=F32) * (1.0 / n_fft)
        x = jnp.dot(fd_ref[...], z.astype(BF16), preferred_element_type=F32)
        xr, xi, hr, hi = x[:n_fft], x[n_fft:], hs[:n_fft], hs[n_fft:]
        y = jnp.concatenate([xr * hr - xi * hi, xr * hi + xi * hr], axis=0).astype(BF16)
        conv = jnp.dot(fi_ref[...], y, preferred_element_type=F32)
        z = g_ref[...].astype(F32) * (conv + z * bias_ref[0, o])
    o_ref[...] = z.astype(o_ref.dtype)


def _small_conv(uc, hfull, hy_bias, layer, n, hy, row_blk, tc=256):
    fr, fd, fi = map(_mxu_operand, _small_dft_constants(n))
    nct = HY_WIDTH // tc
    bias = hy_bias.reshape(hy_bias.shape[0], HY_ORDER, 1, HY_WIDTH)
    col = lambda k: pl.BlockSpec((2 * n, tc), lambda c: (0, k * nct + c))
    return pl.pallas_call(
        functools.partial(_small_conv_kernel, n=n),
        grid=(nct,),
        in_specs=[col(0), col(1), col(2),
                  pl.BlockSpec((HY_ORDER, 2 * n, tc), lambda c: (0, 0, c)),
                  pl.BlockSpec((1, HY_ORDER, 1, tc), lambda c: (layer, 0, 0, c)),
                  _const_spec(fr.shape, 1), _const_spec(fd.shape, 1), _const_spec(fi.shape, 1),
                  pl.BlockSpec(memory_space=pl.ANY)],
        out_specs=pl.BlockSpec((2 * n, tc), lambda c: (row_blk, c)),
        out_shape=jax.ShapeDtypeStruct(hy.shape, hy.dtype),
        input_output_aliases={8: 0},
        compiler_params=_cparams(1),
        name="hyena_small_conv",
    )(uc, uc, uc, hfull, bias, fr, fd, fi, hy)


def _hyena(u, conv_w, conv_b, filt, hy_bias, layer, seg, row_blk_off, hy=None, out_rows=None):
    uc = _short_conv(u, conv_w, conv_b, layer, 2, seg, row_blk_off)
    if 2 * seg == FFT_NO * FFT_NI:
        hfull = _hyena_filter(seg, 2 * seg, *filt, row_group=FFT_NI)
        return _long_conv(uc, _filter_spectrum(hfull), hy_bias, layer, seg, out_rows)
    assert row_blk_off % 2 == 0
    hfull = _hyena_filter(seg, 2 * seg, *filt)
    return _small_conv(uc, hfull, hy_bias, layer, seg, hy, row_blk_off // 2)


def _moe(h2, logits, w1, w3, w2):
    t, d = h2.shape
    dff = w1.shape[-1]
    top_logit, top_idx = lax.top_k(logits, TOP_K)
    gate = jax.nn.softmax(top_logit, axis=-1)
    n_assign = t * TOP_K
    flat_e = top_idx.reshape(-1)
    flat_tok = jnp.repeat(jnp.arange(t, dtype=jnp.int32), TOP_K)
    onehot = (flat_e[:, None] == jnp.arange(N_EXPERTS, dtype=flat_e.dtype)[None, :]).astype(jnp.int32)
    counts = jnp.sum(onehot, axis=0)
    rank = jnp.sum((jnp.cumsum(onehot, axis=0) - onehot) * onehot, axis=1)
    padded = (counts + MOE_ROWS - 1) // MOE_ROWS * MOE_ROWS
    pad_end = jnp.cumsum(padded)
    pad_start = pad_end - padded
    dest = (jnp.sum(onehot * pad_start[None, :], axis=1) + rank).astype(jnp.int32)
    n_blocks = n_assign // MOE_ROWS + N_EXPERTS
    n_slots = n_blocks * MOE_ROWS
    slot_tok = (jnp.arange(n_slots, dtype=jnp.int32) % t).at[dest].set(flat_tok)
    block_start = jnp.arange(n_blocks, dtype=pad_end.dtype) * MOE_ROWS
    block_expert = jnp.minimum(jnp.sum((pad_end[None, :] <= block_start[:, None]).astype(jnp.int32), axis=1),
                               N_EXPERTS - 1).astype(jnp.int32)
    n_used = (pad_end[-1] // MOE_ROWS).astype(jnp.int32).reshape(1)
    ids = (block_expert, n_used)
    xs = h2[slot_tok]
    act = _gmm(xs, (w1, w3), ids, k=d, n=dff, tm=MOE_ROWS, tn=512, out_dtype=BF16, name="moe_up")
    ys = _gmm(act, (w2,), ids, k=dff, n=d, tm=MOE_ROWS, tn=1024, out_dtype=BF16, name="moe_down")
    pos = dest.reshape(t, TOP_K)
    return ys[pos[:, 0]], ys[pos[:, 1]], gate


def kernel(x, c, ctx, c_ctx, w_ada, b_ada, norm_g, w_in, attn_sink, conv_w, conv_b,
           filt_w1, filt_b1, filt_w2, filt_b2, filt_w3, filt_b3, filt_freq, filt_w_out, hyena_bias,
           w_attn_out, w_hyena_out, w_out, ffn_w1, ffn_w3, ffn_w2,
           moe_router, moe_w1, moe_w3, moe_w2):
    batch, seq, d = x.shape
    n_ctx = ctx.shape[1]
    depth = w_in.shape[0]
    in_width = w_in.shape[-1]
    n_lat = batch * seq
    n_all = n_lat + batch * n_ctx
    n_fft = 2 * seq
    ga_off = HY_OFF + (HY_ORDER + 1) * HY_WIDTH
    gh_off = ga_off + d
    assert batch == 2 and FFT_NO * FFT_NI == n_fft

    x_all = (x.reshape(n_lat, d), ctx.reshape(batch * n_ctx, d))
    cond = jnp.concatenate([c, c_ctx[None]], axis=0)
    mod = _ada(cond, w_ada, b_ada)
    mod = mod.reshape(depth, batch + 1, 6, 1, d)
    tm_all = n_all // 8
    tm_lat = n_lat // 8

    def mods(l, j):
        return mod[l, :, j]

    h = _norm_mod(*x_all, norm_g[0, 0], mods(0, 0), mods(0, 1), seq)
    for l in range(depth):
        last = l == depth - 1
        rows = n_lat if last else n_all
        tm = tm_lat if last else tm_all
        filt = (filt_w1[l], filt_b1[l], filt_w2[l], filt_b2[l], filt_w3[l], filt_b3[l],
                filt_freq[l], filt_w_out[l])

        u = _in_proj(h, w_in, l, _rope_tables_for_rows(seq, batch, rows - n_lat), rows, tm)
        if last:
            kv_ctx = _gmm(h[n_lat:], (w_in,), _dense_ids(batch * n_ctx, batch * n_ctx, l), k=d,
                          n=HY_OFF - K_OFF, tm=batch * n_ctx, tn=512, out_dtype=BF16, w_col_off=K_OFF,
                          name="ctx_kv_proj")
            att = _win_attn(u, kv_ctx, 0, 0, attn_sink[l], batch, seq, n_ctx, rows)
        else:
            att = _win_attn(u, u, n_lat // n_ctx, K_OFF, attn_sink[l], batch, seq, n_ctx, rows)
        hy = _hyena(u, conv_w, conv_b, filt, hyena_bias, l, seq, 0, out_rows=rows)
        if not last:
            att = _ctx_attn(u, att, attn_sink[l], batch, seq, n_ctx)
            hy = _hyena(u, conv_w, conv_b, filt, hyena_bias, l, n_ctx, n_lat // n_ctx, hy=hy)
        mrg = _merge(att, hy, u, w_attn_out, w_hyena_out, l, rows, ga_off, gh_off, tm)
        y = _gmm(mrg, (w_out,), _dense_ids(rows, tm, l), k=d, n=d, tm=tm, tn=512,
                 out_dtype=BF16, name="out_proj")
        router = None
        if l % 2 == 1:
            router = jnp.pad(moe_router[l // 2], ((0, 0), (0, LANES - N_EXPERTS)))
        res = _post(x_all, y, mods(l, 2), norm_g[l, 1], seq, rows,
                    nxt=(norm_g[l, 2], mods(l, 3), mods(l, 4)), router=router)
        x_all, h2 = res[0], res[1]

        if l % 2 == 0:
            i = l // 2
            dff = ffn_w1.shape[-1]
            act = _gmm(h2, (ffn_w1, ffn_w3), _dense_ids(rows, tm, i), k=d, n=dff, tm=tm, tn=256,
                       out_dtype=BF16, name="ffn_up")
            f = _gmm(act, (ffn_w2,), _dense_ids(rows, tm // 4, i), k=dff, n=d, tm=tm // 4, tn=512,
                     out_dtype=BF16, w_single_buffer=True, name="ffn_down")
        else:
            i = l // 2
            f = _moe(h2, res[2][:, :N_EXPERTS], moe_w1[i], moe_w3[i], moe_w2[i])
        if last:
            (x_all,) = _post(x_all, f, mods(l, 5), norm_g[l, 3], seq, rows)
        else:
            x_all, h = _post(x_all, f, mods(l, 5), norm_g[l, 3], seq, rows,
                             nxt=(norm_g[l + 1, 0], mods(l + 1, 0), mods(l + 1, 1)))
    return x_all[:n_lat].reshape(batch, seq, d)
```

```python
import functools
import math

import numpy as np
import jax
import jax.numpy as jnp
from jax import lax
from jax.experimental import pallas as pl
from jax.experimental.pallas import tpu as pltpu

F32 = jnp.float32
BF16 = jnp.bfloat16

GRID_W = 64
N_Q_HEADS = 16
N_KV_HEADS = 4
HEAD_DIM = 128
Q_GROUP = N_Q_HEADS // N_KV_HEADS
ATT_WIDTH = N_Q_HEADS * HEAD_DIM
KV_WIDTH = N_KV_HEADS * HEAD_DIM
BLOCK = 128
ROPE_BASE = 10000.0
ROPE_FREQS = HEAD_DIM // 4
MASK_VALUE = -1e30
HY_WIDTH = 2048
HY_ORDER = 2
SHORT_CONV = 3
FILTER_BANDS = 16
FILTER_HIDDEN = 64
DECAY_TARGET = 1e-2
MIN_DECAY = math.log(DECAY_TARGET) / 1.5
MAX_DECAY = math.log(DECAY_TARGET) / 0.3
K_OFF = ATT_WIDTH
V_OFF = K_OFF + KV_WIDTH
HY_OFF = V_OFF + KV_WIDTH
N_EXPERTS = 8
TOP_K = 2
MOE_ROWS = 512
RMS_EPS = 1e-6

LANES = 128
SUBLANES = 8
VMEM_LIMIT_BYTES = 56 * 1024 * 1024

FFT_NO = 128
FFT_NI = 64
HY_TC = 128


def _cparams(n_axes):
    return pltpu.CompilerParams(dimension_semantics=("arbitrary",) * n_axes,
                                vmem_limit_bytes=VMEM_LIMIT_BYTES)


ADA_CHUNK = 64


def _ada_kernel(c_ref, w_ref, b_ref, o_ref, act_ref):
    n_rows, k, _ = c_ref.shape
    tn = w_ref.shape[-1]

    @pl.when(jnp.logical_and(pl.program_id(0) == 0, pl.program_id(1) == 0))
    def _():
        c = c_ref[...]
        act_ref[...] = c * jax.nn.sigmoid(c)

    def body(i, accs):
        r0 = pl.multiple_of(i * ADA_CHUNK, ADA_CHUNK)
        w = w_ref[0, pl.ds(r0, ADA_CHUNK), :]
        out = []
        for r in range(n_rows):
            a = act_ref[r, pl.ds(r0, ADA_CHUNK), :]
            acc = accs[r]
            for s in range(ADA_CHUNK // SUBLANES):
                rows = slice(s * SUBLANES, (s + 1) * SUBLANES)
                acc = acc + w[rows] * jnp.concatenate([a[rows]] * (tn // LANES), axis=1)
            out.append(acc)
        return tuple(out)

    zero = jnp.zeros((SUBLANES, tn), F32)
    accs = lax.fori_loop(0, k // ADA_CHUNK, body, (zero,) * n_rows)
    bias = b_ref[0]
    for r in range(n_rows):
        o_ref[0, r:r + 1, :] = jnp.sum(accs[r], axis=0, keepdims=True) + bias


def _ada(cond, w_ada, b_ada, tn=512):
    n_layers, d, n = w_ada.shape
    rows = cond.shape[0]
    cb = jnp.broadcast_to(cond[:, :, None], (rows, d, LANES))
    return pl.pallas_call(
        _ada_kernel,
        grid=(n_layers, n // tn),
        in_specs=[pl.BlockSpec((rows, d, LANES), lambda l, j: (0, 0, 0)),
                  pl.BlockSpec((1, d, tn), lambda l, j: (l, 0, j)),
                  pl.BlockSpec((1, 1, tn), lambda l, j: (l, 0, j))],
        out_specs=pl.BlockSpec((1, rows, tn), lambda l, j: (l, 0, j)),
        out_shape=jax.ShapeDtypeStruct((n_layers, rows, n), F32),
        scratch_shapes=[pltpu.VMEM((rows, d, LANES), F32)],
        compiler_params=_cparams(2),
        name="ada_mod",
    )(cb, w_ada, b_ada.reshape(n_layers, 1, n))


def _rms(x):
    return x * lax.rsqrt(jnp.mean(x * x, axis=-1, keepdims=True) + RMS_EPS)


def _group_of_tile(tr, seq):
    return lambda i: (jnp.minimum((i * tr) // seq, 2), 0, 0)


def _two_source_specs(x_lat, x_ctx, tr):
    d = x_lat.shape[1]
    nl = x_lat.shape[0] // tr
    return [pl.BlockSpec((tr, d), lambda i: (jnp.minimum(i, nl - 1), 0)),
            pl.BlockSpec((tr, d), lambda i: (jnp.maximum(i - nl, 0), 0))], nl


def _two_source_rows(lat_ref, ctx_ref, n_lat_tiles):
    return jnp.where(pl.program_id(0) < n_lat_tiles, lat_ref[...], ctx_ref[...])


def _norm_mod2_kernel(lat_ref, ctx_ref, g_ref, sh_ref, sc_ref, o_ref, *, n_lat_tiles):
    y = _rms(_two_source_rows(lat_ref, ctx_ref, n_lat_tiles)) * g_ref[...]
    o_ref[...] = (y * (1.0 + sc_ref[0]) + sh_ref[0]).astype(o_ref.dtype)


def _norm_mod(x_lat, x_ctx, g, sh, sc, seq, tr=256):
    d = x_lat.shape[1]
    t = x_lat.shape[0] + x_ctx.shape[0]
    grp = _group_of_tile(tr, seq)
    x_specs, nl = _two_source_specs(x_lat, x_ctx, tr)
    return pl.pallas_call(
        functools.partial(_norm_mod2_kernel, n_lat_tiles=nl),
        grid=(t // tr,),
        in_specs=x_specs + [pl.BlockSpec((1, d), lambda i: (0, 0)),
                            pl.BlockSpec((1, 1, d), grp),
                            pl.BlockSpec((1, 1, d), grp)],
        out_specs=pl.BlockSpec((tr, d), lambda i: (i, 0)),
        out_shape=jax.ShapeDtypeStruct((t, d), BF16),
        compiler_params=_cparams(1),
        name="norm_mod",
    )(x_lat, x_ctx, g.reshape(1, d), sh, sc)


def _post_kernel(x_ref, *rest, with_next, with_router, with_pair, n_lat_tiles):
    if n_lat_tiles is None:
        x = x_ref[...]
    else:
        ctx_ref, *rest = rest
        x = _two_source_rows(x_ref, ctx_ref, n_lat_tiles)
    if with_pair:
        y0_ref, y1_ref, w_ref, gt_ref, gpost_ref, *rest = rest
        w = w_ref[...]
        y = w[:, 0:1] * y0_ref[...].astype(F32) + w[:, 1:2] * y1_ref[...].astype(F32)
    else:
        y_ref, gt_ref, gpost_ref, *rest = rest
        y = y_ref[...].astype(F32)
    xn = x + gt_ref[0] * (_rms(y) * gpost_ref[...])
    if not with_next:
        (xo_ref,) = rest
        xo_ref[...] = xn
        return
    if with_router:
        gpre_ref, sh_ref, sc_ref, wr_ref, xo_ref, ho_ref, lg_ref = rest
    else:
        gpre_ref, sh_ref, sc_ref, xo_ref, ho_ref = rest
    xo_ref[...] = xn
    h = (_rms(xn) * gpre_ref[...]) * (1.0 + sc_ref[0]) + sh_ref[0]
    ho_ref[...] = h.astype(ho_ref.dtype)
    if with_router:
        lg_ref[...] = jnp.dot(h, wr_ref[...], preferred_element_type=F32,
                              precision=lax.Precision.HIGHEST)


def _post(x_all, y, gt, g_post, seq, rows, nxt=None, router=None, tr=256):
    grp = _group_of_tile(tr, seq)
    if isinstance(x_all, tuple):
        d = x_all[0].shape[1]
        x_specs, n_lat_tiles = _two_source_specs(*x_all, tr)
        x_args = list(x_all)
    else:
        d = x_all.shape[1]
        x_specs, n_lat_tiles = [pl.BlockSpec((tr, d), lambda i: (i, 0))], None
        x_args = [x_all]
    row = pl.BlockSpec((tr, d), lambda i: (i, 0))
    vec = pl.BlockSpec((1, d), lambda i: (0, 0))
    mod = pl.BlockSpec((1, 1, d), grp)
    with_pair = isinstance(y, tuple)
    if with_pair:
        y0, y1, w = y
        in_specs = x_specs + [row, row, pl.BlockSpec((tr, w.shape[1]), lambda i: (i, 0)), mod, vec]
        args = x_args + [y0, y1, w, gt, g_post.reshape(1, d)]
    else:
        in_specs = x_specs + [row, mod, vec]
        args = x_args + [y, gt, g_post.reshape(1, d)]
    out_specs = [row]
    out_shape = [jax.ShapeDtypeStruct((rows, d), F32)]
    if nxt is not None:
        g_pre, sh, sc = nxt
        in_specs += [vec, mod, mod]
        args += [g_pre.reshape(1, d), sh, sc]
        out_specs.append(row)
        out_shape.append(jax.ShapeDtypeStruct((rows, d), BF16))
        if router is not None:
            in_specs.append(pl.BlockSpec((d, LANES), lambda i: (0, 0)))
            args.append(router)
            out_specs.append(pl.BlockSpec((tr, LANES), lambda i: (i, 0)))
            out_shape.append(jax.ShapeDtypeStruct((rows, LANES), F32))
    return pl.pallas_call(
        functools.partial(_post_kernel, with_next=nxt is not None, with_router=router is not None,
                          with_pair=with_pair, n_lat_tiles=n_lat_tiles),
        grid=(rows // tr,),
        in_specs=in_specs,
        out_specs=out_specs,
        out_shape=out_shape,
        compiler_params=_cparams(1),
        name="post_norm",
    )(*args)


def _weight_changed(be_ref, m):
    return jnp.logical_or(m == 0, be_ref[m] != be_ref[jnp.maximum(m - 1, 0)])


def _gmm_kernel(be_ref, nv_ref, a_ref, w_ref, o_ref, wb_ref):
    m = pl.program_id(1)

    @pl.when(m < nv_ref[0])
    def _():
        @pl.when(_weight_changed(be_ref, m))
        def _():
            wb_ref[...] = w_ref[0].astype(BF16)

        o_ref[...] = jnp.dot(a_ref[...], wb_ref[...],
                             preferred_element_type=F32).astype(o_ref.dtype)

    @pl.when(m >= nv_ref[0])
    def _():
        o_ref[...] = jnp.zeros(o_ref.shape, o_ref.dtype)


def _swiglu_kernel(be_ref, nv_ref, a_ref, w1_ref, w3_ref, o_ref, w1b_ref, w3b_ref):
    m = pl.program_id(1)

    @pl.when(m < nv_ref[0])
    def _():
        @pl.when(_weight_changed(be_ref, m))
        def _():
            w1b_ref[...] = w1_ref[0].astype(BF16)
            w3b_ref[...] = w3_ref[0].astype(BF16)

        a = a_ref[...]
        g = jnp.dot(a, w1b_ref[...], preferred_element_type=F32)
        u = jnp.dot(a, w3b_ref[...], preferred_element_type=F32)
        o_ref[...] = (g * jax.nn.sigmoid(g) * u).astype(o_ref.dtype)

    @pl.when(m >= nv_ref[0])
    def _():
        o_ref[...] = jnp.zeros(o_ref.shape, o_ref.dtype)


def _gmm(a, ws, ids, *, k, n, tm, tn, out_dtype, w_col_off=0, rows=None, w_single_buffer=False,
         name="gmm"):
    rows = a.shape[0] if rows is None else rows
    assert rows % tm == 0 and n % tn == 0 and w_col_off % tn == 0
    off = w_col_off // tn
    w_mode = dict(pipeline_mode=pl.Buffered(1)) if w_single_buffer else {}

    def a_map(j, m, be_ref, nv_ref):
        return (jnp.minimum(m, nv_ref[0] - 1), 0)

    def w_map(j, m, be_ref, nv_ref):
        return (be_ref[jnp.minimum(m, nv_ref[0] - 1)], 0, j + off)

    def o_map(j, m, be_ref, nv_ref):
        return (m, j)

    kernel = _gmm_kernel if len(ws) == 1 else _swiglu_kernel
    return pl.pallas_call(
        kernel,
        grid_spec=pltpu.PrefetchScalarGridSpec(
            num_scalar_prefetch=2,
            grid=(n // tn, rows // tm),
            in_specs=[pl.BlockSpec((tm, k), a_map)] + [pl.BlockSpec((1, k, tn), w_map, **w_mode)] * len(ws),
            out_specs=pl.BlockSpec((tm, tn), o_map),
            scratch_shapes=[pltpu.VMEM((k, tn), BF16)] * len(ws)),
        out_shape=jax.ShapeDtypeStruct((rows, n), out_dtype),
        compiler_params=_cparams(2),
        name=name,
    )(*ids, a, *ws)


def _dense_ids(rows, tm, idx):
    nb = rows // tm
    return jnp.full((nb,), idx, jnp.int32), jnp.full((1,), nb, jnp.int32)


def _merge_kernel(att_ref, hy_ref, ga_ref, gh_ref, wa_ref, wh_ref, o_ref, wab_ref, whb_ref):
    @pl.when(pl.program_id(1) == 0)
    def _():
        wab_ref[...] = wa_ref[0].astype(BF16)
        whb_ref[...] = wh_ref[0].astype(BF16)

    pa = jnp.dot(att_ref[...], wab_ref[...], preferred_element_type=F32)
    ph = jnp.dot(hy_ref[...], whb_ref[...], preferred_element_type=F32)
    ga = jax.nn.sigmoid(ga_ref[...].astype(F32))
    gh = jax.nn.sigmoid(gh_ref[...].astype(F32))
    o_ref[...] = (ga * pa + gh * ph).astype(o_ref.dtype)


def _merge(att, hy, u, w_ao, w_ho, layer, rows, ga_off, gh_off, tm, tn=512):
    ka, d = w_ao.shape[1:]
    kh = w_ho.shape[1]
    return pl.pallas_call(
        _merge_kernel,
        grid=(d // tn, rows // tm),
        in_specs=[pl.BlockSpec((tm, ka), lambda j, m: (m, 0)),
                  pl.BlockSpec((tm, kh), lambda j, m: (m, 0)),
                  pl.BlockSpec((tm, tn), lambda j, m: (m, ga_off // tn + j)),
                  pl.BlockSpec((tm, tn), lambda j, m: (m, gh_off // tn + j)),
                  pl.BlockSpec((1, ka, tn), lambda j, m: (layer, 0, j)),
                  pl.BlockSpec((1, kh, tn), lambda j, m: (layer, 0, j))],
        out_specs=pl.BlockSpec((tm, tn), lambda j, m: (m, j)),
        out_shape=jax.ShapeDtypeStruct((rows, d), BF16),
        scratch_shapes=[pltpu.VMEM((ka, tn), BF16), pltpu.VMEM((kh, tn), BF16)],
        compiler_params=_cparams(2),
        name="gated_merge",
    )(att, hy, u, u, w_ao, w_ho)


def _rope_tables(seq):
    rows = seq // GRID_W
    row = jnp.repeat(jnp.arange(rows), GRID_W).astype(F32)
    col = jnp.tile(jnp.arange(GRID_W), rows).astype(F32)
    inv = ROPE_BASE ** (-jnp.arange(ROPE_FREQS, dtype=F32) / ROPE_FREQS)
    ang = jnp.stack([row[:, None] * inv, col[:, None] * inv], axis=1)
    cos, sin = jnp.cos(ang), jnp.sin(ang)
    zero = jnp.zeros_like(sin)
    cos_t = jnp.stack([cos, cos], axis=2).reshape(seq, HEAD_DIM)
    s_lo = jnp.stack([-sin, zero], axis=2).reshape(seq, HEAD_DIM)
    s_hi = jnp.stack([zero, sin], axis=2).reshape(seq, HEAD_DIM)
    return cos_t, s_lo, s_hi


def _rope_tables_for_rows(seq, batch, n_extra):
    cos_t, s_lo, s_hi = (jnp.tile(t, (batch, 1)) for t in _rope_tables(seq))
    if n_extra:
        cos_t = jnp.concatenate([cos_t, jnp.ones((n_extra, HEAD_DIM), F32)], axis=0)
        pad = jnp.zeros((n_extra, HEAD_DIM), F32)
        s_lo, s_hi = jnp.concatenate([s_lo, pad], axis=0), jnp.concatenate([s_hi, pad], axis=0)
    return cos_t, s_lo, s_hi


def _in_proj_kernel(a_ref, w_ref, c_ref, lo_ref, hi_ref, o_ref, wb_ref, *, n_rope_tiles):
    j = pl.program_id(0)

    @pl.when(pl.program_id(1) == 0)
    def _():
        wb_ref[...] = w_ref[0].astype(BF16)

    acc = jnp.dot(a_ref[...], wb_ref[...], preferred_element_type=F32)

    @pl.when(j < n_rope_tiles)
    def _():
        c, lo, hi = c_ref[...], lo_ref[...], hi_ref[...]
        for hd in range(acc.shape[1] // HEAD_DIM):
            sl = slice(hd * HEAD_DIM, (hd + 1) * HEAD_DIM)
            x = acc[:, sl]
            up = pltpu.roll(x, HEAD_DIM - ROPE_FREQS, 1)
            dn = pltpu.roll(x, ROPE_FREQS, 1)
            o_ref[:, sl] = (x * c + up * lo + dn * hi).astype(o_ref.dtype)

    @pl.when(j >= n_rope_tiles)
    def _():
        o_ref[...] = acc.astype(o_ref.dtype)


def _in_proj(h, w_in, layer, tables, rows, tm, tn=512):
    k, n = w_in.shape[1:]
    assert rows % tm == 0 and n % tn == 0 and V_OFF % tn == 0
    n_rope_tiles = V_OFF // tn
    tab = pl.BlockSpec((tm, HEAD_DIM), lambda j, m: (jnp.where(j < n_rope_tiles, m, 0), 0))
    return pl.pallas_call(
        functools.partial(_in_proj_kernel, n_rope_tiles=n_rope_tiles),
        grid=(n // tn, rows // tm),
        in_specs=[pl.BlockSpec((tm, k), lambda j, m: (m, 0)),
                  pl.BlockSpec((1, k, tn), lambda j, m: (layer, 0, j)), tab, tab, tab],
        out_specs=pl.BlockSpec((tm, tn), lambda j, m: (m, j)),
        out_shape=jax.ShapeDtypeStruct((rows, n), BF16),
        scratch_shapes=[pltpu.VMEM((k, tn), BF16)],
        compiler_params=_cparams(2),
        name="in_proj",
    )(h, w_in, *tables)


ATTN_ROW_CHUNK = 32
LOG2_E = math.log2(math.e)


def _scores(q, k):
    return lax.dot_general(q, k, (((1,), (1,)), ((), ())), preferred_element_type=F32)


def _win_attn_kernel(sink_ref, q_ref, kp_ref, kc_ref, kn_ref, vp_ref, vc_ref, vn_ref,
                     kx_ref, vx_ref, o_ref, s_ref, p_ref, den_ref):
    n = pl.program_id(1)
    nb = pl.num_programs(1)
    scale = HEAD_DIM ** -0.5
    n_keys = 3 * BLOCK + kx_ref.shape[0]
    qi = lax.broadcasted_iota(jnp.int32, (BLOCK, n_keys), 0)
    kj = lax.broadcasted_iota(jnp.int32, (BLOCK, n_keys), 1)
    bad_prev = jnp.logical_and(kj < BLOCK, jnp.logical_or(kj < qi, n == 0))
    bad_next = jnp.logical_and(jnp.logical_and(kj >= 2 * BLOCK, kj < 3 * BLOCK),
                               jnp.logical_or(kj - 2 * BLOCK > qi, n == nb - 1))
    valid = jnp.logical_not(jnp.logical_or(bad_prev, bad_next))
    head_lanes = lambda head: slice(head * HEAD_DIM, (head + 1) * HEAD_DIM)
    for h in range(N_KV_HEADS):
        hs = head_lanes(h)
        keys = jnp.concatenate([kp_ref[:, hs], kc_ref[:, hs], kn_ref[:, hs], kx_ref[:, hs]], axis=0)
        for g in range(Q_GROUP):
            head = h * Q_GROUP + g
            s_ref[head] = jnp.where(valid, _scores(q_ref[:, head_lanes(head)], keys) * (scale * LOG2_E),
                                    MASK_VALUE)
    for head in range(N_Q_HEADS):
        sink = sink_ref[head // Q_GROUP, head % Q_GROUP] * LOG2_E
        for r0 in range(0, BLOCK, ATTN_ROW_CHUNK):
            rows = slice(r0, r0 + ATTN_ROW_CHUNK)
            s = s_ref[head, rows, :]
            mx = jnp.maximum(jnp.max(s, axis=-1, keepdims=True), sink)
            p = jnp.exp2(s - mx)
            den_ref[head, rows, :] = jnp.sum(p, axis=-1, keepdims=True) + jnp.exp2(sink - mx)
            p_ref[head, rows, :] = p.astype(BF16)
    for h in range(N_KV_HEADS):
        hs = head_lanes(h)
        vals = jnp.concatenate([vp_ref[:, hs], vc_ref[:, hs], vn_ref[:, hs], vx_ref[:, hs]], axis=0)
        for g in range(Q_GROUP):
            head = h * Q_GROUP + g
            o = jnp.dot(p_ref[head], vals, preferred_element_type=F32)
            o_ref[:, head_lanes(head)] = (o / den_ref[head]).astype(o_ref.dtype)


def _win_attn(u, kv_ctx, ctx_row_blk, ctx_k_col, sink, batch, seq, n_ctx, out_rows):
    nb = seq // BLOCK
    assert K_OFF % KV_WIDTH == 0 and V_OFF % KV_WIDTH == 0 and ctx_k_col % KV_WIDTH == 0
    kcol = K_OFF // KV_WIDTH
    vcol = V_OFF // KV_WIDTH
    xk = ctx_k_col // KV_WIDTH

    def blk(shift, col):
        def index(b, n):
            return (b * nb + jnp.clip(n + shift, 0, nb - 1), col)
        return pl.BlockSpec((BLOCK, KV_WIDTH), index)

    return pl.pallas_call(
        _win_attn_kernel,
        grid=(batch, nb),
        in_specs=[pl.BlockSpec(memory_space=pltpu.SMEM),
                  pl.BlockSpec((BLOCK, ATT_WIDTH), lambda b, n: (b * nb + n, 0)),
                  blk(-1, kcol), blk(0, kcol), blk(1, kcol),
                  blk(-1, vcol), blk(0, vcol), blk(1, vcol),
                  pl.BlockSpec((n_ctx, KV_WIDTH), lambda b, n: (ctx_row_blk + b, xk)),
                  pl.BlockSpec((n_ctx, KV_WIDTH), lambda b, n: (ctx_row_blk + b, xk + 1))],
        out_specs=pl.BlockSpec((BLOCK, ATT_WIDTH), lambda b, n: (b * nb + n, 0)),
        out_shape=jax.ShapeDtypeStruct((out_rows, ATT_WIDTH), BF16),
        scratch_shapes=[pltpu.VMEM((N_Q_HEADS, BLOCK, 3 * BLOCK + n_ctx), F32),
                        pltpu.VMEM((N_Q_HEADS, BLOCK, 3 * BLOCK + n_ctx), BF16),
                        pltpu.VMEM((N_Q_HEADS, BLOCK, 1), F32)],
        compiler_params=_cparams(2),
        name="window_attention",
    )(sink.reshape(N_KV_HEADS, Q_GROUP), u, u, u, u, u, u, u, kv_ctx, kv_ctx)


def _ctx_attn_kernel(sink_ref, q_ref, k_ref, v_ref, att_ref, o_ref):
    del att_ref
    h = pl.program_id(1)
    scale = HEAD_DIM ** -0.5
    k, v = k_ref[...], v_ref[...]
    for g in range(Q_GROUP):
        sl = slice(g * HEAD_DIM, (g + 1) * HEAD_DIM)
        s = _scores(q_ref[:, sl], k) * scale
        sink = sink_ref[h, g]
        mx = jnp.maximum(jnp.max(s, axis=-1, keepdims=True), sink)
        p = jnp.exp(s - mx)
        den = jnp.sum(p, axis=-1, keepdims=True) + jnp.exp(sink - mx)
        o = jnp.dot(p.astype(BF16), v, preferred_element_type=F32)
        o_ref[:, sl] = (o / den).astype(o_ref.dtype)


def _ctx_attn(u, att, sink, batch, seq, n_ctx):
    hd = HEAD_DIM
    qw = Q_GROUP * hd
    ctx_blk = (batch * seq) // n_ctx
    return pl.pallas_call(
        _ctx_attn_kernel,
        grid=(batch, N_KV_HEADS),
        in_specs=[pl.BlockSpec(memory_space=pltpu.SMEM),
                  pl.BlockSpec((n_ctx, qw), lambda b, h: (ctx_blk + b, h)),
                  pl.BlockSpec((n_ctx, hd), lambda b, h: (ctx_blk + b, K_OFF // hd + h)),
                  pl.BlockSpec((n_ctx, hd), lambda b, h: (ctx_blk + b, V_OFF // hd + h)),
                  pl.BlockSpec(memory_space=pl.ANY)],
        out_specs=pl.BlockSpec((n_ctx, qw), lambda b, h: (ctx_blk + b, h)),
        out_shape=jax.ShapeDtypeStruct(att.shape, att.dtype),
        input_output_aliases={4: 0},
        compiler_params=_cparams(2),
        name="context_attention",
    )(sink.reshape(N_KV_HEADS, Q_GROUP), u, u, u, att)


def _short_conv_kernel(u_ref, w_ref, b_ref, o_ref):
    x = u_ref[...].astype(F32)
    n = x.shape[0]
    r = lax.broadcasted_iota(jnp.int32, x.shape, 0)
    prev = jnp.where(r == 0, 0.0, pltpu.roll(x, 1, 0))
    nxt = jnp.where(r == n - 1, 0.0, pltpu.roll(x, n - 1, 0))
    w = w_ref[0]
    o_ref[...] = (prev * w[0:1] + x * w[1:2] + nxt * w[2:3] + b_ref[0]).astype(o_ref.dtype)


def _short_conv(u, conv_w, conv_b, layer, n_seq, seg, row_blk_off, tw=256):
    width = conv_w.shape[-1]
    cb = conv_b.reshape(conv_b.shape[0], 1, width)
    return pl.pallas_call(
        _short_conv_kernel,
        grid=(n_seq, width // tw),
        in_specs=[pl.BlockSpec((seg, tw), lambda s, j: (row_blk_off + s, HY_OFF // tw + j)),
                  pl.BlockSpec((1, SHORT_CONV, tw), lambda s, j: (layer, 0, j)),
                  pl.BlockSpec((1, 1, tw), lambda s, j: (layer, 0, j))],
        out_specs=pl.BlockSpec((seg, tw), lambda s, j: (s, j)),
        out_shape=jax.ShapeDtypeStruct((n_seq * seg, width), BF16),
        compiler_params=_cparams(2),
        name="short_conv",
    )(u, conv_w, cb)


def _filter_positions(n, n_fft):
    t = jnp.linspace(0.0, 1.0, n, dtype=F32)[:, None]
    w = 2.0 * math.pi * jnp.arange(n, dtype=F32)[:, None] / n
    bands = jnp.linspace(1e-4, FILTER_BANDS - 1, FILTER_BANDS, dtype=F32)[None, :]
    z = jnp.concatenate([t, jnp.cos(bands * w), -jnp.sin(bands * w)], axis=-1)
    zt = jnp.concatenate([z, t], axis=-1)
    mid = jnp.zeros((n_fft - 2 * n + 1, zt.shape[1]), F32)
    full = jnp.concatenate([zt, mid, zt[1:][::-1]], axis=0)
    feat = jnp.pad(full[:, :-1], ((0, 0), (0, FILTER_HIDDEN - (zt.shape[1] - 1))))
    return feat, full[:, -1:]


def _filter_kernel(z_ref, t_ref, w1_ref, b1_ref, w2_ref, b2_ref, w3_ref, b3_ref, fr_ref,
                   wo_ref, dl_ref, o_ref, hid_ref, *, n, n_fft, row_group):
    hi = lax.Precision.HIGHEST
    tr = z_ref.shape[0]

    @pl.when(pl.program_id(1) == 0)
    def _():
        fr = fr_ref[...]
        h = jnp.sin(fr * (jnp.dot(z_ref[...], w1_ref[...], preferred_element_type=F32, precision=hi)
                          + b1_ref[...]))
        h = jnp.sin(fr * (jnp.dot(h, w2_ref[...], preferred_element_type=F32, precision=hi)
                          + b2_ref[...]))
        h = jnp.sin(fr * (jnp.dot(h, w3_ref[...], preferred_element_type=F32, precision=hi)
                          + b3_ref[...]))
        h_hi = h.astype(BF16)
        hid_ref[0] = h_hi
        hid_ref[1] = (h - h_hi.astype(F32)).astype(BF16)

    h_hi, h_lo = hid_ref[0], hid_ref[1]
    row = pl.program_id(0) * tr + lax.broadcasted_iota(jnp.int32, (tr, 1), 0)
    live = jnp.logical_or(row < n, row > n_fft - n)
    decay = jnp.where(live, jnp.exp(-t_ref[...] * dl_ref[...]), 0.0)
    for o in range(HY_ORDER):
        w = wo_ref[o, 0]
        w_hi = w.astype(BF16)
        w_lo = (w - w_hi.astype(F32)).astype(BF16)
        acc = (jnp.dot(h_hi, w_hi, preferred_element_type=F32)
               + jnp.dot(h_lo, w_hi, preferred_element_type=F32)
               + jnp.dot(h_hi, w_lo, preferred_element_type=F32)) * decay
        if row_group is None:
            o_ref[o] = acc
        else:
            o_ref[o] = jnp.zeros(o_ref.shape[1:], F32)
            for g in range(tr // row_group):
                o_ref[o, g * _pitch(row_group):g * _pitch(row_group) + row_group, :] = (
                    acc[g * row_group:(g + 1) * row_group])


def _hyena_filter(n, n_fft, w1, b1, w2, b2, w3, b3, freq, w_out, row_group=None, tw=512):
    tr = min(512, n)
    assert n % tr == 0 and n_fft % tr == 0
    tr_out, rows_out = tr, n_fft
    if row_group is not None:
        assert tr % row_group == 0
        tr_out, rows_out = (tr // row_group) * _pitch(row_group), (n_fft // row_group) * _pitch(row_group)
    feat, tpos = _filter_positions(n, n_fft)
    hid = FILTER_HIDDEN
    w1p = jnp.pad(w1, ((0, hid - w1.shape[0]), (0, 0)))
    wo = w_out.reshape(hid, HY_ORDER, 2, HY_WIDTH).transpose(1, 2, 0, 3)
    deltas = jnp.abs(jnp.linspace(MIN_DECAY, MAX_DECAY, HY_WIDTH, dtype=F32)).reshape(1, HY_WIDTH)
    small = lambda shape: pl.BlockSpec(shape, lambda r, j: (0,) * len(shape))
    return pl.pallas_call(
        functools.partial(_filter_kernel, n=n, n_fft=n_fft, row_group=row_group),
        grid=(n_fft // tr, HY_WIDTH // tw),
        in_specs=[pl.BlockSpec((tr, hid), lambda r, j: (r, 0)),
                  pl.BlockSpec((tr, 1), lambda r, j: (r, 0)),
                  small((hid, hid)), small((1, hid)), small((hid, hid)), small((1, hid)),
                  small((hid, hid)), small((1, hid)), small((1, hid)),
                  pl.BlockSpec((HY_ORDER, 1, hid, tw), lambda r, j: (0, jnp.where(r * tr >= n, 1, 0), 0, j)),
                  pl.BlockSpec((1, tw), lambda r, j: (0, j))],
        out_specs=pl.BlockSpec((HY_ORDER, tr_out, tw), lambda r, j: (0, r, j)),
        out_shape=jax.ShapeDtypeStruct((HY_ORDER, rows_out, HY_WIDTH), F32),
        scratch_shapes=[pltpu.VMEM((2, tr, hid), BF16)],
        compiler_params=_cparams(2),
        name="hyena_filter",
    )(feat, tpos, w1p, b1.reshape(1, hid), w2, b2.reshape(1, hid), w3, b3.reshape(1, hid),
      freq.reshape(1, hid), wo, deltas)


@functools.lru_cache(maxsize=None)
def _dft_constants(n_fft, no, ni):
    jo = np.arange(no)
    k1 = np.arange(no)
    half = no // 2
    f1r, f1d, f1i = [], [], []
    for i in range(ni):
        ang = 2.0 * np.pi * np.outer(k1, jo * ni + i) / n_fft
        c, s = np.cos(ang), np.sin(ang)
        f1r.append(np.concatenate([c, -s], axis=0))
        ch, sh = c[:, :half], s[:, :half]
        f1d.append(np.block([[ch, sh], [-sh, ch]]))
        angi = 2.0 * np.pi * np.outer(np.arange(half) * ni + i, k1) / n_fft
        ci, si = np.cos(angi), np.sin(angi)
        f1i.append(np.block([[ci, -si], [si, ci]]))
    ang3 = 2.0 * np.pi * np.outer(np.arange(ni), np.arange(ni)) / ni
    c3, s3 = np.cos(ang3), np.sin(ang3)
    m3 = np.block([[c3, s3], [-s3, c3]])
    m3c = np.block([[c3, -s3], [s3, c3]])
    as_f32 = lambda a: np.asarray(a, np.float32)
    return (as_f32(np.stack(f1r)), as_f32(np.stack(f1d)), as_f32(np.stack(f1i)),
            as_f32(m3), as_f32(m3c))


def _mxu_operand(const):
    return jnp.asarray(const, F32).astype(BF16)


def _const_spec(shape, n_axes):
    zeros = (0,) * len(shape)
    if n_axes == 2:
        index = lambda a, b: zeros
    else:
        index = lambda a: zeros
    return pl.BlockSpec(shape, index, pipeline_mode=pl.Buffered(1))


DFT_UNROLL = 32


def _pitch(size):
    return size + SUBLANES


def _block_at(i, size):
    return pl.ds(pl.multiple_of(i * _pitch(size), SUBLANES), size)


def _dft_stage3_rhs(a_ref, k1, no, ni):
    re = a_ref[pl.ds(k1, ni, stride=_pitch(2 * no)), :]
    im = a_ref[pl.ds(no + k1, ni, stride=_pitch(2 * no)), :]
    return jnp.concatenate([re, im], axis=0).astype(BF16)


def _dft_stage3_rhs_pair(a_ref, pair, no, ni):
    return jnp.concatenate([_dft_stage3_rhs(a_ref, 2 * pair, no, ni),
                            _dft_stage3_rhs(a_ref, 2 * pair + 1, no, ni)], axis=1)


def _spectrum_kernel(h_ref, f1r_ref, m3_ref, o_ref, a_ref, *, no, ni):
    inv_n = 1.0 / (no * ni)
    tc = h_ref.shape[-1]

    def stage1(ji, carry):
        rhs = h_ref[0, pl.ds(ji, no, stride=_pitch(ni)), :].astype(BF16)
        a_ref[_block_at(ji, 2 * no), :] = jnp.dot(f1r_ref[ji], rhs, preferred_element_type=F32)
        return carry

    lax.fori_loop(0, ni, stage1, 0, unroll=DFT_UNROLL)

    def stage3(pair, carry):
        x = jnp.dot(m3_ref[...], _dft_stage3_rhs_pair(a_ref, pair, no, ni),
                    preferred_element_type=F32) * inv_n
        r0 = pl.multiple_of(pair * 2 * ni, 2 * ni)
        for i in range(2):
            lanes = slice(i * tc, (i + 1) * tc)
            o_ref[0, 0, pl.ds(r0 + i * ni, ni), :] = x[:ni, lanes].astype(o_ref.dtype)
            o_ref[0, 1, pl.ds(r0 + i * ni, ni), :] = x[ni:, lanes].astype(o_ref.dtype)
        return carry

    lax.fori_loop(0, no // 2, stage3, 0, unroll=DFT_UNROLL // 2)


def _filter_spectrum(hfull, no=FFT_NO, ni=FFT_NI, tc=HY_TC):
    n_ord, rows, width = hfull.shape
    n_fft = no * ni
    assert rows == no * _pitch(ni)
    f1r, _, _, m3, _ = map(_mxu_operand, _dft_constants(n_fft, no, ni))
    return pl.pallas_call(
        functools.partial(_spectrum_kernel, no=no, ni=ni),
        grid=(n_ord, width // tc),
        in_specs=[pl.BlockSpec((1, rows, tc), lambda o, c: (o, 0, c)),
                  _const_spec(f1r.shape, 2), _const_spec(m3.shape, 2)],
        out_specs=pl.BlockSpec((1, 2, n_fft, tc), lambda o, c: (o, 0, 0, c)),
        out_shape=jax.ShapeDtypeStruct((n_ord, 2, n_fft, width), BF16),
        scratch_shapes=[pltpu.VMEM((ni * _pitch(2 * no), tc), F32)],
        compiler_params=_cparams(2),
        name="hyena_spectrum",
    )(hfull, f1r, m3)


def _long_conv_kernel(v_ref, g_ref, spec_ref, bias_ref, f1d_ref, f1i_ref, m3_ref, m3c_ref,
                      o_ref, z_ref, a_ref, b_ref, *, n, no, ni):
    order = pl.program_id(1)
    half = no // 2
    assert n == half * ni

    @pl.when(order == 0)
    def _():
        def load(jo, carry):
            r0 = pl.multiple_of(jo * ni, ni)
            for b in range(2):
                z_ref[b, _block_at(jo, ni), :] = v_ref[pl.ds(b * n + r0, ni), :].astype(F32)
            return carry

        lax.fori_loop(0, half, load, 0, unroll=DFT_UNROLL)

    def stage1(ji, carry):
        zr = z_ref[0, pl.ds(ji, half, stride=_pitch(ni)), :]
        zi = z_ref[1, pl.ds(ji, half, stride=_pitch(ni)), :]
        rhs = jnp.concatenate([zr, zi], axis=0).astype(BF16)
        a_ref[_block_at(ji, 2 * no), :] = jnp.dot(f1d_ref[ji], rhs, preferred_element_type=F32)
        return carry

    lax.fori_loop(0, ni, stage1, 0, unroll=DFT_UNROLL)

    def stage3(pair, carry):
        x = jnp.dot(m3_ref[...], _dft_stage3_rhs_pair(a_ref, pair, no, ni), preferred_element_type=F32)
        r0 = pl.multiple_of(pair * 2 * ni, 2 * ni)
        side_by_side = lambda s: jnp.concatenate([s[:ni], s[ni:]], axis=1).astype(F32)
        hr = side_by_side(spec_ref[0, 0, pl.ds(r0, 2 * ni), :])
        hi = side_by_side(spec_ref[0, 1, pl.ds(r0, 2 * ni), :])
        xr, xi = x[:ni], x[ni:]
        y = jnp.concatenate([xr * hr - xi * hi, xr * hi + xi * hr], axis=0).astype(BF16)
        b = jnp.dot(m3c_ref[...], y, preferred_element_type=F32)
        tc = b.shape[1] // 2
        b_ref[_block_at(2 * pair, 2 * ni), :] = b[:, :tc]
        b_ref[_block_at(2 * pair + 1, 2 * ni), :] = b[:, tc:]
        return carry

    lax.fori_loop(0, no // 2, stage3, 0, unroll=DFT_UNROLL // 2)

    def stage1_inv(t2, carry):
        br = b_ref[pl.ds(t2, no, stride=_pitch(2 * ni)), :]
        bi = b_ref[pl.ds(ni + t2, no, stride=_pitch(2 * ni)), :]
        rhs = jnp.concatenate([br, bi], axis=0).astype(BF16)
        a_ref[pl.ds(pl.multiple_of(t2 * _pitch(2 * no), SUBLANES), no), :] = jnp.dot(
            f1i_ref[t2], rhs, preferred_element_type=F32)
        return carry

    lax.fori_loop(0, ni, stage1_inv, 0, unroll=DFT_UNROLL)

    bias = bias_ref[0, 0]

    def finish(t1, carry):
        r0 = pl.multiple_of(t1 * ni, ni)
        conv = (a_ref[pl.ds(t1, ni, stride=_pitch(2 * no)), :],
                a_ref[pl.ds(half + t1, ni, stride=_pitch(2 * no)), :])
        for b in range(2):
            z = z_ref[b, _block_at(t1, ni), :]
            gate = g_ref[pl.ds(b * n + r0, ni), :].astype(F32)
            zn = gate * (conv[b] + z * bias)
            z_ref[b, _block_at(t1, ni), :] = zn
            o_ref[pl.ds(b * n + r0, ni), :] = zn.astype(o_ref.dtype)
        return carry

    lax.fori_loop(0, half, finish, 0, unroll=DFT_UNROLL)


def _long_conv(uc, spec, hy_bias, layer, n, out_rows, no=FFT_NO, ni=FFT_NI, tc=HY_TC):
    n_fft = no * ni
    assert 2 * n == n_fft
    _, f1d, f1i, m3, m3c = map(_mxu_operand, _dft_constants(n_fft, no, ni))
    nct = HY_WIDTH // tc
    bias = hy_bias.reshape(hy_bias.shape[0], HY_ORDER, 1, HY_WIDTH)
    return pl.pallas_call(
        functools.partial(_long_conv_kernel, n=n, no=no, ni=ni),
        grid=(nct, HY_ORDER),
        in_specs=[pl.BlockSpec((2 * n, tc), lambda c, o: (0, c)),
                  pl.BlockSpec((2 * n, tc), lambda c, o: (0, (1 + o) * nct + c)),
                  pl.BlockSpec((1, 2, n_fft, tc), lambda c, o: (o, 0, 0, c)),
                  pl.BlockSpec((1, 1, 1, tc), lambda c, o: (layer, o, 0, c)),
                  _const_spec(f1d.shape, 2), _const_spec(f1i.shape, 2),
                  _const_spec(m3.shape, 2), _const_spec(m3c.shape, 2)],
        out_specs=pl.BlockSpec((2 * n, tc), lambda c, o: (0, c)),
        out_shape=jax.ShapeDtypeStruct((out_rows, HY_WIDTH), BF16),
        scratch_shapes=[pltpu.VMEM((2, (no // 2) * _pitch(ni), tc), F32),
                        pltpu.VMEM((ni * _pitch(2 * no), tc), F32),
                        pltpu.VMEM((no * _pitch(2 * ni), tc), F32)],
        compiler_params=_cparams(2),
        name="hyena_long_conv",
    )(uc, uc, spec, bias, f1d, f1i, m3, m3c)


@functools.lru_cache(maxsize=None)
def _small_dft_constants(n):
    n_fft = 2 * n
    k = np.arange(n_fft)
    ang = 2.0 * np.pi * np.outer(k, np.arange(n_fft)) / n_fft
    c, s = np.cos(ang), np.sin(ang)
    fr = np.concatenate([c, -s], axis=0)
    ch, sh = c[:, :n], s[:, :n]
    fd = np.block([[ch, sh], [-sh, ch]])
    ci, si = c[:n, :], s[:n, :]
    fi = np.block([[ci, -si], [si, ci]])
    as_f32 = lambda a: np.asarray(a, np.float32)
    return as_f32(fr), as_f32(fd), as_f32(fi)


def _small_conv_kernel(v_ref, g1_ref, g2_ref, h_ref, bias_ref, fr_ref, fd_ref, fi_ref, hy_ref, o_ref,
                       *, n):
    del hy_ref
    n_fft = 2 * n
    z = v_ref[...].astype(F32)
    for o, g_ref in enumerate((g1_ref, g2_ref)):
        hs = jnp.dot(fr_ref[...], h_ref[o].astype(BF16), preferred_element_type=F32) * (1.0 / n_fft)
        x = jnp.dot(fd_ref[...], z.astype(BF16), preferred_element_type=F32)
        xr, xi, hr, hi = x[:n_fft], x[n_fft:], hs[:n_fft], hs[n_fft:]
        y = jnp.concatenate([xr * hr - xi * hi, xr * hi + xi * hr], axis=0).astype(BF16)
        conv = jnp.dot(fi_ref[...], y, preferred_element_type=F32)
        z = g_ref[...].astype(F32) * (conv + z * bias_ref[0, o])
    o_ref[...] = z.astype(o_ref.dtype)


def _small_conv(uc, hfull, hy_bias, layer, n, hy, row_blk, tc=256):
    fr, fd, fi = map(_mxu_operand, _small_dft_constants(n))
    nct = HY_WIDTH // tc
    bias = hy_bias.reshape(hy_bias.shape[0], HY_ORDER, 1, HY_WIDTH)
    col = lambda k: pl.BlockSpec((2 * n, tc), lambda c: (0, k * nct + c))
    return pl.pallas_call(
        functools.partial(_small_conv_kernel, n=n),
        grid=(nct,),
        in_specs=[col(0), col(1), col(2),
                  pl.BlockSpec((HY_ORDER, 2 * n, tc), lambda c: (0, 0, c)),
                  pl.BlockSpec((1, HY_ORDER, 1, tc), lambda c: (layer, 0, 0, c)),
                  _const_spec(fr.shape, 1), _const_spec(fd.shape, 1), _const_spec(fi.shape, 1),
                  pl.BlockSpec(memory_space=pl.ANY)],
        out_specs=pl.BlockSpec((2 * n, tc), lambda c: (row_blk, c)),
        out_shape=jax.ShapeDtypeStruct(hy.shape, hy.dtype),
        input_output_aliases={8: 0},
        compiler_params=_cparams(1),
        name="hyena_small_conv",
    )(uc, uc, uc, hfull, bias, fr, fd, fi, hy)


def _hyena(u, conv_w, conv_b, filt, hy_bias, layer, seg, row_blk_off, hy=None, out_rows=None):
    uc = _short_conv(u, conv_w, conv_b, layer, 2, seg, row_blk_off)
    if 2 * seg == FFT_NO * FFT_NI:
        hfull = _hyena_filter(seg, 2 * seg, *filt, row_group=FFT_NI)
        return _long_conv(uc, _filter_spectrum(hfull), hy_bias, layer, seg, out_rows)
    assert row_blk_off % 2 == 0
    hfull = _hyena_filter(seg, 2 * seg, *filt)
    return _small_conv(uc, hfull, hy_bias, layer, seg, hy, row_blk_off // 2)


def _moe(h2, logits, w1, w3, w2):
    t, d = h2.shape
    dff = w1.shape[-1]
    top_logit, top_idx = lax.top_k(logits, TOP_K)
    gate = jax.nn.softmax(top_logit, axis=-1)
    n_assign = t * TOP_K
    flat_e = top_idx.reshape(-1)
    flat_tok = jnp.repeat(jnp.arange(t, dtype=jnp.int32), TOP_K)
    onehot = (flat_e[:, None] == jnp.arange(N_EXPERTS, dtype=flat_e.dtype)[None, :]).astype(jnp.int32)
    counts = jnp.sum(onehot, axis=0)
    rank = jnp.sum((jnp.cumsum(onehot, axis=0) - onehot) * onehot, axis=1)
    padded = (counts + MOE_ROWS - 1) // MOE_ROWS * MOE_ROWS
    pad_end = jnp.cumsum(padded)
    pad_start = pad_end - padded
    dest = (jnp.sum(onehot * pad_start[None, :], axis=1) + rank).astype(jnp.int32)
    n_blocks = n_assign // MOE_ROWS + N_EXPERTS
    n_slots = n_blocks * MOE_ROWS
    slot_tok = (jnp.arange(n_slots, dtype=jnp.int32) % t).at[dest].set(flat_tok)
    block_start = jnp.arange(n_blocks, dtype=pad_end.dtype) * MOE_ROWS
    block_expert = jnp.minimum(jnp.sum((pad_end[None, :] <= block_start[:, None]).astype(jnp.int32), axis=1),
                               N_EXPERTS - 1).astype(jnp.int32)
    n_used = (pad_end[-1] // MOE_ROWS).astype(jnp.int32).reshape(1)
    ids = (block_expert, n_used)
    xs = h2[slot_tok]
    act = _gmm(xs, (w1, w3), ids, k=d, n=dff, tm=MOE_ROWS, tn=512, out_dtype=BF16, name="moe_up")
    ys = _gmm(act, (w2,), ids, k=dff, n=d, tm=MOE_ROWS, tn=1024, out_dtype=BF16, name="moe_down")
    pos = dest.reshape(t, TOP_K)
    return ys[pos[:, 0]], ys[pos[:, 1]], gate


def kernel(x, c, ctx, c_ctx, w_ada, b_ada, norm_g, w_in, attn_sink, conv_w, conv_b,
           filt_w1, filt_b1, filt_w2, filt_b2, filt_w3, filt_b3, filt_freq, filt_w_out, hyena_bias,
           w_attn_out, w_hyena_out, w_out, ffn_w1, ffn_w3, ffn_w2,
           moe_router, moe_w1, moe_w3, moe_w2):
    batch, seq, d = x.shape
    n_ctx = ctx.shape[1]
    depth = w_in.shape[0]
    in_width = w_in.shape[-1]
    n_lat = batch * seq
    n_all = n_lat + batch * n_ctx
    n_fft = 2 * seq
    ga_off = HY_OFF + (HY_ORDER + 1) * HY_WIDTH
    gh_off = ga_off + d
    assert batch == 2 and FFT_NO * FFT_NI == n_fft

    x_all = (x.reshape(n_lat, d), ctx.reshape(batch * n_ctx, d))
    cond = jnp.concatenate([c, c_ctx[None]], axis=0)
    mod = _ada(cond, w_ada, b_ada)
    mod = mod.reshape(depth, batch + 1, 6, 1, d)
    tm_all = n_all // 8
    tm_lat = n_lat // 8

    def mods(l, j):
        return mod[l, :, j]

    h = _norm_mod(*x_all, norm_g[0, 0], mods(0, 0), mods(0, 1), seq)
    for l in range(depth):
        last = l == depth - 1
        rows = n_lat if last else n_all
        tm = tm_lat if last else tm_all
        filt = (filt_w1[l], filt_b1[l], filt_w2[l], filt_b2[l], filt_w3[l], filt_b3[l],
                filt_freq[l], filt_w_out[l])

        u = _in_proj(h, w_in, l, _rope_tables_for_rows(seq, batch, rows - n_lat), rows, tm)
        if last:
            kv_ctx = _gmm(h[n_lat:], (w_in,), _dense_ids(batch * n_ctx, batch * n_ctx, l), k=d,
                          n=HY_OFF - K_OFF, tm=batch * n_ctx, tn=512, out_dtype=BF16, w_col_off=K_OFF,
                          name="ctx_kv_proj")
            att = _win_attn(u, kv_ctx, 0, 0, attn_sink[l], batch, seq, n_ctx, rows)
        else:
            att = _win_attn(u, u, n_lat // n_ctx, K_OFF, attn_sink[l], batch, seq, n_ctx, rows)
        hy = _hyena(u, conv_w, conv_b, filt, hyena_bias, l, seq, 0, out_rows=rows)
        if not last:
            att = _ctx_attn(u, att, attn_sink[l], batch, seq, n_ctx)
            hy = _hyena(u, conv_w, conv_b, filt, hyena_bias, l, n_ctx, n_lat // n_ctx, hy=hy)
        mrg = _merge(att, hy, u, w_attn_out, w_hyena_out, l, rows, ga_off, gh_off, tm)
        y = _gmm(mrg, (w_out,), _dense_ids(rows, tm, l), k=d, n=d, tm=tm, tn=512,
                 out_dtype=BF16, name="out_proj")
        router = None
        if l % 2 == 1:
            router = jnp.pad(moe_router[l // 2], ((0, 0), (0, LANES - N_EXPERTS)))
        res = _post(x_all, y, mods(l, 2), norm_g[l, 1], seq, rows,
                    nxt=(norm_g[l, 2], mods(l, 3), mods(l, 4)), router=router)
        x_all, h2 = res[0], res[1]

        if l % 2 == 0:
            i = l // 2
            dff = ffn_w1.shape[-1]
            act = _gmm(h2, (ffn_w1, ffn_w3), _dense_ids(rows, tm, i), k=d, n=dff, tm=tm, tn=256,
                       out_dtype=BF16, name="ffn_up")
            f = _gmm(act, (ffn_w2,), _dense_ids(rows, tm // 4, i), k=dff, n=d, tm=tm // 4, tn=512,
                     out_dtype=BF16, w_single_buffer=True, name="ffn_down")
        else:
            i = l // 2
            f = _moe(h2, res[2][:, :N_EXPERTS], moe_w1[i], moe_w3[i], moe_w2[i])
        if last:
            (x_all,) = _post(x_all, f, mods(l, 5), norm_g[l, 3], seq, rows)
        else:
            x_all, h = _post(x_all, f, mods(l, 5), norm_g[l, 3], seq, rows,
                             nxt=(norm_g[l + 1, 0], mods(l + 1, 0), mods(l + 1, 1)))
    return x_all[:n_lat].reshape(batch, seq, d)
```
